```python
import jax, jax.numpy as jnp
from jax import lax
import numpy as np

D_MODEL = 1024
BATCH = 8
SEQ = 2048
DEPTH = 2
DEC_BATCH = 32
DEC_SEQ = 8
PAST_LEN = 8192
PAGE_SIZE = 128

N_MIXERS = 2
N_ATT_LAYERS = (DEPTH + 1) // 2
N_SG_LAYERS = DEPTH // 2
HEAD_DIM = 128
N_HEADS = D_MODEL // HEAD_DIM
N_KV_HEADS = 2
GROUP = N_HEADS // N_KV_HEADS
IDX_HEADS = 4
IDX_DIM = 64
IDX_SCALE = (IDX_DIM ** -0.5) * (IDX_HEADS ** -0.5)
TOPK_MAX = 256
Q_BLOCK = 64
CHUNK = 128
SG_WIDTH = D_MODEL
SG_GROUPS = 8
SG_GROUP_DIM = SG_WIDTH // SG_GROUPS
D_FF = -(-8 * D_MODEL // (3 * 256)) * 256
ATT_SIZES = (N_HEADS * HEAD_DIM, N_KV_HEADS * HEAD_DIM, N_KV_HEADS * HEAD_DIM,
             IDX_HEADS * IDX_DIM, IDX_DIM, IDX_HEADS)
ATT_IN = sum(ATT_SIZES)
ATT_SPLITS = tuple(int(c) for c in np.cumsum(ATT_SIZES)[:-1])
EPS = 1e-6

kernel_name = "dsa_gmlp_hybrid_decode_step"


def rms_norm(x, g):
    xf = x.astype(jnp.float32)
    y = xf * lax.rsqrt(jnp.mean(xf * xf, axis=-1, keepdims=True) + EPS)
    return (y * g.astype(jnp.float32)).astype(x.dtype)


def swiglu(h, w_gu, w_down):
    gate, up = jnp.split(h @ w_gu, 2, axis=-1)
    return (jax.nn.silu(gate) * up) @ w_down


def att_project(h, w_in, g_q, g_k):
    B, T, _ = h.shape
    q, k, v, qi, ki, wi = jnp.split(h @ w_in, ATT_SPLITS, axis=-1)
    q = rms_norm(q.reshape(B, T, N_KV_HEADS, GROUP, HEAD_DIM), g_q)
    k = rms_norm(k.reshape(B, T, N_KV_HEADS, HEAD_DIM), g_k)
    v = v.reshape(B, T, N_KV_HEADS, HEAD_DIM)
    qi = qi.reshape(B, T, IDX_HEADS, IDX_DIM)
    return q, k, v, qi, ki, wi


def dsa_block(q, qi, wi, q_pos, k_all, v_all, ki_all, topk):
    L = k_all.shape[1]
    key_pos = jnp.arange(L, dtype=jnp.int32)
    logits = jnp.einsum('bqhd,bsd->bqhs', qi.astype(jnp.float32), ki_all.astype(jnp.float32))
    score = jnp.einsum('bqh,bqhs->bqs', wi.astype(jnp.float32), jax.nn.relu(logits)) * IDX_SCALE
    admissible = key_pos[None, :] <= q_pos[:, None]
    score = jnp.where(admissible[None], score, -jnp.inf)
    _, sel = lax.top_k(score, topk)
    sel_valid = sel <= q_pos[None, :, None]
    k_sel = jax.vmap(lambda kv, i: kv[i])(k_all, sel)
    v_sel = jax.vmap(lambda kv, i: kv[i])(v_all, sel)
    s = jnp.einsum('bqhgd,bqkhd->bqhgk', q.astype(jnp.float32), k_sel.astype(jnp.float32)) * (HEAD_DIM ** -0.5)
    s = jnp.where(sel_valid[:, :, None, None, :], s, -jnp.inf)
    p = jax.nn.softmax(s, axis=-1)
    o = jnp.einsum('bqhgk,bqkhd->bqhgd', p, v_sel.astype(jnp.float32))
    return o.astype(q.dtype)


def dsa_prompt(q, qi, wi, k, v, ki):
    B, T = q.shape[:2]
    nb = T // Q_BLOCK
    topk = min(TOPK_MAX, T // 4)
    pos = jnp.arange(T, dtype=jnp.int32).reshape(nb, Q_BLOCK)

    def to_blocks(a):
        return jnp.moveaxis(a.reshape((B, nb, Q_BLOCK) + a.shape[2:]), 1, 0)

    def one_block(args):
        qb, qib, wib, pb = args
        return dsa_block(qb, qib, wib, pb, k, v, ki, topk)

    o = lax.map(one_block, (to_blocks(q), to_blocks(qi), to_blocks(wi), pos))
    return jnp.moveaxis(o, 0, 1).reshape(B, T, N_HEADS * HEAD_DIM)


def gather_pages(pool, page_table):
    g = pool[page_table]
    return g.reshape((g.shape[0], g.shape[1] * g.shape[2]) + g.shape[3:])


def dsa_sample(q, qi, wi, k, v, ki, ck, cv, cki, page_table):
    B, T = q.shape[:2]
    k_all = jnp.concatenate([gather_pages(ck, page_table), k], axis=1)
    v_all = jnp.concatenate([gather_pages(cv, page_table), v], axis=1)
    ki_all = jnp.concatenate([gather_pages(cki, page_table), ki], axis=1)
    topk = min(TOPK_MAX, (PAST_LEN + T) // 4)
    pos = PAST_LEN + jnp.arange(T, dtype=jnp.int32)
    o = dsa_block(q, qi, wi, pos, k_all, v_all, ki_all, topk)
    return o.reshape(B, T, N_HEADS * HEAD_DIM)


def sgu_mixer(h, w_in, g_v, w_s, b_s, w_out, chunk_len):
    B, T, _ = h.shape
    u, v = jnp.split(jax.nn.gelu(h @ w_in), 2, axis=-1)
    v = rms_norm(v, g_v)
    nc = T // chunk_len
    tril = jnp.tril(jnp.ones((chunk_len, chunk_len), dtype=w_s.dtype))
    ws = w_s[:, :chunk_len, :chunk_len] * tril[None]
    vc = v.reshape(B, nc, chunk_len, SG_GROUPS, SG_GROUP_DIM)
    mixed = jnp.einsum('gts,bcsgd->bctgd', ws, vc) + b_s[:, :chunk_len].T[None, None, :, :, None]
    out = (u * mixed.reshape(B, T, SG_WIDTH)) @ w_out
    return out, v


def setup_inputs(seed: int = 0) -> dict:
    key = jax.random.key(seed)
    ks = jax.random.split(key, 24)
    n_pages = PAST_LEN // PAGE_SIZE
    n_used = DEC_BATCH * n_pages
    n_pool = n_used + n_used // 4
    f32 = jnp.float32

    def nrm(k, shape, scale=1.0):
        return jax.random.normal(k, shape, f32) * scale

    perm = jax.random.permutation(ks[0], n_pool)
    page_table = perm[:n_used].reshape(DEC_BATCH, n_pages).astype(jnp.int32)
    return {
        "x_prompt": nrm(ks[1], (BATCH, SEQ, D_MODEL)),
        "x_sample": nrm(ks[2], (DEC_BATCH, DEC_SEQ, D_MODEL)),
        "cache_k": nrm(ks[3], (N_ATT_LAYERS, n_pool, PAGE_SIZE, N_KV_HEADS, HEAD_DIM)),
        "cache_v": nrm(ks[4], (N_ATT_LAYERS, n_pool, PAGE_SIZE, N_KV_HEADS, HEAD_DIM)),
        "cache_kidx": nrm(ks[5], (N_ATT_LAYERS, n_pool, PAGE_SIZE, IDX_DIM)),
        "page_table": page_table,
        "norm_mix": 1.0 + nrm(ks[6], (DEPTH, D_MODEL), 0.02),
        "norm_ffn": 1.0 + nrm(ks[7], (DEPTH, D_MODEL), 0.02),
        "att_w_in": nrm(ks[8], (N_ATT_LAYERS, D_MODEL, ATT_IN), D_MODEL ** -0.5),
        "att_q_norm": 1.0 + nrm(ks[9], (N_ATT_LAYERS, HEAD_DIM), 0.02),
        "att_k_norm": 1.0 + nrm(ks[10], (N_ATT_LAYERS, HEAD_DIM), 0.02),
        "att_w_out": nrm(ks[11], (N_ATT_LAYERS, N_HEADS * HEAD_DIM, D_MODEL), (N_HEADS * HEAD_DIM) ** -0.5),
        "sg_w_in": nrm(ks[12], (N_SG_LAYERS, D_MODEL, 2 * SG_WIDTH), D_MODEL ** -0.5),
        "sg_v_norm": 1.0 + nrm(ks[13], (N_SG_LAYERS, SG_WIDTH), 0.02),
        "sg_w_s": nrm(ks[14], (N_SG_LAYERS, SG_GROUPS, CHUNK, CHUNK), CHUNK ** -0.5),
        "sg_b_s": 1.0 + nrm(ks[15], (N_SG_LAYERS, SG_GROUPS, CHUNK), 0.1),
        "sg_w_out": nrm(ks[16], (N_SG_LAYERS, SG_WIDTH, D_MODEL), SG_WIDTH ** -0.5),
        "ffn_w_gu": nrm(ks[17], (DEPTH, D_MODEL, 2 * D_FF), D_MODEL ** -0.5),
        "ffn_w_down": nrm(ks[18], (DEPTH, D_FF, D_MODEL), D_FF ** -0.5),
    }


def reference(x_prompt, x_sample, cache_k, cache_v, cache_kidx, page_table,
              norm_mix, norm_ffn, att_w_in, att_q_norm, att_k_norm, att_w_out,
              sg_w_in, sg_v_norm, sg_w_s, sg_b_s, sg_w_out, ffn_w_gu, ffn_w_down):
    xp, xs = x_prompt, x_sample
    kp_rows, vp_rows, ip_rows = [], [], []
    ks_rows, vs_rows, is_rows = [], [], []
    sgv_rows = []
    for i in range(DEPTH):
        a = i // N_MIXERS
        hp = rms_norm(xp, norm_mix[i])
        hs = rms_norm(xs, norm_mix[i])
        if i % N_MIXERS == 0:
            q, k, v, qi, ki, wi = att_project(hp, att_w_in[a], att_q_norm[a], att_k_norm[a])
            mp = dsa_prompt(q, qi, wi, k, v, ki) @ att_w_out[a]
            kp_rows.append(k); vp_rows.append(v); ip_rows.append(ki)
            q, k, v, qi, ki, wi = att_project(hs, att_w_in[a], att_q_norm[a], att_k_norm[a])
            ms = dsa_sample(q, qi, wi, k, v, ki, cache_k[a], cache_v[a], cache_kidx[a], page_table) @ att_w_out[a]
            ks_rows.append(k); vs_rows.append(v); is_rows.append(ki)
        else:
            mp, _ = sgu_mixer(hp, sg_w_in[a], sg_v_norm[a], sg_w_s[a], sg_b_s[a], sg_w_out[a], CHUNK)
            ms, v_new = sgu_mixer(hs, sg_w_in[a], sg_v_norm[a], sg_w_s[a], sg_b_s[a], sg_w_out[a], xs.shape[1])
            sgv_rows.append(v_new)
        xp = xp + mp
        xs = xs + ms
        xp = xp + swiglu(rms_norm(xp, norm_ffn[i]), ffn_w_gu[i], ffn_w_down[i])
        xs = xs + swiglu(rms_norm(xs, norm_ffn[i]), ffn_w_gu[i], ffn_w_down[i])
    y_prompt, y_sample = xp, xs
    k_prompt = jnp.stack(kp_rows)
    v_prompt = jnp.stack(vp_rows)
    kidx_prompt = jnp.stack(ip_rows)
    k_sample = jnp.stack(ks_rows)
    v_sample = jnp.stack(vs_rows)
    kidx_sample = jnp.stack(is_rows)
    sgv_sample = jnp.stack(sgv_rows)
    return (y_prompt, y_sample, k_prompt, v_prompt, kidx_prompt, k_sample, v_sample, kidx_sample, sgv_sample)
```

```python
import functools

import jax
import jax.numpy as jnp
from jax import lax
from jax.experimental import pallas as pl
from jax.experimental.pallas import tpu as pltpu

D_MODEL = 1024
HEAD_DIM = 128
N_HEADS = 8
N_KV_HEADS = 2
GROUP = N_HEADS // N_KV_HEADS
IDX_HEADS = 4
IDX_DIM = 64
IDX_SCALE = (IDX_DIM ** -0.5) * (IDX_HEADS ** -0.5)
TOPK_MAX = 256
CHUNK = 128
SG_GROUPS = 8
D_FF = 2816
PAGE_SIZE = 128
EPS = 1e-6

Q_W = N_HEADS * HEAD_DIM
KV_W = N_KV_HEADS * HEAD_DIM
QI_W = IDX_HEADS * IDX_DIM
ATT_IN = Q_W + 2 * KV_W + QI_W + IDX_DIM + IDX_HEADS
LANES = 128
ATT_IN_PAD = -(-ATT_IN // LANES) * LANES
IDX_K = 4 * IDX_DIM
INT_MIN = -(2 ** 31)
NEG_INF = float("-inf")
VMEM_LIMIT = 56 * 1024 * 1024

bf16 = jnp.bfloat16
f32 = jnp.float32


def _rms(x, g):
    return x * lax.rsqrt(jnp.mean(x * x, axis=-1, keepdims=True) + EPS) * g


def _dot(a, b):
    return jnp.dot(a, b, preferred_element_type=f32)


def _dot_nt(a, b):
    return lax.dot_general(a, b, (((1,), (1,)), ((), ())), preferred_element_type=f32)


def _hi(x):
    return x.astype(bf16).astype(f32)


def _params(n_grid):
    return pltpu.CompilerParams(
        dimension_semantics=("arbitrary",) * n_grid, vmem_limit_bytes=VMEM_LIMIT)


def _const_spec(shape):
    nd = len(shape)
    return pl.BlockSpec(shape, lambda *_: (0,) * nd)


def _att_proj_kernel(x_ref, g_ref, w_ref, gq_ref, gk_ref,
                     q_ref, k_ref, v_ref, kb_ref, vb_ref, qic_ref, kic_ref, ki_ref, kiwi_ref):
    tm = x_ref.shape[0]
    h = _rms(x_ref[...], g_ref[...]).astype(bf16)
    z = _dot(h, w_ref[...])
    gq = gq_ref[...]
    gk = gk_ref[...]
    for j in range(N_HEADS):
        sl = slice(j * HEAD_DIM, (j + 1) * HEAD_DIM)
        q_ref[:, sl] = _rms(z[:, sl], gq).astype(bf16)
    for j in range(N_KV_HEADS):
        sl = slice(j * HEAD_DIM, (j + 1) * HEAD_DIM)
        kj = _rms(z[:, Q_W + j * HEAD_DIM:Q_W + (j + 1) * HEAD_DIM], gk)
        k_ref[:, sl] = kj
        kb_ref[:, sl] = kj.astype(bf16)
    v = z[:, Q_W + KV_W:Q_W + 2 * KV_W]
    v_ref[...] = v
    vb_ref[...] = v.astype(bf16)

    low_half = lax.broadcasted_iota(jnp.int32, (tm, LANES), 1) < IDX_DIM
    qi0 = Q_W + 2 * KV_W
    for p in range(IDX_HEADS // 2):
        pair = z[:, qi0 + p * LANES:qi0 + (p + 1) * LANES]
        rolled = pltpu.roll(pair, IDX_DIM, axis=1)
        for e, (own, other) in enumerate(((pair, rolled), (rolled, pair))):
            dup = jnp.where(low_half, own, other)
            single = jnp.where(low_half, own, 0.0)
            base = (2 * p + e) * IDX_K
            qic_ref[:, base:base + LANES] = dup.astype(bf16)
            qic_ref[:, base + LANES:base + 2 * LANES] = (single - _hi(single)).astype(bf16)
    last = z[:, qi0 + QI_W:qi0 + QI_W + LANES]
    rolled = pltpu.roll(last, IDX_DIM, axis=1)
    kic_ref[:, :LANES] = jnp.where(low_half, _hi(last), rolled - _hi(rolled)).astype(bf16)
    kic_ref[:, LANES:] = jnp.where(low_half, _hi(last), 0.0).astype(bf16)
    ki_ref[...] = last[:, :IDX_DIM]
    kiwi_ref[...] = last


def _att_proj(x, g, w, gq, gk, tm):
    t = x.shape[0]
    row = lambda width: pl.BlockSpec((tm, width), lambda i: (i, 0))
    outs = [(Q_W, bf16), (KV_W, f32), (KV_W, f32), (KV_W, bf16), (KV_W, bf16),
            (IDX_HEADS * IDX_K, bf16), (IDX_K, bf16), (IDX_DIM, f32), (LANES, f32)]
    return pl.pallas_call(
        _att_proj_kernel,
        grid=(t // tm,),
        in_specs=[row(D_MODEL), _const_spec((1, D_MODEL)), _const_spec((D_MODEL, ATT_IN_PAD)),
                  _const_spec((1, HEAD_DIM)), _const_spec((1, HEAD_DIM))],
        out_specs=[row(wd) for wd, _ in outs],
        out_shape=[jax.ShapeDtypeStruct((t, wd), dt) for wd, dt in outs],
        compiler_params=_params(1),
        name="att_proj",
    )(x, g, w, gq, gk)


def _sort_key(score, admissible):
    bits = lax.bitcast_convert_type(score, jnp.int32)
    key = jnp.where(bits < 0, bits ^ jnp.int32(0x7FFFFFFF), bits)
    return jnp.where(admissible, key, jnp.int32(INT_MIN))


def _count(mask):
    return jnp.sum(jnp.where(mask, 1.0, 0.0), axis=1, keepdims=True)


def _topk_bias(key_ref, tri_ref, before_ref, k_top):
    rows, n = key_ref.shape

    def step(i, t):
        cand = t + lax.shift_left(jnp.int32(1), jnp.int32(31) - i)
        return jnp.where(_count(key_ref[...] >= cand) >= k_top, cand, t)

    t = lax.fori_loop(0, 32, step, jnp.full((rows, 1), INT_MIN, jnp.int32))
    key = key_ref[...]
    need = k_top - _count(key > t)
    tie_b = jnp.where(key == t, 1.0, 0.0).astype(bf16)
    before = _dot(tie_b, before_ref[...])
    pieces = []
    for c in range(n // LANES):
        sl = slice(c * LANES, (c + 1) * LANES)
        key_c = key[:, sl]
        rank = _dot(tie_b[:, sl], tri_ref[...]) + before[:, c:c + 1]
        sel = (key_c > t) | ((key_c == t) & (rank <= need))
        pieces.append(jnp.where(sel & (key_c > INT_MIN), 0.0, NEG_INF))
    return jnp.concatenate(pieces, axis=1)


def _tie_consts(n):
    a = jnp.arange(LANES)
    tri = (a[:, None] <= a[None, :]).astype(bf16)
    before = ((jnp.arange(n)[:, None] // LANES) < a[None, :]).astype(bf16)
    return tri, before


def _indexer_scores(qic, wi, kic):
    score = None
    for h in range(IDX_HEADS):
        logit = _dot_nt(qic[:, h * IDX_K:(h + 1) * IDX_K], kic)
        term = wi[:, IDX_DIM + h:IDX_DIM + h + 1] * jnp.maximum(logit, 0.0)
        score = term if score is None else score + term
    return score * IDX_SCALE


def _dsa_prompt_kernel(q_ref, qic_ref, kiwi_ref, kic_ref, kb_ref, vb_ref, x_ref, wo_ref,
                       tri_ref, before_ref, o_ref, key_scr, att_scr, *, k_top):
    tq = q_ref.shape[1]
    n = kb_ref.shape[1]
    q0 = pl.program_id(1) * tq
    score = _indexer_scores(qic_ref[0], kiwi_ref[0], kic_ref[0])
    q_pos = q0 + lax.broadcasted_iota(jnp.int32, (tq, 1), 0)
    admissible = lax.broadcasted_iota(jnp.int32, (1, n), 1) <= q_pos
    key_scr[...] = _sort_key(score, admissible)
    bias = _topk_bias(key_scr, tri_ref, before_ref, k_top)

    q = q_ref[0]
    for kvh in range(N_KV_HEADS):
        qs = jnp.concatenate(
            [q[:, (kvh * GROUP + g) * HEAD_DIM:(kvh * GROUP + g + 1) * HEAD_DIM]
             for g in range(GROUP)], axis=0)
        kv_sl = slice(kvh * HEAD_DIM, (kvh + 1) * HEAD_DIM)
        s = _dot_nt(qs, kb_ref[0, :, kv_sl]) * (HEAD_DIM ** -0.5)
        s = s.reshape(GROUP, tq, n) + bias[None]
        m = jnp.max(s, axis=-1, keepdims=True)
        p = jnp.exp(s - m)
        denom = jnp.sum(p, axis=-1, keepdims=True)
        o = _dot(p.astype(bf16).reshape(GROUP * tq, n), vb_ref[0, :, kv_sl])
        o = o / denom.reshape(GROUP * tq, 1)
        for g in range(GROUP):
            col = (kvh * GROUP + g) * HEAD_DIM
            att_scr[:, col:col + HEAD_DIM] = o[g * tq:(g + 1) * tq].astype(bf16)
    o_ref[0] = x_ref[0] + _dot(att_scr[...], wo_ref[...])


def _dsa_prompt(q, qic, kiwi, kic, kb, vb, x, wo, tq):
    b, t, _ = q.shape
    k_top = min(TOPK_MAX, t // 4)
    tri, before = _tie_consts(t)
    tile = lambda width: pl.BlockSpec((1, tq, width), lambda i, j: (i, j, 0))
    full = lambda width: pl.BlockSpec((1, t, width), lambda i, j: (i, 0, 0))
    return pl.pallas_call(
        functools.partial(_dsa_prompt_kernel, k_top=k_top),
        grid=(b, t // tq),
        in_specs=[tile(Q_W), tile(IDX_HEADS * IDX_K), tile(LANES), full(IDX_K), full(KV_W),
                  full(KV_W), tile(D_MODEL), _const_spec((Q_W, D_MODEL)),
                  _const_spec((LANES, LANES)), _const_spec((t, LANES))],
        out_specs=tile(D_MODEL),
        out_shape=jax.ShapeDtypeStruct((b, t, D_MODEL), f32),
        scratch_shapes=[pltpu.VMEM((tq, t), jnp.int32), pltpu.VMEM((tq, Q_W), bf16)],
        compiler_params=_params(2),
        name="dsa_prompt",
    )(q, qic, kiwi, kic, kb, vb, x, wo, tri, before)


def _sample_scores_kernel(pt_ref, qic_ref, kiwi_ref, *refs):
    del pt_ref
    *page_refs, o_ref = refs
    qic = qic_ref[...]
    wi = kiwi_ref[...]
    for i, page_ref in enumerate(page_refs):
        page = page_ref[0]
        hi = page.astype(bf16)
        lo = (page - hi.astype(f32)).astype(bf16)
        score = None
        for h in range(IDX_HEADS):
            q_hi = qic[:, h * IDX_K:h * IDX_K + IDX_DIM]
            q_lo = qic[:, h * IDX_K + 2 * IDX_DIM:h * IDX_K + 3 * IDX_DIM]
            logit = _dot_nt(q_hi, hi) + _dot_nt(q_hi, lo) + _dot_nt(q_lo, hi)
            term = wi[:, IDX_DIM + h:IDX_DIM + h + 1] * jnp.maximum(logit, 0.0)
            score = term if score is None else score + term
        o_ref[:, i * PAGE_SIZE:(i + 1) * PAGE_SIZE] = score * IDX_SCALE


def _sample_scores(page_table, qic, kiwi, cache_ki, pages_per_step):
    nb, n_pages = page_table.shape
    t = qic.shape[0] // nb
    steps = n_pages // pages_per_step
    row = lambda width: pl.BlockSpec((t, width), lambda b, j, pt: (b, 0))

    def page_spec(i):
        return pl.BlockSpec((1, PAGE_SIZE, IDX_DIM),
                            lambda b, j, pt: (pt[b, j * pages_per_step + i], 0, 0))

    return pl.pallas_call(
        _sample_scores_kernel,
        grid_spec=pltpu.PrefetchScalarGridSpec(
            num_scalar_prefetch=1,
            grid=(nb, steps),
            in_specs=[row(IDX_HEADS * IDX_K), row(LANES)]
            + [page_spec(i) for i in range(pages_per_step)],
            out_specs=pl.BlockSpec((t, pages_per_step * PAGE_SIZE), lambda b, j, pt: (b, j)),
        ),
        out_shape=jax.ShapeDtypeStruct((nb * t, n_pages * PAGE_SIZE), f32),
        compiler_params=_params(2),
        name="dsa_sample_scores",
    )(page_table, qic, kiwi, *([cache_ki] * pages_per_step))


def _dsa_sample_kernel(pt_ref, sc_ref, q_ref, qic_ref, kiwi_ref, kicn_ref, kbn_ref, vbn_ref,
                       x_ref, wo_ref, tri_ref, before_ref, *refs, k_top, pps):
    del pt_ref
    k_refs = refs[:pps]
    v_refs = refs[pps:2 * pps]
    o_ref, key_scr, bias_scr, m_scr, l_scr, acc_scr = refs[2 * pps:]
    t = q_ref.shape[0]
    n_past = sc_ref.shape[1]
    n_chunks = n_past // PAGE_SIZE + 1
    j = pl.program_id(1)
    rows = GROUP * t

    @pl.when(j == 0)
    def _():
        new = _indexer_scores(qic_ref[...], kiwi_ref[...], kicn_ref[...])
        new = jnp.concatenate([new, jnp.zeros((t, LANES - t), f32)], axis=1)
        qi = lax.broadcasted_iota(jnp.int32, (t, n_past + LANES), 0)
        kj = lax.broadcasted_iota(jnp.int32, (t, n_past + LANES), 1)
        admissible = kj <= n_past + qi
        score = jnp.concatenate([sc_ref[...], new], axis=1)
        key_scr[...] = _sort_key(score, admissible)
        bias = _topk_bias(key_scr, tri_ref, before_ref, k_top)
        for c in range(n_chunks):
            bias_scr[c] = bias[:, c * LANES:(c + 1) * LANES]
        m_scr[...] = jnp.full(m_scr.shape, -1e30, f32)
        l_scr[...] = jnp.zeros(l_scr.shape, f32)
        acc_scr[...] = jnp.zeros(acc_scr.shape, f32)

    q = q_ref[...]

    def attend(kcat, vcat, bias):
        bias4 = jnp.concatenate([bias] * GROUP, axis=0)
        for kvh in range(N_KV_HEADS):
            qs = jnp.concatenate(
                [q[:, (kvh * GROUP + g) * HEAD_DIM:(kvh * GROUP + g + 1) * HEAD_DIM]
                 for g in range(GROUP)], axis=0)
            kv_sl = slice(kvh * HEAD_DIM, (kvh + 1) * HEAD_DIM)
            s = _dot_nt(qs, kcat[:, kv_sl]) * (HEAD_DIM ** -0.5) + bias4
            m_old = m_scr[kvh]
            m_new = jnp.maximum(m_old, jnp.max(s, axis=-1, keepdims=True))
            alpha = jnp.exp(m_old - m_new)
            p = jnp.exp(s - m_new)
            l_scr[kvh] = alpha * l_scr[kvh] + jnp.sum(p, axis=-1, keepdims=True)
            acc_scr[kvh] = alpha * acc_scr[kvh] + _dot(p.astype(bf16), vcat[:, kv_sl])
            m_scr[kvh] = m_new

    kcat = jnp.concatenate([r[0].astype(bf16) for r in k_refs], axis=0)
    vcat = jnp.concatenate([r[0].astype(bf16) for r in v_refs], axis=0)
    bias = jnp.concatenate([bias_scr[j * pps + i] for i in range(pps)], axis=1)
    attend(kcat, vcat, bias)

    @pl.when(j == pl.num_programs(1) - 1)
    def _():
        pad = jnp.zeros((LANES - t, KV_W), bf16)
        attend(jnp.concatenate([kbn_ref[...], pad], axis=0),
               jnp.concatenate([vbn_ref[...], pad], axis=0), bias_scr[n_chunks - 1])
        heads = []
        for kvh in range(N_KV_HEADS):
            o = acc_scr[kvh] / l_scr[kvh]
            heads += [o[g * t:(g + 1) * t] for g in range(GROUP)]
        att = jnp.concatenate(heads, axis=1).astype(bf16)
        o_ref[...] = x_ref[...] + _dot(att, wo_ref[...])


def _dsa_sample(page_table, scores, q, qic, kiwi, kic_new, kb_new, vb_new, x, wo,
                cache_k, cache_v, pages_per_step):
    nb, n_pages = page_table.shape
    t = q.shape[0] // nb
    n_past = n_pages * PAGE_SIZE
    n_all = n_past + LANES
    k_top = min(TOPK_MAX, (n_past + t) // 4)
    steps = n_pages // pages_per_step
    tri, before = _tie_consts(n_all)
    row = lambda width: pl.BlockSpec((t, width), lambda b, j, pt: (b, 0))
    const = lambda shape: pl.BlockSpec(shape, lambda b, j, pt: (0,) * len(shape))

    def page_spec(i):
        return pl.BlockSpec((1, PAGE_SIZE, KV_W),
                            lambda b, j, pt: (pt[b, j * pages_per_step + i], 0, 0))

    rows = GROUP * t
    return pl.pallas_call(
        functools.partial(_dsa_sample_kernel, k_top=k_top, pps=pages_per_step),
        grid_spec=pltpu.PrefetchScalarGridSpec(
            num_scalar_prefetch=1,
            grid=(nb, steps),
            in_specs=[row(n_past), row(Q_W), row(IDX_HEADS * IDX_K), row(LANES), row(IDX_K),
                      row(KV_W), row(KV_W), row(D_MODEL), const((Q_W, D_MODEL)),
                      const((LANES, LANES)), const((n_all, LANES))]
            + [page_spec(i) for i in range(pages_per_step)] * 2,
            out_specs=row(D_MODEL),
            scratch_shapes=[pltpu.VMEM((t, n_all), jnp.int32),
                            pltpu.VMEM((n_all // LANES, t, LANES), f32),
                            pltpu.VMEM((N_KV_HEADS, rows, 1), f32),
                            pltpu.VMEM((N_KV_HEADS, rows, 1), f32),
                            pltpu.VMEM((N_KV_HEADS, rows, HEAD_DIM), f32)],
        ),
        out_shape=jax.ShapeDtypeStruct((nb * t, D_MODEL), f32),
        compiler_params=_params(2),
        name="dsa_sample",
    )(page_table, scores, q, qic, kiwi, kic_new, kb_new, vb_new, x, wo, tri, before,
      *([cache_k] * pages_per_step), *([cache_v] * pages_per_step))


def _ffn_kernel(x_ref, g_ref, wg_ref, wu_ref, wd_ref, o_ref):
    x = x_ref[...]
    h = _rms(x, g_ref[...]).astype(bf16)
    acc = x
    for c in range(wg_ref.shape[0]):
        gate = _dot(h, wg_ref[c])
        up = _dot(h, wu_ref[c])
        acc = acc + _dot((jax.nn.silu(gate) * up).astype(bf16), wd_ref[c])
    o_ref[...] = acc


def _ffn(x, g, wg, wu, wd, tm):
    t = x.shape[0]
    row = pl.BlockSpec((tm, D_MODEL), lambda i: (i, 0))
    resident = lambda a: pl.BlockSpec(a.shape, lambda i: (0,) * a.ndim,
                                      pipeline_mode=pl.Buffered(1))
    return pl.pallas_call(
        _ffn_kernel,
        grid=(t // tm,),
        in_specs=[row, _const_spec((1, D_MODEL)), resident(wg), resident(wu), resident(wd)],
        out_specs=row,
        out_shape=jax.ShapeDtypeStruct((t, D_MODEL), f32),
        compiler_params=_params(1),
        name="ffn",
    )(x, g, wg, wu, wd)


def _sgu_kernel(x_ref, g_ref, win_ref, gv_ref, ws_ref, bias_ref, wout_ref, *refs,
                chunk_len, emit_v):
    if emit_v:
        o_ref, v_ref, gated_scr = refs
    else:
        o_ref, gated_scr = refs
    tm = x_ref.shape[0]
    n_sub = tm // CHUNK
    x = x_ref[...]
    h = _rms(x, g_ref[...]).astype(bf16)
    z = jax.nn.gelu(_dot(h, win_ref[...]))
    u = z[:, :D_MODEL]
    v = _rms(z[:, D_MODEL:], gv_ref[...])
    if emit_v:
        v_ref[...] = v
    vb = v.astype(bf16)
    ti = lax.broadcasted_iota(jnp.int32, (CHUNK, CHUNK), 0)
    si = lax.broadcasted_iota(jnp.int32, (CHUNK, CHUNK), 1)
    shift = chunk_len.bit_length() - 1
    causal = (si <= ti) & (lax.shift_right_logical(si, shift) == lax.shift_right_logical(ti, shift))
    bias = bias_ref[...]
    for grp in range(SG_GROUPS):
        cols = slice(grp * LANES, (grp + 1) * LANES)
        w = jnp.where(causal, ws_ref[grp], 0.0).astype(bf16)
        vg = jnp.concatenate([vb[c * CHUNK:(c + 1) * CHUNK, cols] for c in range(n_sub)], axis=1)
        mixed = _dot(w, vg) + bias[:, grp:grp + 1]
        for c in range(n_sub):
            rows = slice(c * CHUNK, (c + 1) * CHUNK)
            gated_scr[rows, cols] = (u[rows, cols] * mixed[:, c * LANES:(c + 1) * LANES]).astype(bf16)
    o_ref[...] = x + _dot(gated_scr[...], wout_ref[...])


def _sgu(x, g, w_in, g_v, w_s, bias_t, w_out, tm, chunk_len, emit_v):
    t = x.shape[0]
    row = pl.BlockSpec((tm, D_MODEL), lambda i: (i, 0))
    out_shape = [jax.ShapeDtypeStruct((t, D_MODEL), f32)] * (2 if emit_v else 1)
    return pl.pallas_call(
        functools.partial(_sgu_kernel, chunk_len=chunk_len, emit_v=emit_v),
        grid=(t // tm,),
        in_specs=[row, _const_spec((1, D_MODEL)), _const_spec(w_in.shape),
                  _const_spec((1, D_MODEL)), _const_spec(w_s.shape), _const_spec(bias_t.shape),
                  _const_spec(w_out.shape)],
        out_specs=[row] * len(out_shape),
        out_shape=out_shape,
        scratch_shapes=[pltpu.VMEM((tm, D_MODEL), bf16)],
        compiler_params=_params(1),
        name="sgu",
    )(x, g, w_in, g_v, w_s, bias_t, w_out)


def kernel(x_prompt, x_sample, cache_k, cache_v, cache_kidx, page_table, norm_mix, norm_ffn,
           att_w_in, att_q_norm, att_k_norm, att_w_out, sg_w_in, sg_v_norm, sg_w_s, sg_b_s,
           sg_w_out, ffn_w_gu, ffn_w_down):
    nb_p, t_p, _ = x_prompt.shape
    nb_s, t_s, _ = x_sample.shape
    n_pool = cache_k.shape[1]
    xp = x_prompt.reshape(nb_p * t_p, D_MODEL)
    xs = x_sample.reshape(nb_s * t_s, D_MODEL)
    vec = lambda a: a.reshape(1, -1)

    w_in = jnp.pad(att_w_in[0], ((0, 0), (0, ATT_IN_PAD - ATT_IN))).astype(bf16)
    w_out = att_w_out[0].astype(bf16)
    proj = functools.partial(_att_proj, g=vec(norm_mix[0]), w=w_in,
                             gq=vec(att_q_norm[0]), gk=vec(att_k_norm[0]))
    q_p, k_p, v_p, kb_p, vb_p, qic_p, kic_p, ki_p, kiwi_p = proj(xp, tm=256)
    q_s, k_s, v_s, kb_s, vb_s, qic_s, kic_s, ki_s, kiwi_s = proj(xs, tm=nb_s * t_s)

    per_seq = lambda a: a.reshape(nb_p, t_p, a.shape[-1])
    xp = _dsa_prompt(per_seq(q_p), per_seq(qic_p), per_seq(kiwi_p), per_seq(kic_p),
                     per_seq(kb_p), per_seq(vb_p), per_seq(xp), w_out, tq=128)
    xp = xp.reshape(nb_p * t_p, D_MODEL)

    pages_per_step = 8
    scores_s = _sample_scores(page_table, qic_s, kiwi_s,
                              cache_kidx[0].reshape(n_pool, PAGE_SIZE, IDX_DIM), pages_per_step)
    xs = _dsa_sample(page_table, scores_s, q_s, qic_s, kiwi_s, kic_s, kb_s, vb_s, xs, w_out,
                     cache_k[0].reshape(n_pool, PAGE_SIZE, KV_W),
                     cache_v[0].reshape(n_pool, PAGE_SIZE, KV_W), pages_per_step)

    def ffn(x, layer, tm):
        n_ff = 2
        split = lambda w: w.astype(bf16).reshape(D_MODEL, n_ff, D_FF // n_ff).transpose(1, 0, 2)
        wg = split(ffn_w_gu[layer][:, :D_FF])
        wu = split(ffn_w_gu[layer][:, D_FF:])
        wd = ffn_w_down[layer].astype(bf16).reshape(n_ff, D_FF // n_ff, D_MODEL)
        return _ffn(x, vec(norm_ffn[layer]), wg, wu, wd, tm)

    xp = ffn(xp, 0, 512)
    xs = ffn(xs, 0, nb_s * t_s)

    sg_in = sg_w_in[0].astype(bf16)
    sg_out = sg_w_out[0].astype(bf16)
    sgu = functools.partial(_sgu, g=vec(norm_mix[1]), w_in=sg_in, g_v=vec(sg_v_norm[0]),
                            w_out=sg_out, tm=256)
    (xp,) = sgu(xp, w_s=sg_w_s[0], bias_t=sg_b_s[0].T, chunk_len=CHUNK, emit_v=False)
    reps = CHUNK // t_s
    xs, sgv = sgu(xs, w_s=jnp.tile(sg_w_s[0][:, :t_s, :t_s], (1, reps, reps)),
                  bias_t=jnp.tile(sg_b_s[0][:, :t_s], (1, reps)).T, chunk_len=t_s, emit_v=True)

    xp = ffn(xp, 1, 512)
    xs = ffn(xs, 1, nb_s * t_s)

    y_prompt = xp.reshape(nb_p, t_p, D_MODEL)
    y_sample = xs.reshape(nb_s, t_s, D_MODEL)
    kv_p = lambda a: a.reshape(1, nb_p, t_p, N_KV_HEADS, HEAD_DIM)
    kv_s = lambda a: a.reshape(1, nb_s, t_s, N_KV_HEADS, HEAD_DIM)
    return (y_prompt, y_sample, kv_p(k_p), kv_p(v_p), ki_p.reshape(1, nb_p, t_p, IDX_DIM),
            kv_s(k_s), kv_s(v_s), ki_s.reshape(1, nb_s, t_s, IDX_DIM),
            sgv.reshape(1, nb_s, t_s, D_MODEL))
```

```python
import functools

import jax
import jax.numpy as jnp
from jax import lax
from jax.experimental import pallas as pl
from jax.experimental.pallas import tpu as pltpu

D_MODEL = 1024
HEAD_DIM = 128
N_HEADS = 8
N_KV_HEADS = 2
GROUP = N_HEADS // N_KV_HEADS
IDX_HEADS = 4
IDX_DIM = 64
IDX_SCALE = (IDX_DIM ** -0.5) * (IDX_HEADS ** -0.5)
TOPK_MAX = 256
CHUNK = 128
SG_GROUPS = 8
D_FF = 2816
PAGE_SIZE = 128
EPS = 1e-6

Q_W = N_HEADS * HEAD_DIM
KV_W = N_KV_HEADS * HEAD_DIM
QI_W = IDX_HEADS * IDX_DIM
ATT_IN = Q_W + 2 * KV_W + QI_W + IDX_DIM + IDX_HEADS
LANES = 128
ATT_IN_PAD = -(-ATT_IN // LANES) * LANES
IDX_K = 4 * IDX_DIM
INT_MIN = -(2 ** 31)
NEG_INF = float("-inf")
VMEM_LIMIT = 56 * 1024 * 1024

bf16 = jnp.bfloat16
f32 = jnp.float32


def _rms(x, g):
    return x * lax.rsqrt(jnp.mean(x * x, axis=-1, keepdims=True) + EPS) * g


def _dot(a, b):
    return jnp.dot(a, b, preferred_element_type=f32)


def _dot_nt(a, b):
    return lax.dot_general(a, b, (((1,), (1,)), ((), ())), preferred_element_type=f32)


def _hi(x):
    return x.astype(bf16).astype(f32)


def _params(n_grid):
    return pltpu.CompilerParams(
        dimension_semantics=("arbitrary",) * n_grid, vmem_limit_bytes=VMEM_LIMIT)


def _const_spec(shape):
    nd = len(shape)
    return pl.BlockSpec(shape, lambda *_: (0,) * nd)


def _att_proj_kernel(x_ref, g_ref, w_ref, gq_ref, gk_ref,
                     q_ref, k_ref, v_ref, kb_ref, vb_ref, qic_ref, kic_ref, ki_ref, kiwi_ref):
    tm = x_ref.shape[0]
    h = _rms(x_ref[...], g_ref[...]).astype(bf16)
    z = _dot(h, w_ref[...])
    gq = gq_ref[...]
    gk = gk_ref[...]
    for j in range(N_HEADS):
        sl = slice(j * HEAD_DIM, (j + 1) * HEAD_DIM)
        q_ref[:, sl] = _rms(z[:, sl], gq).astype(bf16)
    for j in range(N_KV_HEADS):
        sl = slice(j * HEAD_DIM, (j + 1) * HEAD_DIM)
        kj = _rms(z[:, Q_W + j * HEAD_DIM:Q_W + (j + 1) * HEAD_DIM], gk)
        k_ref[:, sl] = kj
        kb_ref[:, sl] = kj.astype(bf16)
    v = z[:, Q_W + KV_W:Q_W + 2 * KV_W]
    v_ref[...] = v
    vb_ref[...] = v.astype(bf16)

    low_half = lax.broadcasted_iota(jnp.int32, (tm, LANES), 1) < IDX_DIM
    qi0 = Q_W + 2 * KV_W
    for p in range(IDX_HEADS // 2):
        pair = z[:, qi0 + p * LANES:qi0 + (p + 1) * LANES]
        rolled = pltpu.roll(pair, IDX_DIM, axis=1)
        for e, (own, other) in enumerate(((pair, rolled), (rolled, pair))):
            dup = jnp.where(low_half, own, other)
            single = jnp.where(low_half, own, 0.0)
            base = (2 * p + e) * IDX_K
            qic_ref[:, base:base + LANES] = dup.astype(bf16)
            qic_ref[:, base + LANES:base + 2 * LANES] = (single - _hi(single)).astype(bf16)
    last = z[:, qi0 + QI_W:qi0 + QI_W + LANES]
    rolled = pltpu.roll(last, IDX_DIM, axis=1)
    kic_ref[:, :LANES] = jnp.where(low_half, _hi(last), rolled - _hi(rolled)).astype(bf16)
    kic_ref[:, LANES:] = jnp.where(low_half, _hi(last), 0.0).astype(bf16)
    ki_ref[...] = last[:, :IDX_DIM]
    kiwi_ref[...] = last


def _att_proj(x, g, w, gq, gk, tm):
    t = x.shape[0]
    row = lambda width: pl.BlockSpec((tm, width), lambda i: (i, 0))
    outs = [(Q_W, bf16), (KV_W, f32), (KV_W, f32), (KV_W, bf16), (KV_W, bf16),
            (IDX_HEADS * IDX_K, bf16), (IDX_K, bf16), (IDX_DIM, f32), (LANES, f32)]
    return pl.pallas_call(
        _att_proj_kernel,
        grid=(t // tm,),
        in_specs=[row(D_MODEL), _const_spec((1, D_MODEL)), _const_spec((D_MODEL, ATT_IN_PAD)),
                  _const_spec((1, HEAD_DIM)), _const_spec((1, HEAD_DIM))],
        out_specs=[row(wd) for wd, _ in outs],
        out_shape=[jax.ShapeDtypeStruct((t, wd), dt) for wd, dt in outs],
        compiler_params=_params(1),
        name="att_proj",
    )(x, g, w, gq, gk)


def _sort_key(score, admissible):
    bits = lax.bitcast_convert_type(score, jnp.int32)
    key = jnp.where(bits < 0, bits ^ jnp.int32(0x7FFFFFFF), bits)
    return jnp.where(admissible, key, jnp.int32(INT_MIN))


def _count(mask):
    return jnp.sum(jnp.where(mask, 1.0, 0.0), axis=1, keepdims=True)


def _topk_bias(key_ref, tri_ref, before_ref, k_top):
    rows, n = key_ref.shape

    def step(i, t):
        cand = t + lax.shift_left(jnp.int32(1), jnp.int32(31) - i)
        return jnp.where(_count(key_ref[...] >= cand) >= k_top, cand, t)

    t = lax.fori_loop(0, 32, step, jnp.full((rows, 1), INT_MIN, jnp.int32))
    key = key_ref[...]
    need = k_top - _count(key > t)
    tie_b = jnp.where(key == t, 1.0, 0.0).astype(bf16)
    before = _dot(tie_b, before_ref[...])
    pieces = []
    for c in range(n // LANES):
        sl = slice(c * LANES, (c + 1) * LANES)
        key_c = key[:, sl]
        rank = _dot(tie_b[:, sl], tri_ref[...]) + before[:, c:c + 1]
        sel = (key_c > t) | ((key_c == t) & (rank <= need))
        pieces.append(jnp.where(sel & (key_c > INT_MIN), 0.0, NEG_INF))
    return jnp.concatenate(pieces, axis=1)


def _tie_consts(n):
    a = jnp.arange(LANES)
    tri = (a[:, None] <= a[None, :]).astype(bf16)
    before = ((jnp.arange(n)[:, None] // LANES) < a[None, :]).astype(bf16)
    return tri, before


def _indexer_scores(qic, wi, kic):
    score = None
    for h in range(IDX_HEADS):
        logit = _dot_nt(qic[:, h * IDX_K:(h + 1) * IDX_K], kic)
        term = wi[:, IDX_DIM + h:IDX_DIM + h + 1] * jnp.maximum(logit, 0.0)
        score = term if score is None else score + term
    return score * IDX_SCALE


def _dsa_prompt_kernel(q_ref, qic_ref, kiwi_ref, kic_ref, kb_ref, vb_ref, x_ref, wo_ref,
                       tri_ref, before_ref, o_ref, key_scr, att_scr, *, k_top):
    tq = q_ref.shape[1]
    n = kb_ref.shape[1]
    q0 = pl.program_id(1) * tq
    score = _indexer_scores(qic_ref[0], kiwi_ref[0], kic_ref[0])
    q_pos = q0 + lax.broadcasted_iota(jnp.int32, (tq, 1), 0)
    admissible = lax.broadcasted_iota(jnp.int32, (1, n), 1) <= q_pos
    key_scr[...] = _sort_key(score, admissible)
    bias = _topk_bias(key_scr, tri_ref, before_ref, k_top)

    q = q_ref[0]
    for kvh in range(N_KV_HEADS):
        qs = jnp.concatenate(
            [q[:, (kvh * GROUP + g) * HEAD_DIM:(kvh * GROUP + g + 1) * HEAD_DIM]
             for g in range(GROUP)], axis=0)
        kv_sl = slice(kvh * HEAD_DIM, (kvh + 1) * HEAD_DIM)
        s = _dot_nt(qs, kb_ref[0, :, kv_sl]) * (HEAD_DIM ** -0.5)
        s = s.reshape(GROUP, tq, n) + bias[None]
        m = jnp.max(s, axis=-1, keepdims=True)
        p = jnp.exp(s - m)
        denom = jnp.sum(p, axis=-1, keepdims=True)
        o = _dot(p.astype(bf16).reshape(GROUP * tq, n), vb_ref[0, :, kv_sl])
        o = o / denom.reshape(GROUP * tq, 1)
        for g in range(GROUP):
            col = (kvh * GROUP + g) * HEAD_DIM
            att_scr[:, col:col + HEAD_DIM] = o[g * tq:(g + 1) * tq].astype(bf16)
    o_ref[0] = x_ref[0] + _dot(att_scr[...], wo_ref[...])


def _dsa_prompt(q, qic, kiwi, kic, kb, vb, x, wo, tq):
    b, t, _ = q.shape
    k_top = min(TOPK_MAX, t // 4)
    tri, before = _tie_consts(t)
    tile = lambda width: pl.BlockSpec((1, tq, width), lambda i, j: (i, j, 0))
    full = lambda width: pl.BlockSpec((1, t, width), lambda i, j: (i, 0, 0))
    return pl.pallas_call(
        functools.partial(_dsa_prompt_kernel, k_top=k_top),
        grid=(b, t // tq),
        in_specs=[tile(Q_W), tile(IDX_HEADS * IDX_K), tile(LANES), full(IDX_K), full(KV_W),
                  full(KV_W), tile(D_MODEL), _const_spec((Q_W, D_MODEL)),
                  _const_spec((LANES, LANES)), _const_spec((t, LANES))],
        out_specs=tile(D_MODEL),
        out_shape=jax.ShapeDtypeStruct((b, t, D_MODEL), f32),
        scratch_shapes=[pltpu.VMEM((tq, t), jnp.int32), pltpu.VMEM((tq, Q_W), bf16)],
        compiler_params=_params(2),
        name="dsa_prompt",
    )(q, qic, kiwi, kic, kb, vb, x, wo, tri, before)


def _sample_scores_kernel(pt_ref, qic_ref, kiwi_ref, *refs):
    del pt_ref
    *page_refs, o_ref = refs
    qic = qic_ref[...]
    wi = kiwi_ref[...]
    for i, page_ref in enumerate(page_refs):
        page = page_ref[0]
        hi = page.astype(bf16)
        lo = (page - hi.astype(f32)).astype(bf16)
        score = None
        for h in range(IDX_HEADS):
            q_hi = qic[:, h * IDX_K:h * IDX_K + IDX_DIM]
            q_lo = qic[:, h * IDX_K + 2 * IDX_DIM:h * IDX_K + 3 * IDX_DIM]
            logit = _dot(q_hi, hi) + _dot(q_hi, lo) + _dot(q_lo, hi)
            term = wi[:, IDX_DIM + h:IDX_DIM + h + 1] * jnp.maximum(logit, 0.0)
            score = term if score is None else score + term
        o_ref[:, i * PAGE_SIZE:(i + 1) * PAGE_SIZE] = score * IDX_SCALE


def _sample_scores(page_table, qic, kiwi, cache_ki_t, pages_per_step):
    nb, n_pages = page_table.shape
    t = qic.shape[0] // nb
    steps = n_pages // pages_per_step
    row = lambda width: pl.BlockSpec((t, width), lambda b, j, pt: (b, 0))

    def page_spec(i):
        return pl.BlockSpec((1, IDX_DIM, PAGE_SIZE),
                            lambda b, j, pt: (pt[b, j * pages_per_step + i], 0, 0))

    return pl.pallas_call(
        _sample_scores_kernel,
        grid_spec=pltpu.PrefetchScalarGridSpec(
            num_scalar_prefetch=1,
            grid=(nb, steps),
            in_specs=[row(IDX_HEADS * IDX_K), row(LANES)]
            + [page_spec(i) for i in range(pages_per_step)],
            out_specs=pl.BlockSpec((t, pages_per_step * PAGE_SIZE), lambda b, j, pt: (b, j)),
        ),
        out_shape=jax.ShapeDtypeStruct((nb * t, n_pages * PAGE_SIZE), f32),
        compiler_params=_params(2),
        name="dsa_sample_scores",
    )(page_table, qic, kiwi, *([cache_ki_t] * pages_per_step))


def _dsa_sample_kernel(pt_ref, sc_ref, q_ref, qic_ref, kiwi_ref, kicn_ref, kbn_ref, vbn_ref,
                       x_ref, wo_ref, tri_ref, before_ref, *refs, k_top, pps):
    del pt_ref
    k_refs = refs[:pps]
    v_refs = refs[pps:2 * pps]
    o_ref, key_scr, bias_scr, m_scr, l_scr, acc_scr = refs[2 * pps:]
    t = q_ref.shape[0]
    n_past = sc_ref.shape[1]
    n_chunks = n_past // PAGE_SIZE + 1
    j = pl.program_id(1)
    rows = GROUP * t

    @pl.when(j == 0)
    def _():
        new = _indexer_scores(qic_ref[...], kiwi_ref[...], kicn_ref[...])
        new = jnp.concatenate([new, jnp.zeros((t, LANES - t), f32)], axis=1)
        qi = lax.broadcasted_iota(jnp.int32, (t, n_past + LANES), 0)
        kj = lax.broadcasted_iota(jnp.int32, (t, n_past + LANES), 1)
        admissible = kj <= n_past + qi
        score = jnp.concatenate([sc_ref[...], new], axis=1)
        key_scr[...] = _sort_key(score, admissible)
        bias = _topk_bias(key_scr, tri_ref, before_ref, k_top)
        for c in range(n_chunks):
            bias_scr[c] = bias[:, c * LANES:(c + 1) * LANES]
        m_scr[...] = jnp.full(m_scr.shape, -1e30, f32)
        l_scr[...] = jnp.zeros(l_scr.shape, f32)
        acc_scr[...] = jnp.zeros(acc_scr.shape, f32)

    q = q_ref[...]

    def attend(k_heads, v_heads, bias):
        bias4 = jnp.concatenate([bias] * GROUP, axis=0)
        for kvh in range(N_KV_HEADS):
            qs = jnp.concatenate(
                [q[:, (kvh * GROUP + g) * HEAD_DIM:(kvh * GROUP + g + 1) * HEAD_DIM]
                 for g in range(GROUP)], axis=0)
            s = _dot_nt(qs, k_heads[kvh]) * (HEAD_DIM ** -0.5) + bias4
            m_old = m_scr[kvh]
            m_new = jnp.maximum(m_old, jnp.max(s, axis=-1, keepdims=True))
            alpha = jnp.exp(m_old - m_new)
            p = jnp.exp(s - m_new)
            l_scr[kvh] = alpha * l_scr[kvh] + jnp.sum(p, axis=-1, keepdims=True)
            acc_scr[kvh] = alpha * acc_scr[kvh] + _dot(p.astype(bf16), v_heads[kvh])
            m_scr[kvh] = m_new

    def head_rows(page_refs, kvh):
        return jnp.concatenate(
            [r[pl.ds(kvh, PAGE_SIZE, stride=N_KV_HEADS), :].astype(bf16) for r in page_refs],
            axis=0)

    bias = jnp.concatenate([bias_scr[j * pps + i] for i in range(pps)], axis=1)
    attend([head_rows(k_refs, kvh) for kvh in range(N_KV_HEADS)],
           [head_rows(v_refs, kvh) for kvh in range(N_KV_HEADS)], bias)

    @pl.when(j == pl.num_programs(1) - 1)
    def _():
        pad = jnp.zeros((LANES - t, HEAD_DIM), bf16)

        def new_heads(ref):
            return [jnp.concatenate([ref[:, kvh * HEAD_DIM:(kvh + 1) * HEAD_DIM], pad], axis=0)
                    for kvh in range(N_KV_HEADS)]

        attend(new_heads(kbn_ref), new_heads(vbn_ref), bias_scr[n_chunks - 1])
        heads = []
        for kvh in range(N_KV_HEADS):
            o = acc_scr[kvh] / l_scr[kvh]
            heads += [o[g * t:(g + 1) * t] for g in range(GROUP)]
        att = jnp.concatenate(heads, axis=1).astype(bf16)
        o_ref[...] = x_ref[...] + _dot(att, wo_ref[...])


def _dsa_sample(page_table, scores, q, qic, kiwi, kic_new, kb_new, vb_new, x, wo,
                cache_k, cache_v, pages_per_step):
    nb, n_pages = page_table.shape
    t = q.shape[0] // nb
    n_past = n_pages * PAGE_SIZE
    n_all = n_past + LANES
    k_top = min(TOPK_MAX, (n_past + t) // 4)
    steps = n_pages // pages_per_step
    tri, before = _tie_consts(n_all)
    row = lambda width: pl.BlockSpec((t, width), lambda b, j, pt: (b, 0))
    const = lambda shape: pl.BlockSpec(shape, lambda b, j, pt: (0,) * len(shape))

    def page_spec(i):
        return pl.BlockSpec((PAGE_SIZE * N_KV_HEADS, HEAD_DIM),
                            lambda b, j, pt: (pt[b, j * pages_per_step + i], 0))

    rows = GROUP * t
    return pl.pallas_call(
        functools.partial(_dsa_sample_kernel, k_top=k_top, pps=pages_per_step),
        grid_spec=pltpu.PrefetchScalarGridSpec(
            num_scalar_prefetch=1,
            grid=(nb, steps),
            in_specs=[row(n_past), row(Q_W), row(IDX_HEADS * IDX_K), row(LANES), row(IDX_K),
                      row(KV_W), row(KV_W), row(D_MODEL), const((Q_W, D_MODEL)),
                      const((LANES, LANES)), const((n_all, LANES))]
            + [page_spec(i) for i in range(pages_per_step)] * 2,
            out_specs=row(D_MODEL),
            scratch_shapes=[pltpu.VMEM((t, n_all), jnp.int32),
                            pltpu.VMEM((n_all // LANES, t, LANES), f32),
                            pltpu.VMEM((N_KV_HEADS, rows, 1), f32),
                            pltpu.VMEM((N_KV_HEADS, rows, 1), f32),
                            pltpu.VMEM((N_KV_HEADS, rows, HEAD_DIM), f32)],
        ),
        out_shape=jax.ShapeDtypeStruct((nb * t, D_MODEL), f32),
        compiler_params=_params(2),
        name="dsa_sample",
    )(page_table, scores, q, qic, kiwi, kic_new, kb_new, vb_new, x, wo, tri, before,
      *([cache_k] * pages_per_step), *([cache_v] * pages_per_step))


def _ffn_kernel(x_ref, g_ref, wgu_ref, wd_ref, o_ref, *, n_ff):
    x = x_ref[...]
    h = _rms(x, g_ref[...]).astype(bf16)
    acc = x
    width = D_FF // n_ff
    for c in range(n_ff):
        gate = _dot(h, wgu_ref[:, c * width:(c + 1) * width])
        up = _dot(h, wgu_ref[:, D_FF + c * width:D_FF + (c + 1) * width])
        acc = acc + _dot((jax.nn.silu(gate) * up).astype(bf16),
                         wd_ref[c * width:(c + 1) * width, :])
    o_ref[...] = acc


def _ffn(x, g, wgu, wd, tm):
    t = x.shape[0]
    row = pl.BlockSpec((tm, D_MODEL), lambda i: (i, 0))
    resident = lambda a: pl.BlockSpec(a.shape, lambda i: (0,) * a.ndim,
                                      pipeline_mode=pl.Buffered(1))
    return pl.pallas_call(
        functools.partial(_ffn_kernel, n_ff=2),
        grid=(t // tm,),
        in_specs=[row, _const_spec((1, D_MODEL)), resident(wgu), resident(wd)],
        out_specs=row,
        out_shape=jax.ShapeDtypeStruct((t, D_MODEL), f32),
        compiler_params=_params(1),
        name="ffn",
    )(x, g, wgu, wd)


def _sgu_kernel(x_ref, g_ref, win_ref, gv_ref, ws_ref, bias_ref, wout_ref, *refs,
                chunk_len, emit_v):
    if emit_v:
        o_ref, v_ref, gated_scr = refs
    else:
        o_ref, gated_scr = refs
    tm = x_ref.shape[0]
    n_sub = tm // CHUNK
    x = x_ref[...]
    h = _rms(x, g_ref[...]).astype(bf16)
    z = jax.nn.gelu(_dot(h, win_ref[...]))
    u = z[:, :D_MODEL]
    v = _rms(z[:, D_MODEL:], gv_ref[...])
    if emit_v:
        v_ref[...] = v
    vb = v.astype(bf16)
    ti = lax.broadcasted_iota(jnp.int32, (CHUNK, CHUNK), 0)
    si = lax.broadcasted_iota(jnp.int32, (CHUNK, CHUNK), 1)
    shift = chunk_len.bit_length() - 1
    causal = (si <= ti) & (lax.shift_right_logical(si, shift) == lax.shift_right_logical(ti, shift))
    bias = bias_ref[...]
    for grp in range(SG_GROUPS):
        cols = slice(grp * LANES, (grp + 1) * LANES)
        w = jnp.where(causal, ws_ref[grp], 0.0).astype(bf16)
        vg = jnp.concatenate([vb[c * CHUNK:(c + 1) * CHUNK, cols] for c in range(n_sub)], axis=1)
        mixed = _dot(w, vg) + bias[:, grp:grp + 1]
        for c in range(n_sub):
            rows = slice(c * CHUNK, (c + 1) * CHUNK)
            gated_scr[rows, cols] = (u[rows, cols] * mixed[:, c * LANES:(c + 1) * LANES]).astype(bf16)
    o_ref[...] = x + _dot(gated_scr[...], wout_ref[...])


def _sgu(x, g, w_in, g_v, w_s, bias_t, w_out, tm, chunk_len, emit_v):
    t = x.shape[0]
    row = pl.BlockSpec((tm, D_MODEL), lambda i: (i, 0))
    out_shape = [jax.ShapeDtypeStruct((t, D_MODEL), f32)] * (2 if emit_v else 1)
    return pl.pallas_call(
        functools.partial(_sgu_kernel, chunk_len=chunk_len, emit_v=emit_v),
        grid=(t // tm,),
        in_specs=[row, _const_spec((1, D_MODEL)), _const_spec(w_in.shape),
                  _const_spec((1, D_MODEL)), _const_spec(w_s.shape), _const_spec(bias_t.shape),
                  _const_spec(w_out.shape)],
        out_specs=[row] * len(out_shape),
        out_shape=out_shape,
        scratch_shapes=[pltpu.VMEM((tm, D_MODEL), bf16)],
        compiler_params=_params(1),
        name="sgu",
    )(x, g, w_in, g_v, w_s, bias_t, w_out)


def kernel(x_prompt, x_sample, cache_k, cache_v, cache_kidx, page_table, norm_mix, norm_ffn,
           att_w_in, att_q_norm, att_k_norm, att_w_out, sg_w_in, sg_v_norm, sg_w_s, sg_b_s,
           sg_w_out, ffn_w_gu, ffn_w_down):
    nb_p, t_p, _ = x_prompt.shape
    nb_s, t_s, _ = x_sample.shape
    n_pool = cache_k.shape[1]
    xp = x_prompt.reshape(nb_p * t_p, D_MODEL)
    xs = x_sample.reshape(nb_s * t_s, D_MODEL)
    vec = lambda a: a.reshape(1, -1)

    w_in = jnp.pad(att_w_in[0], ((0, 0), (0, ATT_IN_PAD - ATT_IN))).astype(bf16)
    w_out = att_w_out[0].astype(bf16)
    proj = functools.partial(_att_proj, g=vec(norm_mix[0]), w=w_in,
                             gq=vec(att_q_norm[0]), gk=vec(att_k_norm[0]))
    q_p, k_p, v_p, kb_p, vb_p, qic_p, kic_p, ki_p, kiwi_p = proj(xp, tm=256)
    q_s, k_s, v_s, kb_s, vb_s, qic_s, kic_s, ki_s, kiwi_s = proj(xs, tm=nb_s * t_s)

    per_seq = lambda a: a.reshape(nb_p, t_p, a.shape[-1])
    xp = _dsa_prompt(per_seq(q_p), per_seq(qic_p), per_seq(kiwi_p), per_seq(kic_p),
                     per_seq(kb_p), per_seq(vb_p), per_seq(xp), w_out, tq=128)
    xp = xp.reshape(nb_p * t_p, D_MODEL)

    pages_per_step = 8
    scores_s = _sample_scores(page_table, qic_s, kiwi_s,
                              jnp.swapaxes(cache_kidx[0], 1, 2), pages_per_step)
    kv_rows = n_pool * PAGE_SIZE * N_KV_HEADS
    xs = _dsa_sample(page_table, scores_s, q_s, qic_s, kiwi_s, kic_s, kb_s, vb_s, xs, w_out,
                     cache_k[0].reshape(kv_rows, HEAD_DIM),
                     cache_v[0].reshape(kv_rows, HEAD_DIM), pages_per_step)

    def ffn(x, layer, tm):
        return _ffn(x, vec(norm_ffn[layer]), ffn_w_gu[layer].astype(bf16),
                    ffn_w_down[layer].astype(bf16), tm)

    xp = ffn(xp, 0, 512)
    xs = ffn(xs, 0, nb_s * t_s)

    sg_in = sg_w_in[0].astype(bf16)
    sg_out = sg_w_out[0].astype(bf16)
    sgu = functools.partial(_sgu, g=vec(norm_mix[1]), w_in=sg_in, g_v=vec(sg_v_norm[0]),
                            w_out=sg_out, tm=256)
    (xp,) = sgu(xp, w_s=sg_w_s[0], bias_t=sg_b_s[0].T, chunk_len=CHUNK, emit_v=False)
    reps = CHUNK // t_s
    xs, sgv = sgu(xs, w_s=jnp.tile(sg_w_s[0][:, :t_s, :t_s], (1, reps, reps)),
                  bias_t=jnp.tile(sg_b_s[0][:, :t_s], (1, reps)).T, chunk_len=t_s, emit_v=True)

    xp = ffn(xp, 1, 512)
    xs = ffn(xs, 1, nb_s * t_s)

    y_prompt = xp.reshape(nb_p, t_p, D_MODEL)
    y_sample = xs.reshape(nb_s, t_s, D_MODEL)
    kv_p = lambda a: a.reshape(1, nb_p, t_p, N_KV_HEADS, HEAD_DIM)
    kv_s = lambda a: a.reshape(1, nb_s, t_s, N_KV_HEADS, HEAD_DIM)
    return (y_prompt, y_sample, kv_p(k_p), kv_p(v_p), ki_p.reshape(1, nb_p, t_p, IDX_DIM),
            kv_s(k_s), kv_s(v_s), ki_s.reshape(1, nb_s, t_s, IDX_DIM),
            sgv.reshape(1, nb_s, t_s, D_MODEL))
```

```python
import functools

import jax
import jax.numpy as jnp
from jax import lax
from jax.experimental import pallas as pl
from jax.experimental.pallas import tpu as pltpu

D_MODEL = 1024
HEAD_DIM = 128
N_HEADS = 8
N_KV_HEADS = 2
GROUP = N_HEADS // N_KV_HEADS
IDX_HEADS = 4
IDX_DIM = 64
IDX_SCALE = (IDX_DIM ** -0.5) * (IDX_HEADS ** -0.5)
TOPK_MAX = 256
CHUNK = 128
SG_GROUPS = 8
D_FF = 2816
PAGE_SIZE = 128
EPS = 1e-6

Q_W = N_HEADS * HEAD_DIM
KV_W = N_KV_HEADS * HEAD_DIM
QI_W = IDX_HEADS * IDX_DIM
ATT_IN = Q_W + 2 * KV_W + QI_W + IDX_DIM + IDX_HEADS
LANES = 128
SUBLANES = 8
BF16_ROWS = 16
LOG2E = 1.4426950408889634
ATT_IN_PAD = -(-ATT_IN // LANES) * LANES
IDX_K = 4 * IDX_DIM
INT_MIN = -(2 ** 31)
NEG_INF = float("-inf")
VMEM_LIMIT = 56 * 1024 * 1024

bf16 = jnp.bfloat16
f32 = jnp.float32


def _rms(x, g):
    return x * lax.rsqrt(jnp.mean(x * x, axis=-1, keepdims=True) + EPS) * g


def _dot(a, b):
    return jnp.dot(a, b, preferred_element_type=f32)


def _dot_nt(a, b):
    return lax.dot_general(a, b, (((1,), (1,)), ((), ())), preferred_element_type=f32)


def _hi(x):
    return x.astype(bf16).astype(f32)


def _params(n_grid):
    return pltpu.CompilerParams(
        dimension_semantics=("arbitrary",) * n_grid, vmem_limit_bytes=VMEM_LIMIT)


def _const_spec(shape):
    nd = len(shape)
    return pl.BlockSpec(shape, lambda *_: (0,) * nd)


def _att_proj_kernel(x_ref, g_ref, w_ref, gq_ref, gk_ref,
                     q_ref, k_ref, v_ref, kb_ref, vb_ref, qic_ref, kic_ref, ki_ref, kiwi_ref, vt_ref):
    tm = x_ref.shape[0]
    h = _rms(x_ref[...], g_ref[...]).astype(bf16)
    z = _dot(h, w_ref[...])
    gq = gq_ref[...]
    gk = gk_ref[...]
    for j in range(N_HEADS):
        sl = slice(j * HEAD_DIM, (j + 1) * HEAD_DIM)
        q_ref[:, sl] = _rms(z[:, sl], gq).astype(bf16)
    for j in range(N_KV_HEADS):
        sl = slice(j * HEAD_DIM, (j + 1) * HEAD_DIM)
        kj = _rms(z[:, Q_W + j * HEAD_DIM:Q_W + (j + 1) * HEAD_DIM], gk)
        k_ref[:, sl] = kj
        kb_ref[:, sl] = kj.astype(bf16)
    v = z[:, Q_W + KV_W:Q_W + 2 * KV_W]
    v_ref[...] = v
    vb_ref[...] = v.astype(bf16)
    vt_ref[...] = v.T.astype(bf16)

    low_half = lax.broadcasted_iota(jnp.int32, (tm, LANES), 1) < IDX_DIM
    qi0 = Q_W + 2 * KV_W
    for p in range(IDX_HEADS // 2):
        pair = z[:, qi0 + p * LANES:qi0 + (p + 1) * LANES]
        rolled = pltpu.roll(pair, IDX_DIM, axis=1)
        for e, (own, other) in enumerate(((pair, rolled), (rolled, pair))):
            dup = jnp.where(low_half, own, other)
            single = jnp.where(low_half, own, 0.0)
            base = (2 * p + e) * IDX_K
            qic_ref[:, base:base + LANES] = dup.astype(bf16)
            qic_ref[:, base + LANES:base + 2 * LANES] = (single - _hi(single)).astype(bf16)
    last = z[:, qi0 + QI_W:qi0 + QI_W + LANES]
    rolled = pltpu.roll(last, IDX_DIM, axis=1)
    kic_ref[:, :LANES] = jnp.where(low_half, _hi(last), rolled - _hi(rolled)).astype(bf16)
    kic_ref[:, LANES:] = jnp.where(low_half, _hi(last), 0.0).astype(bf16)
    ki_ref[...] = last[:, :IDX_DIM]
    kiwi_ref[...] = last


def _att_proj(x, g, w, gq, gk, tm):
    t = x.shape[0]
    row = lambda width: pl.BlockSpec((tm, width), lambda i: (i, 0))
    outs = [(Q_W, bf16), (KV_W, f32), (KV_W, f32), (KV_W, bf16), (KV_W, bf16),
            (IDX_HEADS * IDX_K, bf16), (IDX_K, bf16), (IDX_DIM, f32), (LANES, f32)]
    return pl.pallas_call(
        _att_proj_kernel,
        grid=(t // tm,),
        in_specs=[row(D_MODEL), _const_spec((1, D_MODEL)), _const_spec((D_MODEL, ATT_IN_PAD)),
                  _const_spec((1, HEAD_DIM)), _const_spec((1, HEAD_DIM))],
        out_specs=[row(wd) for wd, _ in outs] + [pl.BlockSpec((KV_W, tm), lambda i: (0, i))],
        out_shape=[jax.ShapeDtypeStruct((t, wd), dt) for wd, dt in outs]
        + [jax.ShapeDtypeStruct((KV_W, t), bf16)],
        compiler_params=_params(1),
        name="att_proj",
    )(x, g, w, gq, gk)


def _sort_key(score, admissible):
    bits = lax.bitcast_convert_type(score, jnp.int32)
    key = jnp.where(bits < 0, bits ^ jnp.int32(0x7FFFFFFF), bits)
    return jnp.where(admissible, key, jnp.int32(INT_MIN))


def _count(mask, axis=1):
    ones = jnp.where(mask, 1.0, 0.0)
    if axis == 0:
        n, cols = ones.shape
        rows = 4 * SUBLANES
        ones = jnp.sum(ones.reshape(n // rows, rows, cols), axis=0)
    return jnp.sum(ones, axis=axis, keepdims=True)


def _bisect(count_ge, n_bits, k_need, shape, unroll=False):
    lowest = -(2 ** (n_bits - 1))

    def step(i, t):
        cand = t + lax.shift_left(jnp.int32(1), jnp.int32(n_bits - 1) - i)
        return jnp.where(count_ge(cand) >= k_need, cand, t)

    return lax.fori_loop(0, n_bits, step, jnp.full(shape, lowest, jnp.int32), unroll=unroll)


def _count_ge_packed(x16_ref, cand):
    n, cols = x16_ref.shape
    cand16 = jnp.broadcast_to(cand, (BF16_ROWS, cols)).astype(jnp.int16)
    parts = [None] * 4
    for r in range(n // BF16_ROWS):
        hit = jnp.where(x16_ref[r * BF16_ROWS:(r + 1) * BF16_ROWS, :] >= cand16,
                        jnp.int16(1), jnp.int16(0))
        a = r % len(parts)
        parts[a] = hit if parts[a] is None else parts[a] + hit
    part = functools.reduce(lambda x, y: x + y, [p for p in parts if p is not None])
    return jnp.sum(part.astype(jnp.int32).astype(f32), axis=0, keepdims=True)


def _topk_bias_keys_major(key_ref, x16_ref, bias_ref, tril_ref, k_top):
    n, cols = key_ref.shape
    half = 2 ** 15
    key = key_ref[...]
    hi = lax.shift_right_arithmetic(key, 16)
    x16_ref[...] = hi.astype(jnp.int16)
    t_hi = _bisect(lambda cand: _count_ge_packed(x16_ref, cand), 16, k_top, (1, cols), unroll=True)
    k_low = k_top - _count(hi > t_hi, axis=0)
    low = (key & 0xFFFF) - half
    x16_ref[...] = jnp.where(hi == t_hi, low, -half).astype(jnp.int16)
    t_lo = _bisect(lambda cand: _count_ge_packed(x16_ref, cand), 16, k_low, (1, cols), unroll=True)
    t = t_hi * (2 * half) + (t_lo + half)

    need = k_top - _count(key > t, axis=0)
    ties_before = jnp.zeros((1, cols), f32)
    for c in range(n // LANES):
        sl = slice(c * LANES, (c + 1) * LANES)
        key_c = key_ref[sl, :]
        tie = key_c == t
        tie_f = jnp.where(tie, 1.0, 0.0)
        rank = _dot(tril_ref[...], tie_f.astype(bf16)) + ties_before
        ties_before = ties_before + jnp.sum(tie_f, axis=0, keepdims=True)
        sel = (key_c > t) | (tie & (rank <= need))
        bias_ref[sl, :] = jnp.where(sel & (key_c > INT_MIN), 0.0, NEG_INF)


def _topk_bias(key_ref, tri_ref, before_ref, k_top, store):
    rows, n = key_ref.shape
    t = _bisect(lambda cand: _count(key_ref[...] >= cand), 32, k_top, (rows, 1))
    key = key_ref[...]
    need = k_top - _count(key > t)
    tie_b = jnp.where(key == t, 1.0, 0.0).astype(bf16)
    before = _dot(tie_b, before_ref[:n, :])
    for c in range(n // LANES):
        sl = slice(c * LANES, (c + 1) * LANES)
        key_c = key[:, sl]
        rank = _dot(tie_b[:, sl], tri_ref[...]) + before[:, c:c + 1]
        sel = (key_c > t) | ((key_c == t) & (rank <= need))
        store(c, jnp.where(sel & (key_c > INT_MIN), 0.0, NEG_INF))


def _tie_consts(n):
    a = jnp.arange(LANES)
    tri = (a[:, None] <= a[None, :]).astype(bf16)
    before = ((jnp.arange(n)[:, None] // LANES) < a[None, :]).astype(bf16)
    return tri, before


def _indexer_scores(qic, wi, kic):
    rows = qic.shape[0]
    heads = [qic[:, h * IDX_K:(h + 1) * IDX_K] for h in range(IDX_HEADS)]
    if rows % BF16_ROWS == 0:
        logits = _dot_nt(jnp.concatenate(heads, axis=0), kic)
        logits = [logits[h * rows:(h + 1) * rows] for h in range(IDX_HEADS)]
    else:
        logits = [_dot_nt(qh, kic) for qh in heads]
    score = None
    for h in range(IDX_HEADS):
        relu = jnp.maximum(logits[h], 0.0)
        term = wi[:, IDX_DIM + h:IDX_DIM + h + 1] * relu
        score = term if score is None else score + term
    return score * IDX_SCALE


def _dsa_prompt_tile(lk, q0, k_top, q_ref, qic_ref, kiwi_ref, kic_ref, kb_ref, vt_ref, tril_ref,
                     key_scr, x16_scr, bias_scr, s_scr, p_scr, att_scr):
    tq = q_ref.shape[1]
    qic = qic_ref[0]
    q_idx = jnp.concatenate(
        [qic[:, h * IDX_K:(h + 1) * IDX_K] for h in range(IDX_HEADS)], axis=0)
    logits = _dot_nt(kic_ref[0, :lk, :], q_idx)
    wi_t = kiwi_ref[0].T
    score = None
    for h in range(IDX_HEADS):
        term = wi_t[IDX_DIM + h:IDX_DIM + h + 1, :] * jnp.maximum(logits[:, h * tq:(h + 1) * tq], 0.0)
        score = term if score is None else score + term
    score = score * IDX_SCALE
    q_pos = q0 + lax.broadcasted_iota(jnp.int32, (1, tq), 1)
    admissible = lax.broadcasted_iota(jnp.int32, (lk, 1), 0) <= q_pos
    keys = key_scr.at[:lk, :]
    keys[...] = _sort_key(score, admissible)

    q = q_ref[0]
    for kvh in range(N_KV_HEADS):
        qs = jnp.concatenate(
            [q[:, (kvh * GROUP + g) * HEAD_DIM:(kvh * GROUP + g + 1) * HEAD_DIM]
             for g in range(GROUP)], axis=0)
        kv_sl = slice(kvh * HEAD_DIM, (kvh + 1) * HEAD_DIM)
        s_scr[kvh, :lk, :] = _dot_nt(kb_ref[0, :lk, kv_sl], qs)

    bias_ref = bias_scr.at[:lk, :]
    _topk_bias_keys_major(keys, x16_scr.at[:lk, :], bias_ref, tril_ref, k_top)

    exp_scale = (HEAD_DIM ** -0.5) * LOG2E
    for kvh in range(N_KV_HEADS):
        kv_sl = slice(kvh * HEAD_DIM, (kvh + 1) * HEAD_DIM)
        denoms = []
        for g in range(GROUP):
            cols = slice(g * tq, (g + 1) * tq)
            sb = s_scr[kvh, :lk, cols] + bias_ref[...]
            m = jnp.max(sb, axis=0, keepdims=True)
            p = jnp.exp2((sb - m) * exp_scale)
            denoms.append(jnp.sum(p, axis=0, keepdims=True))
            p_scr[kvh, :lk, cols] = p.astype(bf16)
        o_t = _dot(vt_ref[kv_sl, :lk], p_scr[kvh, :lk, :])
        o_t = o_t / jnp.concatenate(denoms, axis=1)
        for g in range(GROUP):
            col = (kvh * GROUP + g) * HEAD_DIM
            att_scr[:, col:col + HEAD_DIM] = o_t[:, g * tq:(g + 1) * tq].T.astype(bf16)


def _dsa_prompt_kernel(q_ref, qic_ref, kiwi_ref, kic_ref, kb_ref, vt_ref, x_ref, wo_ref,
                       tril_ref, o_ref, *scratch, k_top, span):
    tq = q_ref.shape[1]
    n = kb_ref.shape[1]
    q0 = pl.program_id(1) * tq
    span_class = (q0 + tq - 1) // span
    for c in range(n // span):
        @pl.when(span_class == c)
        def _(lk=(c + 1) * span):
            _dsa_prompt_tile(lk, q0, k_top, q_ref, qic_ref, kiwi_ref, kic_ref, kb_ref, vt_ref,
                             tril_ref, *scratch)
    att_scr = scratch[-1]
    o_ref[0] = x_ref[0] + _dot(att_scr[...], wo_ref[...])


def _dsa_prompt(q, qic, kiwi, kic, kb, vt, x, wo, tq, span):
    b, t, _ = q.shape
    k_top = min(TOPK_MAX, t // 4)
    a = jnp.arange(LANES)
    tril = (a[None, :] <= a[:, None]).astype(bf16)
    cols = GROUP * tq
    tile = lambda width: pl.BlockSpec((1, tq, width), lambda i, j: (i, j, 0))
    full = lambda width: pl.BlockSpec((1, t, width), lambda i, j: (i, 0, 0))
    return pl.pallas_call(
        functools.partial(_dsa_prompt_kernel, k_top=k_top, span=span),
        grid=(b, t // tq),
        in_specs=[tile(Q_W), tile(IDX_HEADS * IDX_K), tile(LANES), full(IDX_K), full(KV_W),
                  pl.BlockSpec((KV_W, t), lambda i, j: (0, i)), tile(D_MODEL),
                  _const_spec((Q_W, D_MODEL)), _const_spec((LANES, LANES))],
        out_specs=tile(D_MODEL),
        out_shape=jax.ShapeDtypeStruct((b, t, D_MODEL), f32),
        scratch_shapes=[pltpu.VMEM((t, tq), jnp.int32),
                        pltpu.VMEM((t, tq), jnp.int16),
                        pltpu.VMEM((t, tq), f32),
                        pltpu.VMEM((N_KV_HEADS, t, cols), f32),
                        pltpu.VMEM((N_KV_HEADS, t, cols), bf16),
                        pltpu.VMEM((tq, Q_W), bf16)],
        compiler_params=_params(2),
        name="dsa_prompt",
    )(q, qic, kiwi, kic, kb, vt, x, wo, tril)


def _sample_scores_kernel(pt_ref, qic_ref, kiwi_ref, *refs):
    del pt_ref
    *page_refs, o_ref = refs
    qic = qic_ref[...]
    wi = kiwi_ref[...]
    for i, page_ref in enumerate(page_refs):
        page = page_ref[0]
        hi = page.astype(bf16)
        lo = (page - hi.astype(f32)).astype(bf16)
        score = None
        for h in range(IDX_HEADS):
            q_hi = qic[:, h * IDX_K:h * IDX_K + IDX_DIM]
            q_lo = qic[:, h * IDX_K + 2 * IDX_DIM:h * IDX_K + 3 * IDX_DIM]
            logit = _dot(q_hi, hi) + _dot(q_hi, lo) + _dot(q_lo, hi)
            term = wi[:, IDX_DIM + h:IDX_DIM + h + 1] * jnp.maximum(logit, 0.0)
            score = term if score is None else score + term
        o_ref[:, i * PAGE_SIZE:(i + 1) * PAGE_SIZE] = score * IDX_SCALE


def _sample_scores(page_table, qic, kiwi, cache_ki_t, pages_per_step):
    nb, n_pages = page_table.shape
    t = qic.shape[0] // nb
    steps = n_pages // pages_per_step
    row = lambda width: pl.BlockSpec((t, width), lambda b, j, pt: (b, 0))

    def page_spec(i):
        return pl.BlockSpec((1, IDX_DIM, PAGE_SIZE),
                            lambda b, j, pt: (pt[b, j * pages_per_step + i], 0, 0))

    return pl.pallas_call(
        _sample_scores_kernel,
        grid_spec=pltpu.PrefetchScalarGridSpec(
            num_scalar_prefetch=1,
            grid=(nb, steps),
            in_specs=[row(IDX_HEADS * IDX_K), row(LANES)]
            + [page_spec(i) for i in range(pages_per_step)],
            out_specs=pl.BlockSpec((t, pages_per_step * PAGE_SIZE), lambda b, j, pt: (b, j)),
        ),
        out_shape=jax.ShapeDtypeStruct((nb * t, n_pages * PAGE_SIZE), f32),
        compiler_params=_params(2),
        name="dsa_sample_scores",
    )(page_table, qic, kiwi, *([cache_ki_t] * pages_per_step))


def _dsa_sample_kernel(pt_ref, sc_ref, q_ref, qic_ref, kiwi_ref, kicn_ref, kbn_ref, vbn_ref,
                       x_ref, wo_ref, tri_ref, before_ref, *refs, k_top, pps):
    del pt_ref
    k_refs = refs[:pps]
    v_refs = refs[pps:2 * pps]
    o_ref, key_scr, bias_scr, m_scr, l_scr, acc_scr = refs[2 * pps:]
    t = q_ref.shape[0]
    n_past = sc_ref.shape[1]
    n_chunks = n_past // PAGE_SIZE + 1
    j = pl.program_id(1)
    rows = GROUP * t

    @pl.when(j == 0)
    def _():
        new = _indexer_scores(qic_ref[...], kiwi_ref[...], kicn_ref[...])
        new = jnp.concatenate([new, jnp.zeros((t, LANES - t), f32)], axis=1)
        qi = lax.broadcasted_iota(jnp.int32, (t, n_past + LANES), 0)
        kj = lax.broadcasted_iota(jnp.int32, (t, n_past + LANES), 1)
        admissible = kj <= n_past + qi
        score = jnp.concatenate([sc_ref[...], new], axis=1)
        key_scr[...] = _sort_key(score, admissible)

        def store(c, piece):
            bias_scr[c] = piece

        _topk_bias(key_scr, tri_ref, before_ref, k_top, store)
        m_scr[...] = jnp.full(m_scr.shape, -1e30, f32)
        l_scr[...] = jnp.zeros(l_scr.shape, f32)
        acc_scr[...] = jnp.zeros(acc_scr.shape, f32)

    q = q_ref[...]

    def attend(k_heads, v_heads, bias):
        bias4 = jnp.concatenate([bias] * GROUP, axis=0)
        for kvh in range(N_KV_HEADS):
            qs = jnp.concatenate(
                [q[:, (kvh * GROUP + g) * HEAD_DIM:(kvh * GROUP + g + 1) * HEAD_DIM]
                 for g in range(GROUP)], axis=0)
            s = _dot_nt(qs, k_heads[kvh]) * (HEAD_DIM ** -0.5) + bias4
            m_old = m_scr[kvh]
            m_new = jnp.maximum(m_old, jnp.max(s, axis=-1, keepdims=True))
            alpha = jnp.exp(m_old - m_new)
            p = jnp.exp(s - m_new)
            l_scr[kvh] = alpha * l_scr[kvh] + jnp.sum(p, axis=-1, keepdims=True)
            acc_scr[kvh] = alpha * acc_scr[kvh] + _dot(p.astype(bf16), v_heads[kvh])
            m_scr[kvh] = m_new

    def head_rows(page_refs, kvh):
        return jnp.concatenate(
            [r[pl.ds(kvh, PAGE_SIZE, stride=N_KV_HEADS), :].astype(bf16) for r in page_refs],
            axis=0)

    bias = jnp.concatenate([bias_scr[j * pps + i] for i in range(pps)], axis=1)
    attend([head_rows(k_refs, kvh) for kvh in range(N_KV_HEADS)],
           [head_rows(v_refs, kvh) for kvh in range(N_KV_HEADS)], bias)

    @pl.when(j == pl.num_programs(1) - 1)
    def _():
        pad = jnp.zeros((LANES - t, HEAD_DIM), bf16)

        def new_heads(ref):
            return [jnp.concatenate([ref[:, kvh * HEAD_DIM:(kvh + 1) * HEAD_DIM], pad], axis=0)
                    for kvh in range(N_KV_HEADS)]

        attend(new_heads(kbn_ref), new_heads(vbn_ref), bias_scr[n_chunks - 1])
        heads = []
        for kvh in range(N_KV_HEADS):
            o = acc_scr[kvh] / l_scr[kvh]
            heads += [o[g * t:(g + 1) * t] for g in range(GROUP)]
        att = jnp.concatenate(heads, axis=1).astype(bf16)
        o_ref[...] = x_ref[...] + _dot(att, wo_ref[...])


def _dsa_sample(page_table, scores, q, qic, kiwi, kic_new, kb_new, vb_new, x, wo,
                cache_k, cache_v, pages_per_step):
    nb, n_pages = page_table.shape
    t = q.shape[0] // nb
    n_past = n_pages * PAGE_SIZE
    n_all = n_past + LANES
    k_top = min(TOPK_MAX, (n_past + t) // 4)
    steps = n_pages // pages_per_step
    tri, before = _tie_consts(n_all)
    row = lambda width: pl.BlockSpec((t, width), lambda b, j, pt: (b, 0))
    const = lambda shape: pl.BlockSpec(shape, lambda b, j, pt: (0,) * len(shape))

    def page_spec(i):
        return pl.BlockSpec((PAGE_SIZE * N_KV_HEADS, HEAD_DIM),
                            lambda b, j, pt: (pt[b, j * pages_per_step + i], 0))

    rows = GROUP * t
    return pl.pallas_call(
        functools.partial(_dsa_sample_kernel, k_top=k_top, pps=pages_per_step),
        grid_spec=pltpu.PrefetchScalarGridSpec(
            num_scalar_prefetch=1,
            grid=(nb, steps),
            in_specs=[row(n_past), row(Q_W), row(IDX_HEADS * IDX_K), row(LANES), row(IDX_K),
                      row(KV_W), row(KV_W), row(D_MODEL), const((Q_W, D_MODEL)),
                      const((LANES, LANES)), const((n_all, LANES))]
            + [page_spec(i) for i in range(pages_per_step)] * 2,
            out_specs=row(D_MODEL),
            scratch_shapes=[pltpu.VMEM((t, n_all), jnp.int32),
                            pltpu.VMEM((n_all // LANES, t, LANES), f32),
                            pltpu.VMEM((N_KV_HEADS, rows, 1), f32),
                            pltpu.VMEM((N_KV_HEADS, rows, 1), f32),
                            pltpu.VMEM((N_KV_HEADS, rows, HEAD_DIM), f32)],
        ),
        out_shape=jax.ShapeDtypeStruct((nb * t, D_MODEL), f32),
        compiler_params=_params(2),
        name="dsa_sample",
    )(page_table, scores, q, qic, kiwi, kic_new, kb_new, vb_new, x, wo, tri, before,
      *([cache_k] * pages_per_step), *([cache_v] * pages_per_step))


def _ffn_kernel(x_ref, g_ref, wgu_ref, wd_ref, o_ref, *, n_ff):
    x = x_ref[...]
    h = _rms(x, g_ref[...]).astype(bf16)
    acc = x
    width = D_FF // n_ff
    for c in range(n_ff):
        gate = _dot(h, wgu_ref[:, c * width:(c + 1) * width])
        up = _dot(h, wgu_ref[:, D_FF + c * width:D_FF + (c + 1) * width])
        acc = acc + _dot((jax.nn.silu(gate) * up).astype(bf16),
                         wd_ref[c * width:(c + 1) * width, :])
    o_ref[...] = acc


def _ffn(x, g, wgu, wd, tm):
    t = x.shape[0]
    row = pl.BlockSpec((tm, D_MODEL), lambda i: (i, 0))
    resident = lambda a: pl.BlockSpec(a.shape, lambda i: (0,) * a.ndim,
                                      pipeline_mode=pl.Buffered(1))
    return pl.pallas_call(
        functools.partial(_ffn_kernel, n_ff=2),
        grid=(t // tm,),
        in_specs=[row, _const_spec((1, D_MODEL)), resident(wgu), resident(wd)],
        out_specs=row,
        out_shape=jax.ShapeDtypeStruct((t, D_MODEL), f32),
        compiler_params=_params(1),
        name="ffn",
    )(x, g, wgu, wd)


def _sgu_kernel(x_ref, g_ref, win_ref, gv_ref, ws_ref, bias_ref, wout_ref, *refs,
                chunk_len, emit_v):
    if emit_v:
        o_ref, v_ref, gated_scr = refs
    else:
        o_ref, gated_scr = refs
    tm = x_ref.shape[0]
    n_sub = tm // CHUNK
    x = x_ref[...]
    h = _rms(x, g_ref[...]).astype(bf16)
    z = jax.nn.gelu(_dot(h, win_ref[...]))
    u = z[:, :D_MODEL]
    v = _rms(z[:, D_MODEL:], gv_ref[...])
    if emit_v:
        v_ref[...] = v
    vb = v.astype(bf16)
    ti = lax.broadcasted_iota(jnp.int32, (CHUNK, CHUNK), 0)
    si = lax.broadcasted_iota(jnp.int32, (CHUNK, CHUNK), 1)
    shift = chunk_len.bit_length() - 1
    causal = (si <= ti) & (lax.shift_right_logical(si, shift) == lax.shift_right_logical(ti, shift))
    bias = bias_ref[...]
    for grp in range(SG_GROUPS):
        cols = slice(grp * LANES, (grp + 1) * LANES)
        w = jnp.where(causal, ws_ref[grp], 0.0).astype(bf16)
        vg = jnp.concatenate([vb[c * CHUNK:(c + 1) * CHUNK, cols] for c in range(n_sub)], axis=1)
        mixed = _dot(w, vg) + bias[:, grp:grp + 1]
        for c in range(n_sub):
            rows = slice(c * CHUNK, (c + 1) * CHUNK)
            gated_scr[rows, cols] = (u[rows, cols] * mixed[:, c * LANES:(c + 1) * LANES]).astype(bf16)
    o_ref[...] = x + _dot(gated_scr[...], wout_ref[...])


def _sgu(x, g, w_in, g_v, w_s, bias_t, w_out, tm, chunk_len, emit_v):
    t = x.shape[0]
    row = pl.BlockSpec((tm, D_MODEL), lambda i: (i, 0))
    out_shape = [jax.ShapeDtypeStruct((t, D_MODEL), f32)] * (2 if emit_v else 1)
    return pl.pallas_call(
        functools.partial(_sgu_kernel, chunk_len=chunk_len, emit_v=emit_v),
        grid=(t // tm,),
        in_specs=[row, _const_spec((1, D_MODEL)), _const_spec(w_in.shape),
                  _const_spec((1, D_MODEL)), _const_spec(w_s.shape), _const_spec(bias_t.shape),
                  _const_spec(w_out.shape)],
        out_specs=[row] * len(out_shape),
        out_shape=out_shape,
        scratch_shapes=[pltpu.VMEM((tm, D_MODEL), bf16)],
        compiler_params=_params(1),
        name="sgu",
    )(x, g, w_in, g_v, w_s, bias_t, w_out)


def kernel(x_prompt, x_sample, cache_k, cache_v, cache_kidx, page_table, norm_mix, norm_ffn,
           att_w_in, att_q_norm, att_k_norm, att_w_out, sg_w_in, sg_v_norm, sg_w_s, sg_b_s,
           sg_w_out, ffn_w_gu, ffn_w_down):
    nb_p, t_p, _ = x_prompt.shape
    nb_s, t_s, _ = x_sample.shape
    n_pool = cache_k.shape[1]
    xp = x_prompt.reshape(nb_p * t_p, D_MODEL)
    xs = x_sample.reshape(nb_s * t_s, D_MODEL)
    vec = lambda a: a.reshape(1, -1)

    w_in = jnp.pad(att_w_in[0], ((0, 0), (0, ATT_IN_PAD - ATT_IN))).astype(bf16)
    w_out = att_w_out[0].astype(bf16)
    proj = functools.partial(_att_proj, g=vec(norm_mix[0]), w=w_in,
                             gq=vec(att_q_norm[0]), gk=vec(att_k_norm[0]))
    q_p, k_p, v_p, kb_p, _, qic_p, kic_p, ki_p, kiwi_p, vt_p = proj(xp, tm=256)
    q_s, k_s, v_s, kb_s, vb_s, qic_s, kic_s, ki_s, kiwi_s, _ = proj(xs, tm=nb_s * t_s)

    per_seq = lambda a: a.reshape(nb_p, t_p, a.shape[-1])
    xp = _dsa_prompt(per_seq(q_p), per_seq(qic_p), per_seq(kiwi_p), per_seq(kic_p),
                     per_seq(kb_p), vt_p, per_seq(xp), w_out, tq=128, span=256)
    xp = xp.reshape(nb_p * t_p, D_MODEL)

    pages_per_step = 8
    scores_s = _sample_scores(page_table, qic_s, kiwi_s,
                              jnp.swapaxes(cache_kidx[0], 1, 2), pages_per_step)
    kv_rows = n_pool * PAGE_SIZE * N_KV_HEADS
    xs = _dsa_sample(page_table, scores_s, q_s, qic_s, kiwi_s, kic_s, kb_s, vb_s, xs, w_out,
                     cache_k[0].reshape(kv_rows, HEAD_DIM),
                     cache_v[0].reshape(kv_rows, HEAD_DIM), pages_per_step)

    def ffn(x, layer, tm):
        return _ffn(x, vec(norm_ffn[layer]), ffn_w_gu[layer].astype(bf16),
                    ffn_w_down[layer].astype(bf16), tm)

    xp = ffn(xp, 0, 512)
    xs = ffn(xs, 0, nb_s * t_s)

    sg_in = sg_w_in[0].astype(bf16)
    sg_out = sg_w_out[0].astype(bf16)
    sgu = functools.partial(_sgu, g=vec(norm_mix[1]), w_in=sg_in, g_v=vec(sg_v_norm[0]),
                            w_out=sg_out, tm=256)
    (xp,) = sgu(xp, w_s=sg_w_s[0], bias_t=sg_b_s[0].T, chunk_len=CHUNK, emit_v=False)
    reps = CHUNK // t_s
    xs, sgv = sgu(xs, w_s=jnp.tile(sg_w_s[0][:, :t_s, :t_s], (1, reps, reps)),
                  bias_t=jnp.tile(sg_b_s[0][:, :t_s], (1, reps)).T, chunk_len=t_s, emit_v=True)

    xp = ffn(xp, 1, 512)
    xs = ffn(xs, 1, nb_s * t_s)

    y_prompt = xp.reshape(nb_p, t_p, D_MODEL)
    y_sample = xs.reshape(nb_s, t_s, D_MODEL)
    kv_p = lambda a: a.reshape(1, nb_p, t_p, N_KV_HEADS, HEAD_DIM)
    kv_s = lambda a: a.reshape(1, nb_s, t_s, N_KV_HEADS, HEAD_DIM)
    return (y_prompt, y_sample, kv_p(k_p), kv_p(v_p), ki_p.reshape(1, nb_p, t_p, IDX_DIM),
            kv_s(k_s), kv_s(v_s), ki_s.reshape(1, nb_s, t_s, IDX_DIM),
            sgv.reshape(1, nb_s, t_s, D_MODEL))
```

```python
import functools

import jax
import jax.numpy as jnp
from jax import lax
from jax.experimental import pallas as pl
from jax.experimental.pallas import tpu as pltpu

D_MODEL = 1024
HEAD_DIM = 128
N_HEADS = 8
N_KV_HEADS = 2
GROUP = N_HEADS // N_KV_HEADS
IDX_HEADS = 4
IDX_DIM = 64
IDX_SCALE = (IDX_DIM ** -0.5) * (IDX_HEADS ** -0.5)
TOPK_MAX = 256
CHUNK = 128
SG_GROUPS = 8
D_FF = 2816
PAGE_SIZE = 128
EPS = 1e-6

Q_W = N_HEADS * HEAD_DIM
KV_W = N_KV_HEADS * HEAD_DIM
QI_W = IDX_HEADS * IDX_DIM
ATT_IN = Q_W + 2 * KV_W + QI_W + IDX_DIM + IDX_HEADS
LANES = 128
SUBLANES = 8
BF16_ROWS = 16
LOG2E = 1.4426950408889634
ATT_IN_PAD = -(-ATT_IN // LANES) * LANES
IDX_K = 4 * IDX_DIM
INT_MIN = -(2 ** 31)
NEG_INF = float("-inf")
VMEM_LIMIT = 56 * 1024 * 1024

bf16 = jnp.bfloat16
f32 = jnp.float32


def _rms(x, g):
    return x * lax.rsqrt(jnp.mean(x * x, axis=-1, keepdims=True) + EPS) * g


def _dot(a, b):
    return jnp.dot(a, b, preferred_element_type=f32)


def _dot_nt(a, b):
    return lax.dot_general(a, b, (((1,), (1,)), ((), ())), preferred_element_type=f32)


def _hi(x):
    return x.astype(bf16).astype(f32)


def _params(n_grid):
    return pltpu.CompilerParams(
        dimension_semantics=("arbitrary",) * n_grid, vmem_limit_bytes=VMEM_LIMIT)


def _const_spec(shape):
    nd = len(shape)
    return pl.BlockSpec(shape, lambda *_: (0,) * nd)


def _att_proj_kernel(x_ref, g_ref, w_ref, gq_ref, gk_ref,
                     q_ref, k_ref, v_ref, kb_ref, vb_ref, qic_ref, kic_ref, ki_ref, kiwi_ref, vt_ref):
    tm = x_ref.shape[0]
    h = _rms(x_ref[...], g_ref[...]).astype(bf16)
    z = _dot(h, w_ref[...])
    gq = gq_ref[...]
    gk = gk_ref[...]
    for j in range(N_HEADS):
        sl = slice(j * HEAD_DIM, (j + 1) * HEAD_DIM)
        q_ref[:, sl] = _rms(z[:, sl], gq).astype(bf16)
    for j in range(N_KV_HEADS):
        sl = slice(j * HEAD_DIM, (j + 1) * HEAD_DIM)
        kj = _rms(z[:, Q_W + j * HEAD_DIM:Q_W + (j + 1) * HEAD_DIM], gk)
        k_ref[:, sl] = kj
        kb_ref[:, sl] = kj.astype(bf16)
    v = z[:, Q_W + KV_W:Q_W + 2 * KV_W]
    v_ref[...] = v
    vb_ref[...] = v.astype(bf16)
    vt_ref[...] = v.T.astype(bf16)

    low_half = lax.broadcasted_iota(jnp.int32, (tm, LANES), 1) < IDX_DIM
    qi0 = Q_W + 2 * KV_W
    for p in range(IDX_HEADS // 2):
        pair = z[:, qi0 + p * LANES:qi0 + (p + 1) * LANES]
        rolled = pltpu.roll(pair, IDX_DIM, axis=1)
        for e, (own, other) in enumerate(((pair, rolled), (rolled, pair))):
            dup = jnp.where(low_half, own, other)
            single = jnp.where(low_half, own, 0.0)
            base = (2 * p + e) * IDX_K
            qic_ref[:, base:base + LANES] = dup.astype(bf16)
            qic_ref[:, base + LANES:base + 2 * LANES] = (single - _hi(single)).astype(bf16)
    last = z[:, qi0 + QI_W:qi0 + QI_W + LANES]
    rolled = pltpu.roll(last, IDX_DIM, axis=1)
    kic_ref[:, :LANES] = jnp.where(low_half, _hi(last), rolled - _hi(rolled)).astype(bf16)
    kic_ref[:, LANES:] = jnp.where(low_half, _hi(last), 0.0).astype(bf16)
    ki_ref[...] = last[:, :IDX_DIM]
    kiwi_ref[...] = last


def _att_proj(x, g, w, gq, gk, tm):
    t = x.shape[0]
    row = lambda width: pl.BlockSpec((tm, width), lambda i: (i, 0))
    outs = [(Q_W, bf16), (KV_W, f32), (KV_W, f32), (KV_W, bf16), (KV_W, bf16),
            (IDX_HEADS * IDX_K, bf16), (IDX_K, bf16), (IDX_DIM, f32), (LANES, f32)]
    return pl.pallas_call(
        _att_proj_kernel,
        grid=(t // tm,),
        in_specs=[row(D_MODEL), _const_spec((1, D_MODEL)), _const_spec((D_MODEL, ATT_IN_PAD)),
                  _const_spec((1, HEAD_DIM)), _const_spec((1, HEAD_DIM))],
        out_specs=[row(wd) for wd, _ in outs] + [pl.BlockSpec((KV_W, tm), lambda i: (0, i))],
        out_shape=[jax.ShapeDtypeStruct((t, wd), dt) for wd, dt in outs]
        + [jax.ShapeDtypeStruct((KV_W, t), bf16)],
        compiler_params=_params(1),
        name="att_proj",
    )(x, g, w, gq, gk)


def _sort_key(score, admissible):
    bits = lax.bitcast_convert_type(score, jnp.int32)
    key = jnp.where(bits < 0, bits ^ jnp.int32(0x7FFFFFFF), bits)
    return jnp.where(admissible, key, jnp.int32(INT_MIN))


def _count(mask, axis=1):
    ones = jnp.where(mask, 1.0, 0.0)
    if axis == 0:
        n, cols = ones.shape
        rows = 4 * SUBLANES
        ones = jnp.sum(ones.reshape(n // rows, rows, cols), axis=0)
    return jnp.sum(ones, axis=axis, keepdims=True)


def _bisect(count_ge, n_bits, k_need, shape):
    lowest = -(2 ** (n_bits - 1))

    def step(i, t):
        cand = t + lax.shift_left(jnp.int32(1), jnp.int32(n_bits - 1) - i)
        return jnp.where(count_ge(cand) >= k_need, cand, t)

    return lax.fori_loop(0, n_bits, step, jnp.full(shape, lowest, jnp.int32))


def _count_ge_packed(x16_ref, cand):
    n, cols = x16_ref.shape
    cand16 = jnp.broadcast_to(cand, (BF16_ROWS, cols)).astype(jnp.int16)
    parts = [None] * 4
    for r in range(n // BF16_ROWS):
        hit = jnp.where(x16_ref[r * BF16_ROWS:(r + 1) * BF16_ROWS, :] >= cand16,
                        jnp.int16(1), jnp.int16(0))
        a = r % len(parts)
        parts[a] = hit if parts[a] is None else parts[a] + hit
    part = functools.reduce(lambda x, y: x + y, [p for p in parts if p is not None])
    return jnp.sum(part.astype(jnp.int32).astype(f32), axis=0, keepdims=True)


def _topk_bias_keys_major(key_ref, x16_ref, bias_ref, tril_ref, k_top):
    n, cols = key_ref.shape
    half = 2 ** 15
    key = key_ref[...]
    hi = lax.shift_right_arithmetic(key, 16)
    x16_ref[...] = hi.astype(jnp.int16)
    t_hi = _bisect(lambda cand: _count_ge_packed(x16_ref, cand), 16, k_top, (1, cols))
    k_low = k_top - _count(hi > t_hi, axis=0)
    low = (key & 0xFFFF) - half
    x16_ref[...] = jnp.where(hi == t_hi, low, -half).astype(jnp.int16)
    t_lo = _bisect(lambda cand: _count_ge_packed(x16_ref, cand), 16, k_low, (1, cols))
    t = t_hi * (2 * half) + (t_lo + half)

    need = k_top - _count(key > t, axis=0)
    ties_before = jnp.zeros((1, cols), f32)
    for c in range(n // LANES):
        sl = slice(c * LANES, (c + 1) * LANES)
        key_c = key_ref[sl, :]
        tie = key_c == t
        tie_f = jnp.where(tie, 1.0, 0.0)
        rank = _dot(tril_ref[...], tie_f.astype(bf16)) + ties_before
        ties_before = ties_before + jnp.sum(tie_f, axis=0, keepdims=True)
        sel = (key_c > t) | (tie & (rank <= need))
        bias_ref[sl, :] = jnp.where(sel & (key_c > INT_MIN), 0.0, NEG_INF)


def _topk_bias(key_ref, tri_ref, before_ref, k_top, store):
    rows, n = key_ref.shape
    t = _bisect(lambda cand: _count(key_ref[...] >= cand), 32, k_top, (rows, 1))
    key = key_ref[...]
    need = k_top - _count(key > t)
    tie_b = jnp.where(key == t, 1.0, 0.0).astype(bf16)
    before = _dot(tie_b, before_ref[:n, :])
    for c in range(n // LANES):
        sl = slice(c * LANES, (c + 1) * LANES)
        key_c = key[:, sl]
        rank = _dot(tie_b[:, sl], tri_ref[...]) + before[:, c:c + 1]
        sel = (key_c > t) | ((key_c == t) & (rank <= need))
        store(c, jnp.where(sel & (key_c > INT_MIN), 0.0, NEG_INF))


def _tie_consts(n):
    a = jnp.arange(LANES)
    tri = (a[:, None] <= a[None, :]).astype(bf16)
    before = ((jnp.arange(n)[:, None] // LANES) < a[None, :]).astype(bf16)
    return tri, before


def _indexer_scores(qic, wi, kic):
    rows = qic.shape[0]
    heads = [qic[:, h * IDX_K:(h + 1) * IDX_K] for h in range(IDX_HEADS)]
    if rows % BF16_ROWS == 0:
        logits = _dot_nt(jnp.concatenate(heads, axis=0), kic)
        logits = [logits[h * rows:(h + 1) * rows] for h in range(IDX_HEADS)]
    else:
        logits = [_dot_nt(qh, kic) for qh in heads]
    score = None
    for h in range(IDX_HEADS):
        relu = jnp.maximum(logits[h], 0.0)
        term = wi[:, IDX_DIM + h:IDX_DIM + h + 1] * relu
        score = term if score is None else score + term
    return score * IDX_SCALE


def _dsa_prompt_tile(lk, q0, k_top, q_ref, qic_ref, kiwi_ref, kic_ref, kb_ref, vt_ref, tril_ref,
                     key_scr, x16_scr, bias_scr, s_scr, p_scr, att_scr):
    tq = q_ref.shape[1]
    qic = qic_ref[0]
    q_idx = jnp.concatenate(
        [qic[:, h * IDX_K:(h + 1) * IDX_K] for h in range(IDX_HEADS)], axis=0)
    logits = _dot_nt(kic_ref[0, :lk, :], q_idx)
    wi_t = kiwi_ref[0].T
    score = None
    for h in range(IDX_HEADS):
        term = wi_t[IDX_DIM + h:IDX_DIM + h + 1, :] * jnp.maximum(logits[:, h * tq:(h + 1) * tq], 0.0)
        score = term if score is None else score + term
    score = score * IDX_SCALE
    q_pos = q0 + lax.broadcasted_iota(jnp.int32, (1, tq), 1)
    admissible = lax.broadcasted_iota(jnp.int32, (lk, 1), 0) <= q_pos
    keys = key_scr.at[:lk, :]
    keys[...] = _sort_key(score, admissible)

    q = q_ref[0]
    for kvh in range(N_KV_HEADS):
        qs = jnp.concatenate(
            [q[:, (kvh * GROUP + g) * HEAD_DIM:(kvh * GROUP + g + 1) * HEAD_DIM]
             for g in range(GROUP)], axis=0)
        kv_sl = slice(kvh * HEAD_DIM, (kvh + 1) * HEAD_DIM)
        s_scr[kvh, :lk, :] = _dot_nt(kb_ref[0, :lk, kv_sl], qs)

    bias_ref = bias_scr.at[:lk, :]
    _topk_bias_keys_major(keys, x16_scr.at[:lk, :], bias_ref, tril_ref, k_top)

    exp_scale = (HEAD_DIM ** -0.5) * LOG2E
    for kvh in range(N_KV_HEADS):
        kv_sl = slice(kvh * HEAD_DIM, (kvh + 1) * HEAD_DIM)
        denoms = []
        for g in range(GROUP):
            cols = slice(g * tq, (g + 1) * tq)
            sb = s_scr[kvh, :lk, cols] + bias_ref[...]
            m = jnp.max(sb, axis=0, keepdims=True)
            p = jnp.exp2((sb - m) * exp_scale)
            denoms.append(jnp.sum(p, axis=0, keepdims=True))
            p_scr[kvh, :lk, cols] = p.astype(bf16)
        o_t = _dot(vt_ref[kv_sl, :lk], p_scr[kvh, :lk, :])
        o_t = o_t / jnp.concatenate(denoms, axis=1)
        for g in range(GROUP):
            col = (kvh * GROUP + g) * HEAD_DIM
            att_scr[:, col:col + HEAD_DIM] = o_t[:, g * tq:(g + 1) * tq].T.astype(bf16)


def _dsa_prompt_kernel(q_ref, qic_ref, kiwi_ref, kic_ref, kb_ref, vt_ref, x_ref, wo_ref,
                       tril_ref, o_ref, *scratch, k_top, span):
    tq = q_ref.shape[1]
    n = kb_ref.shape[1]
    q0 = pl.program_id(1) * tq
    span_class = (q0 + tq - 1) // span
    for c in range(n // span):
        @pl.when(span_class == c)
        def _(lk=(c + 1) * span):
            _dsa_prompt_tile(lk, q0, k_top, q_ref, qic_ref, kiwi_ref, kic_ref, kb_ref, vt_ref,
                             tril_ref, *scratch)
    att_scr = scratch[-1]
    o_ref[0] = x_ref[0] + _dot(att_scr[...], wo_ref[...])


def _dsa_prompt(q, qic, kiwi, kic, kb, vt, x, wo, tq, span):
    b, t, _ = q.shape
    k_top = min(TOPK_MAX, t // 4)
    a = jnp.arange(LANES)
    tril = (a[None, :] <= a[:, None]).astype(bf16)
    cols = GROUP * tq
    tile = lambda width: pl.BlockSpec((1, tq, width), lambda i, j: (i, j, 0))
    full = lambda width: pl.BlockSpec((1, t, width), lambda i, j: (i, 0, 0))
    return pl.pallas_call(
        functools.partial(_dsa_prompt_kernel, k_top=k_top, span=span),
        grid=(b, t // tq),
        in_specs=[tile(Q_W), tile(IDX_HEADS * IDX_K), tile(LANES), full(IDX_K), full(KV_W),
                  pl.BlockSpec((KV_W, t), lambda i, j: (0, i)), tile(D_MODEL),
                  _const_spec((Q_W, D_MODEL)), _const_spec((LANES, LANES))],
        out_specs=tile(D_MODEL),
        out_shape=jax.ShapeDtypeStruct((b, t, D_MODEL), f32),
        scratch_shapes=[pltpu.VMEM((t, tq), jnp.int32),
                        pltpu.VMEM((t, tq), jnp.int16),
                        pltpu.VMEM((t, tq), f32),
                        pltpu.VMEM((N_KV_HEADS, t, cols), f32),
                        pltpu.VMEM((N_KV_HEADS, t, cols), bf16),
                        pltpu.VMEM((tq, Q_W), bf16)],
        compiler_params=_params(2),
        name="dsa_prompt",
    )(q, qic, kiwi, kic, kb, vt, x, wo, tril)


def _sample_scores_kernel(pt_ref, qic_ref, kiwi_ref, *refs):
    del pt_ref
    *page_refs, o_ref = refs
    qic = qic_ref[...]
    wi = kiwi_ref[...]
    for i, page_ref in enumerate(page_refs):
        page = page_ref[0]
        hi = page.astype(bf16)
        lo = (page - hi.astype(f32)).astype(bf16)
        score = None
        for h in range(IDX_HEADS):
            q_hi = qic[:, h * IDX_K:h * IDX_K + IDX_DIM]
            q_lo = qic[:, h * IDX_K + 2 * IDX_DIM:h * IDX_K + 3 * IDX_DIM]
            logit = _dot(q_hi, hi) + _dot(q_hi, lo) + _dot(q_lo, hi)
            term = wi[:, IDX_DIM + h:IDX_DIM + h + 1] * jnp.maximum(logit, 0.0)
            score = term if score is None else score + term
        o_ref[:, i * PAGE_SIZE:(i + 1) * PAGE_SIZE] = score * IDX_SCALE


def _sample_scores(page_table, qic, kiwi, cache_ki_t, pages_per_step):
    nb, n_pages = page_table.shape
    t = qic.shape[0] // nb
    steps = n_pages // pages_per_step
    row = lambda width: pl.BlockSpec((t, width), lambda b, j, pt: (b, 0))

    def page_spec(i):
        return pl.BlockSpec((1, IDX_DIM, PAGE_SIZE),
                            lambda b, j, pt: (pt[b, j * pages_per_step + i], 0, 0))

    return pl.pallas_call(
        _sample_scores_kernel,
        grid_spec=pltpu.PrefetchScalarGridSpec(
            num_scalar_prefetch=1,
            grid=(nb, steps),
            in_specs=[row(IDX_HEADS * IDX_K), row(LANES)]
            + [page_spec(i) for i in range(pages_per_step)],
            out_specs=pl.BlockSpec((t, pages_per_step * PAGE_SIZE), lambda b, j, pt: (b, j)),
        ),
        out_shape=jax.ShapeDtypeStruct((nb * t, n_pages * PAGE_SIZE), f32),
        compiler_params=_params(2),
        name="dsa_sample_scores",
    )(page_table, qic, kiwi, *([cache_ki_t] * pages_per_step))


def _dsa_sample_kernel(pt_ref, sc_ref, q_ref, qic_ref, kiwi_ref, kicn_ref, kbn_ref, vbn_ref,
                       x_ref, wo_ref, tri_ref, before_ref, *refs, k_top, pps):
    del pt_ref
    k_refs = refs[:pps]
    v_refs = refs[pps:2 * pps]
    o_ref, key_scr, bias_scr, m_scr, l_scr, acc_scr = refs[2 * pps:]
    t = q_ref.shape[0]
    n_past = sc_ref.shape[1]
    n_chunks = n_past // PAGE_SIZE + 1
    j = pl.program_id(1)
    rows = GROUP * t

    @pl.when(j == 0)
    def _():
        new = _indexer_scores(qic_ref[...], kiwi_ref[...], kicn_ref[...])
        new = jnp.concatenate([new, jnp.zeros((t, LANES - t), f32)], axis=1)
        qi = lax.broadcasted_iota(jnp.int32, (t, n_past + LANES), 0)
        kj = lax.broadcasted_iota(jnp.int32, (t, n_past + LANES), 1)
        admissible = kj <= n_past + qi
        score = jnp.concatenate([sc_ref[...], new], axis=1)
        key_scr[...] = _sort_key(score, admissible)

        def store(c, piece):
            bias_scr[c] = piece

        _topk_bias(key_scr, tri_ref, before_ref, k_top, store)
        m_scr[...] = jnp.full(m_scr.shape, -1e30, f32)
        l_scr[...] = jnp.zeros(l_scr.shape, f32)
        acc_scr[...] = jnp.zeros(acc_scr.shape, f32)

    q = q_ref[...]

    def attend(k_heads, v_heads, bias):
        bias4 = jnp.concatenate([bias] * GROUP, axis=0)
        for kvh in range(N_KV_HEADS):
            qs = jnp.concatenate(
                [q[:, (kvh * GROUP + g) * HEAD_DIM:(kvh * GROUP + g + 1) * HEAD_DIM]
                 for g in range(GROUP)], axis=0)
            s = _dot_nt(qs, k_heads[kvh]) * (HEAD_DIM ** -0.5) + bias4
            m_old = m_scr[kvh]
            m_new = jnp.maximum(m_old, jnp.max(s, axis=-1, keepdims=True))
            alpha = jnp.exp(m_old - m_new)
            p = jnp.exp(s - m_new)
            l_scr[kvh] = alpha * l_scr[kvh] + jnp.sum(p, axis=-1, keepdims=True)
            acc_scr[kvh] = alpha * acc_scr[kvh] + _dot(p.astype(bf16), v_heads[kvh])
            m_scr[kvh] = m_new

    def head_rows(page_refs, kvh):
        return jnp.concatenate(
            [r[pl.ds(kvh, PAGE_SIZE, stride=N_KV_HEADS), :].astype(bf16) for r in page_refs],
            axis=0)

    bias = jnp.concatenate([bias_scr[j * pps + i] for i in range(pps)], axis=1)
    attend([head_rows(k_refs, kvh) for kvh in range(N_KV_HEADS)],
           [head_rows(v_refs, kvh) for kvh in range(N_KV_HEADS)], bias)

    @pl.when(j == pl.num_programs(1) - 1)
    def _():
        pad = jnp.zeros((LANES - t, HEAD_DIM), bf16)

        def new_heads(ref):
            return [jnp.concatenate([ref[:, kvh * HEAD_DIM:(kvh + 1) * HEAD_DIM], pad], axis=0)
                    for kvh in range(N_KV_HEADS)]

        attend(new_heads(kbn_ref), new_heads(vbn_ref), bias_scr[n_chunks - 1])
        heads = []
        for kvh in range(N_KV_HEADS):
            o = acc_scr[kvh] / l_scr[kvh]
            heads += [o[g * t:(g + 1) * t] for g in range(GROUP)]
        att = jnp.concatenate(heads, axis=1).astype(bf16)
        o_ref[...] = x_ref[...] + _dot(att, wo_ref[...])


def _dsa_sample(page_table, scores, q, qic, kiwi, kic_new, kb_new, vb_new, x, wo,
                cache_k, cache_v, pages_per_step):
    nb, n_pages = page_table.shape
    t = q.shape[0] // nb
    n_past = n_pages * PAGE_SIZE
    n_all = n_past + LANES
    k_top = min(TOPK_MAX, (n_past + t) // 4)
    steps = n_pages // pages_per_step
    tri, before = _tie_consts(n_all)
    row = lambda width: pl.BlockSpec((t, width), lambda b, j, pt: (b, 0))
    const = lambda shape: pl.BlockSpec(shape, lambda b, j, pt: (0,) * len(shape))

    def page_spec(i):
        return pl.BlockSpec((PAGE_SIZE * N_KV_HEADS, HEAD_DIM),
                            lambda b, j, pt: (pt[b, j * pages_per_step + i], 0))

    rows = GROUP * t
    return pl.pallas_call(
        functools.partial(_dsa_sample_kernel, k_top=k_top, pps=pages_per_step),
        grid_spec=pltpu.PrefetchScalarGridSpec(
            num_scalar_prefetch=1,
            grid=(nb, steps),
            in_specs=[row(n_past), row(Q_W), row(IDX_HEADS * IDX_K), row(LANES), row(IDX_K),
                      row(KV_W), row(KV_W), row(D_MODEL), const((Q_W, D_MODEL)),
                      const((LANES, LANES)), const((n_all, LANES))]
            + [page_spec(i) for i in range(pages_per_step)] * 2,
            out_specs=row(D_MODEL),
            scratch_shapes=[pltpu.VMEM((t, n_all), jnp.int32),
                            pltpu.VMEM((n_all // LANES, t, LANES), f32),
                            pltpu.VMEM((N_KV_HEADS, rows, 1), f32),
                            pltpu.VMEM((N_KV_HEADS, rows, 1), f32),
                            pltpu.VMEM((N_KV_HEADS, rows, HEAD_DIM), f32)],
        ),
        out_shape=jax.ShapeDtypeStruct((nb * t, D_MODEL), f32),
        compiler_params=_params(2),
        name="dsa_sample",
    )(page_table, scores, q, qic, kiwi, kic_new, kb_new, vb_new, x, wo, tri, before,
      *([cache_k] * pages_per_step), *([cache_v] * pages_per_step))


def _ffn_kernel(x_ref, g_ref, wgu_ref, wd_ref, o_ref, *, n_ff):
    x = x_ref[...]
    h = _rms(x, g_ref[...]).astype(bf16)
    acc = x
    width = D_FF // n_ff
    for c in range(n_ff):
        gate = _dot(h, wgu_ref[:, c * width:(c + 1) * width])
        up = _dot(h, wgu_ref[:, D_FF + c * width:D_FF + (c + 1) * width])
        acc = acc + _dot((jax.nn.silu(gate) * up).astype(bf16),
                         wd_ref[c * width:(c + 1) * width, :])
    o_ref[...] = acc


def _ffn(x, g, wgu, wd, tm):
    t = x.shape[0]
    row = pl.BlockSpec((tm, D_MODEL), lambda i: (i, 0))
    resident = lambda a: pl.BlockSpec(a.shape, lambda i: (0,) * a.ndim,
                                      pipeline_mode=pl.Buffered(1))
    return pl.pallas_call(
        functools.partial(_ffn_kernel, n_ff=2),
        grid=(t // tm,),
        in_specs=[row, _const_spec((1, D_MODEL)), resident(wgu), resident(wd)],
        out_specs=row,
        out_shape=jax.ShapeDtypeStruct((t, D_MODEL), f32),
        compiler_params=_params(1),
        name="ffn",
    )(x, g, wgu, wd)


def _sgu_kernel(x_ref, g_ref, win_ref, gv_ref, ws_ref, bias_ref, wout_ref, *refs,
                chunk_len, emit_v):
    if emit_v:
        o_ref, v_ref, gated_scr = refs
    else:
        o_ref, gated_scr = refs
    tm = x_ref.shape[0]
    n_sub = tm // CHUNK
    x = x_ref[...]
    h = _rms(x, g_ref[...]).astype(bf16)
    z = jax.nn.gelu(_dot(h, win_ref[...]))
    u = z[:, :D_MODEL]
    v = _rms(z[:, D_MODEL:], gv_ref[...])
    if emit_v:
        v_ref[...] = v
    vb = v.astype(bf16)
    ti = lax.broadcasted_iota(jnp.int32, (CHUNK, CHUNK), 0)
    si = lax.broadcasted_iota(jnp.int32, (CHUNK, CHUNK), 1)
    shift = chunk_len.bit_length() - 1
    causal = (si <= ti) & (lax.shift_right_logical(si, shift) == lax.shift_right_logical(ti, shift))
    bias = bias_ref[...]
    for grp in range(SG_GROUPS):
        cols = slice(grp * LANES, (grp + 1) * LANES)
        w = jnp.where(causal, ws_ref[grp], 0.0).astype(bf16)
        vg = jnp.concatenate([vb[c * CHUNK:(c + 1) * CHUNK, cols] for c in range(n_sub)], axis=1)
        mixed = _dot(w, vg) + bias[:, grp:grp + 1]
        for c in range(n_sub):
            rows = slice(c * CHUNK, (c + 1) * CHUNK)
            gated_scr[rows, cols] = (u[rows, cols] * mixed[:, c * LANES:(c + 1) * LANES]).astype(bf16)
    o_ref[...] = x + _dot(gated_scr[...], wout_ref[...])


def _sgu(x, g, w_in, g_v, w_s, bias_t, w_out, tm, chunk_len, emit_v):
    t = x.shape[0]
    row = pl.BlockSpec((tm, D_MODEL), lambda i: (i, 0))
    out_shape = [jax.ShapeDtypeStruct((t, D_MODEL), f32)] * (2 if emit_v else 1)
    return pl.pallas_call(
        functools.partial(_sgu_kernel, chunk_len=chunk_len, emit_v=emit_v),
        grid=(t // tm,),
        in_specs=[row, _const_spec((1, D_MODEL)), _const_spec(w_in.shape),
                  _const_spec((1, D_MODEL)), _const_spec(w_s.shape), _const_spec(bias_t.shape),
                  _const_spec(w_out.shape)],
        out_specs=[row] * len(out_shape),
        out_shape=out_shape,
        scratch_shapes=[pltpu.VMEM((tm, D_MODEL), bf16)],
        compiler_params=_params(1),
        name="sgu",
    )(x, g, w_in, g_v, w_s, bias_t, w_out)


def kernel(x_prompt, x_sample, cache_k, cache_v, cache_kidx, page_table, norm_mix, norm_ffn,
           att_w_in, att_q_norm, att_k_norm, att_w_out, sg_w_in, sg_v_norm, sg_w_s, sg_b_s,
           sg_w_out, ffn_w_gu, ffn_w_down):
    nb_p, t_p, _ = x_prompt.shape
    nb_s, t_s, _ = x_sample.shape
    n_pool = cache_k.shape[1]
    xp = x_prompt.reshape(nb_p * t_p, D_MODEL)
    xs = x_sample.reshape(nb_s * t_s, D_MODEL)
    vec = lambda a: a.reshape(1, -1)

    w_in = jnp.pad(att_w_in[0], ((0, 0), (0, ATT_IN_PAD - ATT_IN))).astype(bf16)
    w_out = att_w_out[0].astype(bf16)
    proj = functools.partial(_att_proj, g=vec(norm_mix[0]), w=w_in,
                             gq=vec(att_q_norm[0]), gk=vec(att_k_norm[0]))
    q_p, k_p, v_p, kb_p, _, qic_p, kic_p, ki_p, kiwi_p, vt_p = proj(xp, tm=256)
    q_s, k_s, v_s, kb_s, vb_s, qic_s, kic_s, ki_s, kiwi_s, _ = proj(xs, tm=nb_s * t_s)

    per_seq = lambda a: a.reshape(nb_p, t_p, a.shape[-1])
    xp = _dsa_prompt(per_seq(q_p), per_seq(qic_p), per_seq(kiwi_p), per_seq(kic_p),
                     per_seq(kb_p), vt_p, per_seq(xp), w_out, tq=128, span=512)
    xp = xp.reshape(nb_p * t_p, D_MODEL)

    pages_per_step = 8
    scores_s = _sample_scores(page_table, qic_s, kiwi_s,
                              jnp.swapaxes(cache_kidx[0], 1, 2), pages_per_step)
    kv_rows = n_pool * PAGE_SIZE * N_KV_HEADS
    xs = _dsa_sample(page_table, scores_s, q_s, qic_s, kiwi_s, kic_s, kb_s, vb_s, xs, w_out,
                     cache_k[0].reshape(kv_rows, HEAD_DIM),
                     cache_v[0].reshape(kv_rows, HEAD_DIM), pages_per_step)

    def ffn(x, layer, tm):
        return _ffn(x, vec(norm_ffn[layer]), ffn_w_gu[layer].astype(bf16),
                    ffn_w_down[layer].astype(bf16), tm)

    xp = ffn(xp, 0, 512)
    xs = ffn(xs, 0, nb_s * t_s)

    sg_in = sg_w_in[0].astype(bf16)
    sg_out = sg_w_out[0].astype(bf16)
    sgu = functools.partial(_sgu, g=vec(norm_mix[1]), w_in=sg_in, g_v=vec(sg_v_norm[0]),
                            w_out=sg_out, tm=256)
    (xp,) = sgu(xp, w_s=sg_w_s[0], bias_t=sg_b_s[0].T, chunk_len=CHUNK, emit_v=False)
    reps = CHUNK // t_s
    xs, sgv = sgu(xs, w_s=jnp.tile(sg_w_s[0][:, :t_s, :t_s], (1, reps, reps)),
                  bias_t=jnp.tile(sg_b_s[0][:, :t_s], (1, reps)).T, chunk_len=t_s, emit_v=True)

    xp = ffn(xp, 1, 512)
    xs = ffn(xs, 1, nb_s * t_s)

    y_prompt = xp.reshape(nb_p, t_p, D_MODEL)
    y_sample = xs.reshape(nb_s, t_s, D_MODEL)
    kv_p = lambda a: a.reshape(1, nb_p, t_p, N_KV_HEADS, HEAD_DIM)
    kv_s = lambda a: a.reshape(1, nb_s, t_s, N_KV_HEADS, HEAD_DIM)
    return (y_prompt, y_sample, kv_p(k_p), kv_p(v_p), ki_p.reshape(1, nb_p, t_p, IDX_DIM),
            kv_s(k_s), kv_s(v_s), ki_s.reshape(1, nb_s, t_s, IDX_DIM),
            sgv.reshape(1, nb_s, t_s, D_MODEL))
```

```python
import functools

import jax
import jax.numpy as jnp
from jax import lax
from jax.experimental import pallas as pl
from jax.experimental.pallas import tpu as pltpu

D_MODEL = 1024
HEAD_DIM = 128
N_HEADS = 8
N_KV_HEADS = 2
GROUP = N_HEADS // N_KV_HEADS
IDX_HEADS = 4
IDX_DIM = 64
IDX_SCALE = (IDX_DIM ** -0.5) * (IDX_HEADS ** -0.5)
TOPK_MAX = 256
CHUNK = 128
SG_GROUPS = 8
D_FF = 2816
PAGE_SIZE = 128
EPS = 1e-6

Q_W = N_HEADS * HEAD_DIM
KV_W = N_KV_HEADS * HEAD_DIM
QI_W = IDX_HEADS * IDX_DIM
ATT_IN = Q_W + 2 * KV_W + QI_W + IDX_DIM + IDX_HEADS
LANES = 128
SUBLANES = 8
BF16_ROWS = 16
LOG2E = 1.4426950408889634
ATT_IN_PAD = -(-ATT_IN // LANES) * LANES
IDX_K = 4 * IDX_DIM
INT_MIN = -(2 ** 31)
NEG_INF = float("-inf")
VMEM_LIMIT = 56 * 1024 * 1024

bf16 = jnp.bfloat16
f32 = jnp.float32


def _rms(x, g):
    return x * lax.rsqrt(jnp.mean(x * x, axis=-1, keepdims=True) + EPS) * g


def _dot(a, b):
    return jnp.dot(a, b, preferred_element_type=f32)


def _dot_nt(a, b):
    return lax.dot_general(a, b, (((1,), (1,)), ((), ())), preferred_element_type=f32)


def _hi(x):
    return x.astype(bf16).astype(f32)


def _params(n_grid):
    return pltpu.CompilerParams(
        dimension_semantics=("arbitrary",) * n_grid, vmem_limit_bytes=VMEM_LIMIT)


def _const_spec(shape):
    nd = len(shape)
    return pl.BlockSpec(shape, lambda *_: (0,) * nd)


def _att_proj_kernel(x_ref, g_ref, w_ref, gq_ref, gk_ref,
                     q_ref, k_ref, v_ref, kb_ref, vb_ref, qic_ref, kic_ref, ki_ref, kiwi_ref, vt_ref):
    tm = x_ref.shape[0]
    h = _rms(x_ref[...], g_ref[...]).astype(bf16)
    z = _dot(h, w_ref[...])
    gq = gq_ref[...]
    gk = gk_ref[...]
    for j in range(N_HEADS):
        sl = slice(j * HEAD_DIM, (j + 1) * HEAD_DIM)
        q_ref[:, sl] = _rms(z[:, sl], gq).astype(bf16)
    for j in range(N_KV_HEADS):
        sl = slice(j * HEAD_DIM, (j + 1) * HEAD_DIM)
        kj = _rms(z[:, Q_W + j * HEAD_DIM:Q_W + (j + 1) * HEAD_DIM], gk)
        k_ref[:, sl] = kj
        kb_ref[:, sl] = kj.astype(bf16)
    v = z[:, Q_W + KV_W:Q_W + 2 * KV_W]
    v_ref[...] = v
    vb_ref[...] = v.astype(bf16)
    vt_ref[...] = v.T.astype(bf16)

    low_half = lax.broadcasted_iota(jnp.int32, (tm, LANES), 1) < IDX_DIM
    qi0 = Q_W + 2 * KV_W
    for p in range(IDX_HEADS // 2):
        pair = z[:, qi0 + p * LANES:qi0 + (p + 1) * LANES]
        rolled = pltpu.roll(pair, IDX_DIM, axis=1)
        for e, (own, other) in enumerate(((pair, rolled), (rolled, pair))):
            dup = jnp.where(low_half, own, other)
            single = jnp.where(low_half, own, 0.0)
            base = (2 * p + e) * IDX_K
            qic_ref[:, base:base + LANES] = dup.astype(bf16)
            qic_ref[:, base + LANES:base + 2 * LANES] = (single - _hi(single)).astype(bf16)
    last = z[:, qi0 + QI_W:qi0 + QI_W + LANES]
    rolled = pltpu.roll(last, IDX_DIM, axis=1)
    kic_ref[:, :LANES] = jnp.where(low_half, _hi(last), rolled - _hi(rolled)).astype(bf16)
    kic_ref[:, LANES:] = jnp.where(low_half, _hi(last), 0.0).astype(bf16)
    ki_ref[...] = last[:, :IDX_DIM]
    kiwi_ref[...] = last


def _att_proj(x, g, w, gq, gk, tm):
    t = x.shape[0]
    row = lambda width: pl.BlockSpec((tm, width), lambda i: (i, 0))
    outs = [(Q_W, bf16), (KV_W, f32), (KV_W, f32), (KV_W, bf16), (KV_W, bf16),
            (IDX_HEADS * IDX_K, bf16), (IDX_K, bf16), (IDX_DIM, f32), (LANES, f32)]
    return pl.pallas_call(
        _att_proj_kernel,
        grid=(t // tm,),
        in_specs=[row(D_MODEL), _const_spec((1, D_MODEL)), _const_spec((D_MODEL, ATT_IN_PAD)),
                  _const_spec((1, HEAD_DIM)), _const_spec((1, HEAD_DIM))],
        out_specs=[row(wd) for wd, _ in outs] + [pl.BlockSpec((KV_W, tm), lambda i: (0, i))],
        out_shape=[jax.ShapeDtypeStruct((t, wd), dt) for wd, dt in outs]
        + [jax.ShapeDtypeStruct((KV_W, t), bf16)],
        compiler_params=_params(1),
        name="att_proj",
    )(x, g, w, gq, gk)


def _sort_key(score, admissible):
    bits = lax.bitcast_convert_type(score, jnp.int32)
    key = jnp.where(bits < 0, bits ^ jnp.int32(0x7FFFFFFF), bits)
    return jnp.where(admissible, key, jnp.int32(INT_MIN))


def _count(mask, axis=1):
    ones = jnp.where(mask, 1.0, 0.0)
    if axis == 0:
        n, cols = ones.shape
        rows = 4 * SUBLANES
        ones = jnp.sum(ones.reshape(n // rows, rows, cols), axis=0)
    return jnp.sum(ones, axis=axis, keepdims=True)


def _bisect(count_ge, n_bits, k_need, shape):
    lowest = -(2 ** (n_bits - 1))

    def step(i, t):
        cand = t + lax.shift_left(jnp.int32(1), jnp.int32(n_bits - 1) - i)
        return jnp.where(count_ge(cand) >= k_need, cand, t)

    return lax.fori_loop(0, n_bits, step, jnp.full(shape, lowest, jnp.int32))


def _count_ge_packed(x16_ref, cand):
    n, cols = x16_ref.shape
    cand16 = jnp.broadcast_to(cand, (BF16_ROWS, cols)).astype(jnp.int16)
    parts = [None] * 4
    for r in range(n // BF16_ROWS):
        hit = jnp.where(x16_ref[r * BF16_ROWS:(r + 1) * BF16_ROWS, :] >= cand16,
                        jnp.int16(1), jnp.int16(0))
        a = r % len(parts)
        parts[a] = hit if parts[a] is None else parts[a] + hit
    part = functools.reduce(lambda x, y: x + y, [p for p in parts if p is not None])
    return jnp.sum(part.astype(jnp.int32).astype(f32), axis=0, keepdims=True)


def _topk_bias_keys_major(key_ref, x16_ref, bias_ref, tril_ref, k_top):
    n, cols = key_ref.shape
    half = 2 ** 15
    key = key_ref[...]
    hi = lax.shift_right_arithmetic(key, 16)
    x16_ref[...] = hi.astype(jnp.int16)
    t_hi = _bisect(lambda cand: _count_ge_packed(x16_ref, cand), 16, k_top, (1, cols))
    k_low = k_top - _count(hi > t_hi, axis=0)
    low = (key & 0xFFFF) - half
    x16_ref[...] = jnp.where(hi == t_hi, low, -half).astype(jnp.int16)
    t_lo = _bisect(lambda cand: _count_ge_packed(x16_ref, cand), 16, k_low, (1, cols))
    t = t_hi * (2 * half) + (t_lo + half)

    need = k_top - _count(key > t, axis=0)
    ties_before = jnp.zeros((1, cols), f32)
    for c in range(n // LANES):
        sl = slice(c * LANES, (c + 1) * LANES)
        key_c = key_ref[sl, :]
        tie = key_c == t
        tie_f = jnp.where(tie, 1.0, 0.0)
        rank = _dot(tril_ref[...], tie_f.astype(bf16)) + ties_before
        ties_before = ties_before + jnp.sum(tie_f, axis=0, keepdims=True)
        sel = (key_c > t) | (tie & (rank <= need))
        bias_ref[sl, :] = jnp.where(sel & (key_c > INT_MIN), 0.0, NEG_INF)


def _topk_bias(key_ref, tri_ref, before_ref, k_top, store):
    rows, n = key_ref.shape
    t = _bisect(lambda cand: _count(key_ref[...] >= cand), 32, k_top, (rows, 1))
    key = key_ref[...]
    need = k_top - _count(key > t)
    tie_b = jnp.where(key == t, 1.0, 0.0).astype(bf16)
    before = _dot(tie_b, before_ref[:n, :])
    for c in range(n // LANES):
        sl = slice(c * LANES, (c + 1) * LANES)
        key_c = key[:, sl]
        rank = _dot(tie_b[:, sl], tri_ref[...]) + before[:, c:c + 1]
        sel = (key_c > t) | ((key_c == t) & (rank <= need))
        store(c, jnp.where(sel & (key_c > INT_MIN), 0.0, NEG_INF))


def _tie_consts(n):
    a = jnp.arange(LANES)
    tri = (a[:, None] <= a[None, :]).astype(bf16)
    before = ((jnp.arange(n)[:, None] // LANES) < a[None, :]).astype(bf16)
    return tri, before


def _indexer_scores(qic, wi, kic):
    rows = qic.shape[0]
    heads = [qic[:, h * IDX_K:(h + 1) * IDX_K] for h in range(IDX_HEADS)]
    if rows % BF16_ROWS == 0:
        logits = _dot_nt(jnp.concatenate(heads, axis=0), kic)
        logits = [logits[h * rows:(h + 1) * rows] for h in range(IDX_HEADS)]
    else:
        logits = [_dot_nt(qh, kic) for qh in heads]
    score = None
    for h in range(IDX_HEADS):
        relu = jnp.maximum(logits[h], 0.0)
        term = wi[:, IDX_DIM + h:IDX_DIM + h + 1] * relu
        score = term if score is None else score + term
    return score * IDX_SCALE


def _dsa_prompt_tile(lk, q0, k_top, q_ref, qic_ref, kiwi_ref, kic_ref, kb_ref, vt_ref, tril_ref,
                     key_scr, x16_scr, bias_scr, s_scr, p_scr, att_scr):
    tq = q_ref.shape[1]
    qic = qic_ref[0]
    q_idx = jnp.concatenate(
        [qic[:, h * IDX_K:(h + 1) * IDX_K] for h in range(IDX_HEADS)], axis=0)
    logits = _dot_nt(kic_ref[0, :lk, :], q_idx)
    wi_t = kiwi_ref[0].T
    score = None
    for h in range(IDX_HEADS):
        term = wi_t[IDX_DIM + h:IDX_DIM + h + 1, :] * jnp.maximum(logits[:, h * tq:(h + 1) * tq], 0.0)
        score = term if score is None else score + term
    score = score * IDX_SCALE
    q_pos = q0 + lax.broadcasted_iota(jnp.int32, (1, tq), 1)
    admissible = lax.broadcasted_iota(jnp.int32, (lk, 1), 0) <= q_pos
    keys = key_scr.at[:lk, :]
    keys[...] = _sort_key(score, admissible)

    q = q_ref[0]
    for kvh in range(N_KV_HEADS):
        qs = jnp.concatenate(
            [q[:, (kvh * GROUP + g) * HEAD_DIM:(kvh * GROUP + g + 1) * HEAD_DIM]
             for g in range(GROUP)], axis=0)
        kv_sl = slice(kvh * HEAD_DIM, (kvh + 1) * HEAD_DIM)
        s_scr[kvh, :lk, :] = _dot_nt(kb_ref[0, :lk, kv_sl], qs)

    bias_ref = bias_scr.at[:lk, :]
    _topk_bias_keys_major(keys, x16_scr.at[:lk, :], bias_ref, tril_ref, k_top)

    exp_scale = (HEAD_DIM ** -0.5) * LOG2E
    for kvh in range(N_KV_HEADS):
        kv_sl = slice(kvh * HEAD_DIM, (kvh + 1) * HEAD_DIM)
        denoms = []
        for g in range(GROUP):
            cols = slice(g * tq, (g + 1) * tq)
            sb = s_scr[kvh, :lk, cols] + bias_ref[...]
            m = jnp.max(sb, axis=0, keepdims=True)
            p = jnp.exp2((sb - m) * exp_scale)
            denoms.append(jnp.sum(p, axis=0, keepdims=True))
            p_scr[kvh, :lk, cols] = p.astype(bf16)
        o_t = _dot(vt_ref[kv_sl, :lk], p_scr[kvh, :lk, :])
        o_t = o_t / jnp.concatenate(denoms, axis=1)
        for g in range(GROUP):
            col = (kvh * GROUP + g) * HEAD_DIM
            att_scr[:, col:col + HEAD_DIM] = o_t[:, g * tq:(g + 1) * tq].T.astype(bf16)


def _dsa_prompt_kernel(q_ref, qic_ref, kiwi_ref, kic_ref, kb_ref, vt_ref, x_ref, wo_ref,
                       tril_ref, o_ref, *scratch, k_top, span):
    tq = q_ref.shape[1]
    n = kb_ref.shape[1]
    q0 = pl.program_id(1) * tq
    span_class = (q0 + tq - 1) // span
    for c in range(n // span):
        @pl.when(span_class == c)
        def _(lk=(c + 1) * span):
            _dsa_prompt_tile(lk, q0, k_top, q_ref, qic_ref, kiwi_ref, kic_ref, kb_ref, vt_ref,
                             tril_ref, *scratch)
    att_scr = scratch[-1]
    o_ref[0] = x_ref[0] + _dot(att_scr[...], wo_ref[...])


def _dsa_prompt(q, qic, kiwi, kic, kb, vt, x, wo, tq, span):
    b, t, _ = q.shape
    k_top = min(TOPK_MAX, t // 4)
    a = jnp.arange(LANES)
    tril = (a[None, :] <= a[:, None]).astype(bf16)
    cols = GROUP * tq
    tile = lambda width: pl.BlockSpec((1, tq, width), lambda i, j: (i, j, 0))
    full = lambda width: pl.BlockSpec((1, t, width), lambda i, j: (i, 0, 0))
    return pl.pallas_call(
        functools.partial(_dsa_prompt_kernel, k_top=k_top, span=span),
        grid=(b, t // tq),
        in_specs=[tile(Q_W), tile(IDX_HEADS * IDX_K), tile(LANES), full(IDX_K), full(KV_W),
                  pl.BlockSpec((KV_W, t), lambda i, j: (0, i)), tile(D_MODEL),
                  _const_spec((Q_W, D_MODEL)), _const_spec((LANES, LANES))],
        out_specs=tile(D_MODEL),
        out_shape=jax.ShapeDtypeStruct((b, t, D_MODEL), f32),
        scratch_shapes=[pltpu.VMEM((t, tq), jnp.int32),
                        pltpu.VMEM((t, tq), jnp.int16),
                        pltpu.VMEM((t, tq), f32),
                        pltpu.VMEM((N_KV_HEADS, t, cols), f32),
                        pltpu.VMEM((N_KV_HEADS, t, cols), bf16),
                        pltpu.VMEM((tq, Q_W), bf16)],
        compiler_params=_params(2),
        name="dsa_prompt",
    )(q, qic, kiwi, kic, kb, vt, x, wo, tril)


def _sample_scores_kernel(pt_ref, qic_ref, kiwi_ref, kicn_ref, cache_ref, o_ref, buf, sem):
    b = pl.program_id(0)
    n_pages = pt_ref.shape[1]
    n_past = n_pages * PAGE_SIZE
    t = qic_ref.shape[0]

    def page_copies(batch, slot):
        return [pltpu.make_async_copy(
            cache_ref.at[pt_ref[batch, p]],
            buf.at[slot, :, pl.ds(p * PAGE_SIZE, PAGE_SIZE)], sem.at[slot])
            for p in range(n_pages)]

    slot = lax.rem(b, 2)

    @pl.when(b == 0)
    def _():
        for copy in page_copies(0, 0):
            copy.start()

    @pl.when(b + 1 < pl.num_programs(0))
    def _():
        for copy in page_copies(b + 1, 1 - slot):
            copy.start()

    for copy in page_copies(b, slot):
        copy.wait()

    keys_t = buf[slot]
    k_hi = keys_t.astype(bf16)
    k_lo = (keys_t - k_hi.astype(f32)).astype(bf16)
    qic = qic_ref[...].astype(f32)

    def stacked(offset):
        return jnp.concatenate(
            [qic[:, h * IDX_K + offset:h * IDX_K + offset + IDX_DIM] for h in range(IDX_HEADS)],
            axis=0).astype(bf16)

    q_hi, q_lo = stacked(0), stacked(2 * IDX_DIM)
    logits = _dot(q_hi, k_hi) + _dot(q_hi, k_lo) + _dot(q_lo, k_hi)
    wi = kiwi_ref[...]
    score = None
    for h in range(IDX_HEADS):
        term = wi[:, IDX_DIM + h:IDX_DIM + h + 1] * jnp.maximum(logits[h * t:(h + 1) * t], 0.0)
        score = term if score is None else score + term
    o_ref[:, :n_past] = score * IDX_SCALE
    new = _indexer_scores(qic_ref[...], wi, kicn_ref[...])
    o_ref[:, n_past:] = jnp.concatenate([new, jnp.zeros((t, LANES - t), f32)], axis=1)


def _sample_scores(page_table, qic, kiwi, kic_new, cache_ki_t):
    nb, n_pages = page_table.shape
    t = qic.shape[0] // nb
    n_past = n_pages * PAGE_SIZE
    row = lambda width: pl.BlockSpec((t, width), lambda b, pt: (b, 0))
    return pl.pallas_call(
        _sample_scores_kernel,
        grid_spec=pltpu.PrefetchScalarGridSpec(
            num_scalar_prefetch=1,
            grid=(nb,),
            in_specs=[row(IDX_HEADS * IDX_K), row(LANES), row(IDX_K),
                      pl.BlockSpec(memory_space=pl.ANY)],
            out_specs=row(n_past + LANES),
            scratch_shapes=[pltpu.VMEM((2, IDX_DIM, n_past), f32),
                            pltpu.SemaphoreType.DMA((2,))],
        ),
        out_shape=jax.ShapeDtypeStruct((nb * t, n_past + LANES), f32),
        compiler_params=_params(1),
        name="dsa_sample_scores",
    )(page_table, qic, kiwi, kic_new, cache_ki_t)


def _sample_select_kernel(sc_ref, tri_ref, before_ref, o_ref, key_scr, *, k_top, t, n_past):
    rows, n = sc_ref.shape
    qi = lax.broadcasted_iota(jnp.int32, (rows, n), 0) & (t - 1)
    kj = lax.broadcasted_iota(jnp.int32, (rows, n), 1)
    admissible = kj <= n_past + qi
    key_scr[...] = _sort_key(sc_ref[...], admissible)

    def store(c, piece):
        o_ref[:, c * LANES:(c + 1) * LANES] = piece

    _topk_bias(key_scr, tri_ref, before_ref, k_top, store)


def _sample_select(scores, t, n_past, rows_per_step):
    n_rows, n_all = scores.shape
    assert t & (t - 1) == 0 and rows_per_step % t == 0, (t, rows_per_step)
    k_top = min(TOPK_MAX, (n_past + t) // 4)
    tri, before = _tie_consts(n_all)
    block = pl.BlockSpec((rows_per_step, n_all), lambda i: (i, 0))
    return pl.pallas_call(
        functools.partial(_sample_select_kernel, k_top=k_top, t=t, n_past=n_past),
        grid=(n_rows // rows_per_step,),
        in_specs=[block, _const_spec((LANES, LANES)), _const_spec((n_all, LANES))],
        out_specs=block,
        out_shape=jax.ShapeDtypeStruct((n_rows, n_all), f32),
        scratch_shapes=[pltpu.VMEM((rows_per_step, n_all), jnp.int32)],
        compiler_params=_params(1),
        name="dsa_sample_select",
    )(scores, tri, before)


def _sample_attend_kernel(pt_ref, bias_ref, biasn_ref, q_ref, kbn_ref, vbn_ref, x_ref, wo_ref,
                          ck_ref, cv_ref, o_ref, kbuf, vbuf, ksem, vsem, m_scr, l_scr, acc_scr,
                          *, pps):
    b, j = pl.program_id(0), pl.program_id(1)
    n_j = pl.num_programs(1)
    step = b * n_j + j
    t = q_ref.shape[0]
    page_rows = PAGE_SIZE * N_KV_HEADS

    def page_copies(batch, chunk, slot):
        copies = []
        for i in range(pps):
            src = pl.ds(pl.multiple_of(pt_ref[batch, chunk * pps + i] * page_rows, page_rows),
                        page_rows)
            dst = pl.ds(i * page_rows, page_rows)
            copies.append(pltpu.make_async_copy(ck_ref.at[src, :], kbuf.at[slot, dst, :], ksem.at[slot]))
            copies.append(pltpu.make_async_copy(cv_ref.at[src, :], vbuf.at[slot, dst, :], vsem.at[slot]))
        return copies

    slot = lax.rem(step, 2)

    @pl.when(step == 0)
    def _():
        for copy in page_copies(0, 0, 0):
            copy.start()

    @pl.when(step + 1 < pl.num_programs(0) * n_j)
    def _():
        nxt = step + 1
        for copy in page_copies(nxt // n_j, lax.rem(nxt, n_j), 1 - slot):
            copy.start()

    @pl.when(j == 0)
    def _():
        m_scr[...] = jnp.full(m_scr.shape, -1e30, f32)
        l_scr[...] = jnp.zeros(l_scr.shape, f32)
        acc_scr[...] = jnp.zeros(acc_scr.shape, f32)

    q = q_ref[...]

    def attend(k_heads, v_heads, bias):
        bias4 = jnp.concatenate([bias] * GROUP, axis=0)
        for kvh in range(N_KV_HEADS):
            qs = jnp.concatenate(
                [q[:, (kvh * GROUP + g) * HEAD_DIM:(kvh * GROUP + g + 1) * HEAD_DIM]
                 for g in range(GROUP)], axis=0)
            s = _dot_nt(qs, k_heads[kvh]) * (HEAD_DIM ** -0.5) + bias4
            m_old = m_scr[kvh]
            m_new = jnp.maximum(m_old, jnp.max(s, axis=-1, keepdims=True))
            alpha = jnp.exp(m_old - m_new)
            p = jnp.exp(s - m_new)
            l_scr[kvh] = alpha * l_scr[kvh] + jnp.sum(p, axis=-1, keepdims=True)
            acc_scr[kvh] = alpha * acc_scr[kvh] + _dot(p.astype(bf16), v_heads[kvh])
            m_scr[kvh] = m_new

    for copy in page_copies(b, j, slot):
        copy.wait()

    def head_rows(buf, kvh):
        return buf[slot, pl.ds(kvh, pps * PAGE_SIZE, stride=N_KV_HEADS), :].astype(bf16)

    attend([head_rows(kbuf, kvh) for kvh in range(N_KV_HEADS)],
           [head_rows(vbuf, kvh) for kvh in range(N_KV_HEADS)], bias_ref[...])

    @pl.when(j == n_j - 1)
    def _():
        pad = jnp.zeros((LANES - t, HEAD_DIM), bf16)

        def new_heads(ref):
            return [jnp.concatenate([ref[:, kvh * HEAD_DIM:(kvh + 1) * HEAD_DIM], pad], axis=0)
                    for kvh in range(N_KV_HEADS)]

        attend(new_heads(kbn_ref), new_heads(vbn_ref), biasn_ref[...])
        heads = []
        for kvh in range(N_KV_HEADS):
            o = acc_scr[kvh] / l_scr[kvh]
            heads += [o[g * t:(g + 1) * t] for g in range(GROUP)]
        att = jnp.concatenate(heads, axis=1).astype(bf16)
        o_ref[...] = x_ref[...] + _dot(att, wo_ref[...])


def _sample_attend(page_table, bias, q, kb_new, vb_new, x, wo, cache_k, cache_v, pages_per_step):
    nb, n_pages = page_table.shape
    t = q.shape[0] // nb
    steps = n_pages // pages_per_step
    span = pages_per_step * PAGE_SIZE
    row = lambda width: pl.BlockSpec((t, width), lambda b, j, pt: (b, 0))
    rows = GROUP * t
    buf = pltpu.VMEM((2, span * N_KV_HEADS, HEAD_DIM), f32)
    return pl.pallas_call(
        functools.partial(_sample_attend_kernel, pps=pages_per_step),
        grid_spec=pltpu.PrefetchScalarGridSpec(
            num_scalar_prefetch=1,
            grid=(nb, steps),
            in_specs=[pl.BlockSpec((t, span), lambda b, j, pt: (b, j)),
                      pl.BlockSpec((t, LANES), lambda b, j, pt: (b, n_pages)),
                      row(Q_W), row(KV_W), row(KV_W), row(D_MODEL),
                      pl.BlockSpec((Q_W, D_MODEL), lambda b, j, pt: (0, 0)),
                      pl.BlockSpec(memory_space=pl.ANY), pl.BlockSpec(memory_space=pl.ANY)],
            out_specs=row(D_MODEL),
            scratch_shapes=[buf, buf, pltpu.SemaphoreType.DMA((2,)), pltpu.SemaphoreType.DMA((2,)),
                            pltpu.VMEM((N_KV_HEADS, rows, 1), f32),
                            pltpu.VMEM((N_KV_HEADS, rows, 1), f32),
                            pltpu.VMEM((N_KV_HEADS, rows, HEAD_DIM), f32)],
        ),
        out_shape=jax.ShapeDtypeStruct((nb * t, D_MODEL), f32),
        compiler_params=_params(2),
        name="dsa_sample_attend",
    )(page_table, bias, bias, q, kb_new, vb_new, x, wo, cache_k, cache_v)


def _ffn_kernel(x_ref, g_ref, wgu_ref, wd_ref, o_ref, *, n_ff):
    x = x_ref[...]
    h = _rms(x, g_ref[...]).astype(bf16)
    acc = x
    width = D_FF // n_ff
    for c in range(n_ff):
        gate = _dot(h, wgu_ref[:, c * width:(c + 1) * width])
        up = _dot(h, wgu_ref[:, D_FF + c * width:D_FF + (c + 1) * width])
        acc = acc + _dot((jax.nn.silu(gate) * up).astype(bf16),
                         wd_ref[c * width:(c + 1) * width, :])
    o_ref[...] = acc


def _ffn(x, g, wgu, wd, tm):
    t = x.shape[0]
    row = pl.BlockSpec((tm, D_MODEL), lambda i: (i, 0))
    resident = lambda a: pl.BlockSpec(a.shape, lambda i: (0,) * a.ndim,
                                      pipeline_mode=pl.Buffered(1))
    return pl.pallas_call(
        functools.partial(_ffn_kernel, n_ff=2),
        grid=(t // tm,),
        in_specs=[row, _const_spec((1, D_MODEL)), resident(wgu), resident(wd)],
        out_specs=row,
        out_shape=jax.ShapeDtypeStruct((t, D_MODEL), f32),
        compiler_params=_params(1),
        name="ffn",
    )(x, g, wgu, wd)


def _sgu_kernel(x_ref, g_ref, win_ref, gv_ref, ws_ref, bias_ref, wout_ref, *refs,
                chunk_len, emit_v):
    if emit_v:
        o_ref, v_ref, gated_scr = refs
    else:
        o_ref, gated_scr = refs
    tm = x_ref.shape[0]
    n_sub = tm // CHUNK
    x = x_ref[...]
    h = _rms(x, g_ref[...]).astype(bf16)
    z = jax.nn.gelu(_dot(h, win_ref[...]))
    u = z[:, :D_MODEL]
    v = _rms(z[:, D_MODEL:], gv_ref[...])
    if emit_v:
        v_ref[...] = v
    vb = v.astype(bf16)
    ti = lax.broadcasted_iota(jnp.int32, (CHUNK, CHUNK), 0)
    si = lax.broadcasted_iota(jnp.int32, (CHUNK, CHUNK), 1)
    shift = chunk_len.bit_length() - 1
    causal = (si <= ti) & (lax.shift_right_logical(si, shift) == lax.shift_right_logical(ti, shift))
    bias = bias_ref[...]
    for grp in range(SG_GROUPS):
        cols = slice(grp * LANES, (grp + 1) * LANES)
        w = jnp.where(causal, ws_ref[grp], 0.0).astype(bf16)
        vg = jnp.concatenate([vb[c * CHUNK:(c + 1) * CHUNK, cols] for c in range(n_sub)], axis=1)
        mixed = _dot(w, vg) + bias[:, grp:grp + 1]
        for c in range(n_sub):
            rows = slice(c * CHUNK, (c + 1) * CHUNK)
            gated_scr[rows, cols] = (u[rows, cols] * mixed[:, c * LANES:(c + 1) * LANES]).astype(bf16)
    o_ref[...] = x + _dot(gated_scr[...], wout_ref[...])


def _sgu(x, g, w_in, g_v, w_s, bias_t, w_out, tm, chunk_len, emit_v):
    t = x.shape[0]
    row = pl.BlockSpec((tm, D_MODEL), lambda i: (i, 0))
    out_shape = [jax.ShapeDtypeStruct((t, D_MODEL), f32)] * (2 if emit_v else 1)
    return pl.pallas_call(
        functools.partial(_sgu_kernel, chunk_len=chunk_len, emit_v=emit_v),
        grid=(t // tm,),
        in_specs=[row, _const_spec((1, D_MODEL)), _const_spec(w_in.shape),
                  _const_spec((1, D_MODEL)), _const_spec(w_s.shape), _const_spec(bias_t.shape),
                  _const_spec(w_out.shape)],
        out_specs=[row] * len(out_shape),
        out_shape=out_shape,
        scratch_shapes=[pltpu.VMEM((tm, D_MODEL), bf16)],
        compiler_params=_params(1),
        name="sgu",
    )(x, g, w_in, g_v, w_s, bias_t, w_out)


def kernel(x_prompt, x_sample, cache_k, cache_v, cache_kidx, page_table, norm_mix, norm_ffn,
           att_w_in, att_q_norm, att_k_norm, att_w_out, sg_w_in, sg_v_norm, sg_w_s, sg_b_s,
           sg_w_out, ffn_w_gu, ffn_w_down):
    nb_p, t_p, _ = x_prompt.shape
    nb_s, t_s, _ = x_sample.shape
    n_pool = cache_k.shape[1]
    xp = x_prompt.reshape(nb_p * t_p, D_MODEL)
    xs = x_sample.reshape(nb_s * t_s, D_MODEL)
    vec = lambda a: a.reshape(1, -1)

    w_in = jnp.pad(att_w_in[0], ((0, 0), (0, ATT_IN_PAD - ATT_IN))).astype(bf16)
    w_out = att_w_out[0].astype(bf16)
    proj = functools.partial(_att_proj, g=vec(norm_mix[0]), w=w_in,
                             gq=vec(att_q_norm[0]), gk=vec(att_k_norm[0]))
    q_p, k_p, v_p, kb_p, _, qic_p, kic_p, ki_p, kiwi_p, vt_p = proj(xp, tm=256)
    q_s, k_s, v_s, kb_s, vb_s, qic_s, kic_s, ki_s, kiwi_s, _ = proj(xs, tm=nb_s * t_s)

    per_seq = lambda a: a.reshape(nb_p, t_p, a.shape[-1])
    xp = _dsa_prompt(per_seq(q_p), per_seq(qic_p), per_seq(kiwi_p), per_seq(kic_p),
                     per_seq(kb_p), vt_p, per_seq(xp), w_out, tq=128, span=512)
    xp = xp.reshape(nb_p * t_p, D_MODEL)

    scores_s = _sample_scores(page_table, qic_s, kiwi_s, kic_s, jnp.swapaxes(cache_kidx[0], 1, 2))
    bias_s = _sample_select(scores_s, t_s, page_table.shape[1] * PAGE_SIZE, rows_per_step=64)
    kv_rows = n_pool * PAGE_SIZE * N_KV_HEADS
    xs = _sample_attend(page_table, bias_s, q_s, kb_s, vb_s, xs, w_out,
                        cache_k[0].reshape(kv_rows, HEAD_DIM),
                        cache_v[0].reshape(kv_rows, HEAD_DIM), pages_per_step=16)

    def ffn(x, layer, tm):
        return _ffn(x, vec(norm_ffn[layer]), ffn_w_gu[layer].astype(bf16),
                    ffn_w_down[layer].astype(bf16), tm)

    xp = ffn(xp, 0, 512)
    xs = ffn(xs, 0, nb_s * t_s)

    sg_in = sg_w_in[0].astype(bf16)
    sg_out = sg_w_out[0].astype(bf16)
    sgu = functools.partial(_sgu, g=vec(norm_mix[1]), w_in=sg_in, g_v=vec(sg_v_norm[0]),
                            w_out=sg_out, tm=256)
    (xp,) = sgu(xp, w_s=sg_w_s[0], bias_t=sg_b_s[0].T, chunk_len=CHUNK, emit_v=False)
    reps = CHUNK // t_s
    xs, sgv = sgu(xs, w_s=jnp.tile(sg_w_s[0][:, :t_s, :t_s], (1, reps, reps)),
                  bias_t=jnp.tile(sg_b_s[0][:, :t_s], (1, reps)).T, chunk_len=t_s, emit_v=True)

    xp = ffn(xp, 1, 512)
    xs = ffn(xs, 1, nb_s * t_s)

    y_prompt = xp.reshape(nb_p, t_p, D_MODEL)
    y_sample = xs.reshape(nb_s, t_s, D_MODEL)
    kv_p = lambda a: a.reshape(1, nb_p, t_p, N_KV_HEADS, HEAD_DIM)
    kv_s = lambda a: a.reshape(1, nb_s, t_s, N_KV_HEADS, HEAD_DIM)
    return (y_prompt, y_sample, kv_p(k_p), kv_p(v_p), ki_p.reshape(1, nb_p, t_p, IDX_DIM),
            kv_s(k_s), kv_s(v_s), ki_s.reshape(1, nb_s, t_s, IDX_DIM),
            sgv.reshape(1, nb_s, t_s, D_MODEL))
```

```python
import functools

import jax
import jax.numpy as jnp
from jax import lax
from jax.experimental import pallas as pl
from jax.experimental.pallas import tpu as pltpu

D_MODEL = 1024
HEAD_DIM = 128
N_HEADS = 8
N_KV_HEADS = 2
GROUP = N_HEADS // N_KV_HEADS
IDX_HEADS = 4
IDX_DIM = 64
IDX_SCALE = (IDX_DIM ** -0.5) * (IDX_HEADS ** -0.5)
TOPK_MAX = 256
CHUNK = 128
SG_GROUPS = 8
D_FF = 2816
PAGE_SIZE = 128
EPS = 1e-6

Q_W = N_HEADS * HEAD_DIM
KV_W = N_KV_HEADS * HEAD_DIM
QI_W = IDX_HEADS * IDX_DIM
ATT_IN = Q_W + 2 * KV_W + QI_W + IDX_DIM + IDX_HEADS
LANES = 128
SUBLANES = 8
BF16_ROWS = 16
LOG2E = 1.4426950408889634
ATT_IN_PAD = -(-ATT_IN // LANES) * LANES
IDX_K = 4 * IDX_DIM
INT_MIN = -(2 ** 31)
NEG_INF = float("-inf")
VMEM_LIMIT = 56 * 1024 * 1024

bf16 = jnp.bfloat16
f32 = jnp.float32


def _rms(x, g):
    return x * lax.rsqrt(jnp.mean(x * x, axis=-1, keepdims=True) + EPS) * g


def _dot(a, b):
    return jnp.dot(a, b, preferred_element_type=f32)


def _dot_nt(a, b):
    return lax.dot_general(a, b, (((1,), (1,)), ((), ())), preferred_element_type=f32)


def _hi(x):
    return x.astype(bf16).astype(f32)


def _params(n_grid):
    return pltpu.CompilerParams(
        dimension_semantics=("arbitrary",) * n_grid, vmem_limit_bytes=VMEM_LIMIT)


def _const_spec(shape):
    nd = len(shape)
    return pl.BlockSpec(shape, lambda *_: (0,) * nd)


def _att_proj_kernel(x_ref, g_ref, w_ref, gq_ref, gk_ref,
                     q_ref, k_ref, v_ref, kb_ref, vb_ref, qic_ref, kic_ref, ki_ref, kiwi_ref, vt_ref):
    tm = x_ref.shape[0]
    h = _rms(x_ref[...], g_ref[...]).astype(bf16)
    z = _dot(h, w_ref[...])
    gq = gq_ref[...]
    gk = gk_ref[...]
    for j in range(N_HEADS):
        sl = slice(j * HEAD_DIM, (j + 1) * HEAD_DIM)
        q_ref[:, sl] = _rms(z[:, sl], gq).astype(bf16)
    v = z[:, Q_W + KV_W:Q_W + 2 * KV_W]
    for j in range(N_KV_HEADS):
        sl = slice(j * HEAD_DIM, (j + 1) * HEAD_DIM)
        kj = _rms(z[:, Q_W + j * HEAD_DIM:Q_W + (j + 1) * HEAD_DIM], gk)
        kb_ref[:, sl] = kj.astype(bf16)
        head_rows = pl.ds(j, tm, stride=N_KV_HEADS)
        k_ref[head_rows, :] = kj
        v_ref[head_rows, :] = v[:, sl]
    vb_ref[...] = v.astype(bf16)
    vt_ref[...] = v.T.astype(bf16)

    low_half = lax.broadcasted_iota(jnp.int32, (tm, LANES), 1) < IDX_DIM
    qi0 = Q_W + 2 * KV_W
    for p in range(IDX_HEADS // 2):
        pair = z[:, qi0 + p * LANES:qi0 + (p + 1) * LANES]
        rolled = pltpu.roll(pair, IDX_DIM, axis=1)
        for e, (own, other) in enumerate(((pair, rolled), (rolled, pair))):
            dup = jnp.where(low_half, own, other)
            single = jnp.where(low_half, own, 0.0)
            base = (2 * p + e) * IDX_K
            qic_ref[:, base:base + LANES] = dup.astype(bf16)
            qic_ref[:, base + LANES:base + 2 * LANES] = (single - _hi(single)).astype(bf16)
    last = z[:, qi0 + QI_W:qi0 + QI_W + LANES]
    rolled = pltpu.roll(last, IDX_DIM, axis=1)
    kic_ref[:, :LANES] = jnp.where(low_half, _hi(last), rolled - _hi(rolled)).astype(bf16)
    kic_ref[:, LANES:] = jnp.where(low_half, _hi(last), 0.0).astype(bf16)
    ki_ref[...] = last[:, :IDX_DIM]
    kiwi_ref[...] = last


def _att_proj(x, g, w, gq, gk, tm):
    t = x.shape[0]
    row = lambda width: pl.BlockSpec((tm, width), lambda i: (i, 0))
    head_rows = pl.BlockSpec((N_KV_HEADS * tm, HEAD_DIM), lambda i: (i, 0))
    head_rows_shape = jax.ShapeDtypeStruct((N_KV_HEADS * t, HEAD_DIM), f32)
    outs = [(Q_W, bf16), None, None, (KV_W, bf16), (KV_W, bf16),
            (IDX_HEADS * IDX_K, bf16), (IDX_K, bf16), (IDX_DIM, f32), (LANES, f32)]
    return pl.pallas_call(
        _att_proj_kernel,
        grid=(t // tm,),
        in_specs=[row(D_MODEL), _const_spec((1, D_MODEL)), _const_spec((D_MODEL, ATT_IN_PAD)),
                  _const_spec((1, HEAD_DIM)), _const_spec((1, HEAD_DIM))],
        out_specs=[row(o[0]) if o else head_rows for o in outs]
        + [pl.BlockSpec((KV_W, tm), lambda i: (0, i))],
        out_shape=[jax.ShapeDtypeStruct((t, o[0]), o[1]) if o else head_rows_shape for o in outs]
        + [jax.ShapeDtypeStruct((KV_W, t), bf16)],
        compiler_params=_params(1),
        name="att_proj",
    )(x, g, w, gq, gk)


def _key_to_float(key):
    bits = jnp.where(key < 0, key ^ jnp.int32(0x7FFFFFFF), key)
    return lax.bitcast_convert_type(bits, f32)


def _count(mask, axis=1):
    ones = jnp.where(mask, 1.0, 0.0)
    if axis == 0:
        n, cols = ones.shape
        rows = 4 * SUBLANES
        ones = jnp.sum(ones.reshape(n // rows, rows, cols), axis=0)
    return jnp.sum(ones, axis=axis, keepdims=True)


def _rank_threshold(score_ref, k_top, axis):
    shape = tuple(1 if a == axis else d for a, d in enumerate(score_ref.shape))

    def step(i, t):
        cand = t + lax.shift_left(jnp.int32(1), jnp.int32(31) - i)
        enough = _count(score_ref[...] >= _key_to_float(cand), axis) >= k_top
        return jnp.where(enough, cand, t)

    t = lax.fori_loop(0, 32, step, jnp.full(shape, INT_MIN, jnp.int32))
    return _key_to_float(t), t == INT_MIN


def _topk_bias_keys_major(score_ref, bias_ref, tril_ref, k_top):
    n, _ = score_ref.shape
    t, take_all = _rank_threshold(score_ref, k_top, axis=0)

    def above(s):
        return (s > t) | (take_all & (s > NEG_INF))

    need = k_top - _count(above(score_ref[...]), axis=0)
    ties_before = jnp.zeros(t.shape, f32)
    for c in range(n // LANES):
        sl = slice(c * LANES, (c + 1) * LANES)
        s_c = score_ref[sl, :]
        tie = s_c == t
        tie_f = jnp.where(tie, 1.0, 0.0)
        rank = _dot(tril_ref[...], tie_f.astype(bf16)) + ties_before
        ties_before = ties_before + jnp.sum(tie_f, axis=0, keepdims=True)
        bias_ref[sl, :] = jnp.where(above(s_c) | (tie & (rank <= need)), 0.0, NEG_INF)


def _topk_bias(score_ref, tri_ref, before_ref, k_top, store):
    _, n = score_ref.shape
    t, take_all = _rank_threshold(score_ref, k_top, axis=1)
    def above(s):
        return (s > t) | (take_all & (s > NEG_INF))

    s = score_ref[...]
    need = k_top - _count(above(s))
    tie_b = jnp.where(s == t, 1.0, 0.0).astype(bf16)
    before = _dot(tie_b, before_ref[:n, :])
    for c in range(n // LANES):
        sl = slice(c * LANES, (c + 1) * LANES)
        s_c = s[:, sl]
        rank = _dot(tie_b[:, sl], tri_ref[...]) + before[:, c:c + 1]
        store(c, jnp.where(above(s_c) | ((s_c == t) & (rank <= need)), 0.0, NEG_INF))


def _tie_consts(n):
    a = jnp.arange(LANES)
    tri = (a[:, None] <= a[None, :]).astype(bf16)
    before = ((jnp.arange(n)[:, None] // LANES) < a[None, :]).astype(bf16)
    return tri, before


def _indexer_scores(qic, wi, kic):
    rows = qic.shape[0]
    heads = [qic[:, h * IDX_K:(h + 1) * IDX_K] for h in range(IDX_HEADS)]
    if rows % BF16_ROWS == 0:
        logits = _dot_nt(jnp.concatenate(heads, axis=0), kic)
        logits = [logits[h * rows:(h + 1) * rows] for h in range(IDX_HEADS)]
    else:
        logits = [_dot_nt(qh, kic) for qh in heads]
    score = None
    for h in range(IDX_HEADS):
        relu = jnp.maximum(logits[h], 0.0)
        term = wi[:, IDX_DIM + h:IDX_DIM + h + 1] * relu
        score = term if score is None else score + term
    return score * IDX_SCALE


def _dsa_prompt_tile(lk, q0, k_top, q_ref, qic_ref, kiwi_ref, kic_ref, kb_ref, vt_ref, tril_ref,
                     score_scr, bias_scr, s_scr, p_scr, att_scr):
    tq = q_ref.shape[1]
    qic = qic_ref[0]
    q_idx = jnp.concatenate(
        [qic[:, h * IDX_K:(h + 1) * IDX_K] for h in range(IDX_HEADS)], axis=0)
    logits = _dot_nt(kic_ref[0, :lk, :], q_idx)
    wi_t = kiwi_ref[0].T
    score = None
    for h in range(IDX_HEADS):
        term = wi_t[IDX_DIM + h:IDX_DIM + h + 1, :] * jnp.maximum(logits[:, h * tq:(h + 1) * tq], 0.0)
        score = term if score is None else score + term
    score = score * IDX_SCALE
    q_pos = q0 + lax.broadcasted_iota(jnp.int32, (1, tq), 1)
    admissible = lax.broadcasted_iota(jnp.int32, (lk, 1), 0) <= q_pos
    scores = score_scr.at[:lk, :]
    scores[...] = jnp.where(admissible, score, NEG_INF)

    q = q_ref[0]
    for kvh in range(N_KV_HEADS):
        qs = jnp.concatenate(
            [q[:, (kvh * GROUP + g) * HEAD_DIM:(kvh * GROUP + g + 1) * HEAD_DIM]
             for g in range(GROUP)], axis=0)
        kv_sl = slice(kvh * HEAD_DIM, (kvh + 1) * HEAD_DIM)
        s_scr[kvh, :lk, :] = _dot_nt(kb_ref[0, :lk, kv_sl], qs)

    bias_ref = bias_scr.at[:lk, :]
    _topk_bias_keys_major(scores, bias_ref, tril_ref, k_top)

    exp_scale = (HEAD_DIM ** -0.5) * LOG2E
    for kvh in range(N_KV_HEADS):
        kv_sl = slice(kvh * HEAD_DIM, (kvh + 1) * HEAD_DIM)
        denoms = []
        for g in range(GROUP):
            cols = slice(g * tq, (g + 1) * tq)
            sb = s_scr[kvh, :lk, cols] + bias_ref[...]
            m = jnp.max(sb, axis=0, keepdims=True)
            p = jnp.exp2((sb - m) * exp_scale)
            denoms.append(jnp.sum(p, axis=0, keepdims=True))
            p_scr[kvh, :lk, cols] = p.astype(bf16)
        o_t = _dot(vt_ref[kv_sl, :lk], p_scr[kvh, :lk, :])
        o_t = o_t / jnp.concatenate(denoms, axis=1)
        for g in range(GROUP):
            col = (kvh * GROUP + g) * HEAD_DIM
            att_scr[:, col:col + HEAD_DIM] = o_t[:, g * tq:(g + 1) * tq].T.astype(bf16)


def _dsa_prompt_kernel(q_ref, qic_ref, kiwi_ref, kic_ref, kb_ref, vt_ref, x_ref, wo_ref,
                       tril_ref, o_ref, *scratch, k_top, span):
    tq = q_ref.shape[1]
    n = kb_ref.shape[1]
    q0 = pl.program_id(1) * tq
    span_class = (q0 + tq - 1) // span
    for c in range(n // span):
        @pl.when(span_class == c)
        def _(lk=(c + 1) * span):
            _dsa_prompt_tile(lk, q0, k_top, q_ref, qic_ref, kiwi_ref, kic_ref, kb_ref, vt_ref,
                             tril_ref, *scratch)
    att_scr = scratch[-1]
    o_ref[0] = x_ref[0] + _dot(att_scr[...], wo_ref[...])


def _dsa_prompt(q, qic, kiwi, kic, kb, vt, x, wo, tq, span):
    b, t, _ = q.shape
    k_top = min(TOPK_MAX, t // 4)
    a = jnp.arange(LANES)
    tril = (a[None, :] <= a[:, None]).astype(bf16)
    cols = GROUP * tq
    tile = lambda width: pl.BlockSpec((1, tq, width), lambda i, j: (i, j, 0))
    full = lambda width: pl.BlockSpec((1, t, width), lambda i, j: (i, 0, 0))
    return pl.pallas_call(
        functools.partial(_dsa_prompt_kernel, k_top=k_top, span=span),
        grid=(b, t // tq),
        in_specs=[tile(Q_W), tile(IDX_HEADS * IDX_K), tile(LANES), full(IDX_K), full(KV_W),
                  pl.BlockSpec((KV_W, t), lambda i, j: (0, i)), tile(D_MODEL),
                  _const_spec((Q_W, D_MODEL)), _const_spec((LANES, LANES))],
        out_specs=tile(D_MODEL),
        out_shape=jax.ShapeDtypeStruct((b, t, D_MODEL), f32),
        scratch_shapes=[pltpu.VMEM((t, tq), f32),
                        pltpu.VMEM((t, tq), f32),
                        pltpu.VMEM((N_KV_HEADS, t, cols), f32),
                        pltpu.VMEM((N_KV_HEADS, t, cols), bf16),
                        pltpu.VMEM((tq, Q_W), bf16)],
        compiler_params=_params(2),
        name="dsa_prompt",
    )(q, qic, kiwi, kic, kb, vt, x, wo, tril)


def _sample_scores_kernel(pt_ref, qic_ref, kiwi_ref, kicn_ref, cache_ref, o_ref, buf, sem):
    b = pl.program_id(0)
    n_pages = pt_ref.shape[1]
    n_past = n_pages * PAGE_SIZE
    t = qic_ref.shape[0]

    def page_copies(batch, slot):
        return [pltpu.make_async_copy(
            cache_ref.at[pt_ref[batch, p]],
            buf.at[slot, :, pl.ds(p * PAGE_SIZE, PAGE_SIZE)], sem.at[slot])
            for p in range(n_pages)]

    slot = lax.rem(b, 2)

    @pl.when(b == 0)
    def _():
        for copy in page_copies(0, 0):
            copy.start()

    @pl.when(b + 1 < pl.num_programs(0))
    def _():
        for copy in page_copies(b + 1, 1 - slot):
            copy.start()

    for copy in page_copies(b, slot):
        copy.wait()

    keys_t = buf[slot]
    k_hi = keys_t.astype(bf16)
    k_lo = (keys_t - k_hi.astype(f32)).astype(bf16)
    qic = qic_ref[...].astype(f32)

    def stacked(offset):
        return jnp.concatenate(
            [qic[:, h * IDX_K + offset:h * IDX_K + offset + IDX_DIM] for h in range(IDX_HEADS)],
            axis=0).astype(bf16)

    q_hi, q_lo = stacked(0), stacked(2 * IDX_DIM)
    logits = _dot(q_hi, k_hi) + _dot(q_hi, k_lo) + _dot(q_lo, k_hi)
    wi = kiwi_ref[...]
    score = None
    for h in range(IDX_HEADS):
        term = wi[:, IDX_DIM + h:IDX_DIM + h + 1] * jnp.maximum(logits[h * t:(h + 1) * t], 0.0)
        score = term if score is None else score + term
    o_ref[:, :n_past] = score * IDX_SCALE
    new = _indexer_scores(qic_ref[...], wi, kicn_ref[...])
    o_ref[:, n_past:] = jnp.concatenate([new, jnp.zeros((t, LANES - t), f32)], axis=1)


def _sample_scores(page_table, qic, kiwi, kic_new, cache_ki_t):
    nb, n_pages = page_table.shape
    t = qic.shape[0] // nb
    n_past = n_pages * PAGE_SIZE
    row = lambda width: pl.BlockSpec((t, width), lambda b, pt: (b, 0))
    return pl.pallas_call(
        _sample_scores_kernel,
        grid_spec=pltpu.PrefetchScalarGridSpec(
            num_scalar_prefetch=1,
            grid=(nb,),
            in_specs=[row(IDX_HEADS * IDX_K), row(LANES), row(IDX_K),
                      pl.BlockSpec(memory_space=pl.ANY)],
            out_specs=row(n_past + LANES),
            scratch_shapes=[pltpu.VMEM((2, IDX_DIM, n_past), f32),
                            pltpu.SemaphoreType.DMA((2,))],
        ),
        out_shape=jax.ShapeDtypeStruct((nb * t, n_past + LANES), f32),
        compiler_params=_params(1),
        name="dsa_sample_scores",
    )(page_table, qic, kiwi, kic_new, cache_ki_t)


def _sample_select_kernel(sc_ref, tri_ref, before_ref, o_ref, score_scr, *, k_top, t, n_past):
    rows, n = sc_ref.shape
    qi = lax.broadcasted_iota(jnp.int32, (rows, n), 0) & (t - 1)
    kj = lax.broadcasted_iota(jnp.int32, (rows, n), 1)
    admissible = kj <= n_past + qi
    score_scr[...] = jnp.where(admissible, sc_ref[...], NEG_INF)

    def store(c, piece):
        o_ref[:, c * LANES:(c + 1) * LANES] = piece

    _topk_bias(score_scr, tri_ref, before_ref, k_top, store)


def _sample_select(scores, t, n_past, rows_per_step):
    n_rows, n_all = scores.shape
    assert t & (t - 1) == 0 and rows_per_step % t == 0, (t, rows_per_step)
    k_top = min(TOPK_MAX, (n_past + t) // 4)
    tri, before = _tie_consts(n_all)
    block = pl.BlockSpec((rows_per_step, n_all), lambda i: (i, 0))
    return pl.pallas_call(
        functools.partial(_sample_select_kernel, k_top=k_top, t=t, n_past=n_past),
        grid=(n_rows // rows_per_step,),
        in_specs=[block, _const_spec((LANES, LANES)), _const_spec((n_all, LANES))],
        out_specs=block,
        out_shape=jax.ShapeDtypeStruct((n_rows, n_all), f32),
        scratch_shapes=[pltpu.VMEM((rows_per_step, n_all), f32)],
        compiler_params=_params(1),
        name="dsa_sample_select",
    )(scores, tri, before)


def _sample_attend_kernel(pt_ref, bias_ref, biasn_ref, q_ref, kbn_ref, vbn_ref, x_ref, wo_ref,
                          ck_ref, cv_ref, o_ref, kbuf, vbuf, ksem, vsem, m_scr, l_scr, acc_scr,
                          *, pps):
    b, j = pl.program_id(0), pl.program_id(1)
    n_j = pl.num_programs(1)
    step = b * n_j + j
    t = q_ref.shape[0]
    page_rows = PAGE_SIZE * N_KV_HEADS

    def page_copies(batch, chunk, slot):
        copies = []
        for i in range(pps):
            src = pl.ds(pl.multiple_of(pt_ref[batch, chunk * pps + i] * page_rows, page_rows),
                        page_rows)
            dst = pl.ds(i * page_rows, page_rows)
            copies.append(pltpu.make_async_copy(ck_ref.at[src, :], kbuf.at[slot, dst, :], ksem.at[slot]))
            copies.append(pltpu.make_async_copy(cv_ref.at[src, :], vbuf.at[slot, dst, :], vsem.at[slot]))
        return copies

    slot = lax.rem(step, 2)

    @pl.when(step == 0)
    def _():
        for copy in page_copies(0, 0, 0):
            copy.start()

    @pl.when(step + 1 < pl.num_programs(0) * n_j)
    def _():
        nxt = step + 1
        for copy in page_copies(nxt // n_j, lax.rem(nxt, n_j), 1 - slot):
            copy.start()

    @pl.when(j == 0)
    def _():
        m_scr[...] = jnp.full(m_scr.shape, -1e30, f32)
        l_scr[...] = jnp.zeros(l_scr.shape, f32)
        acc_scr[...] = jnp.zeros(acc_scr.shape, f32)

    q = q_ref[...]

    rows = GROUP * t

    def attend(k_heads, v_heads, bias):
        s = jnp.concatenate(
            [_dot_nt(jnp.concatenate(
                [q[:, (kvh * GROUP + g) * HEAD_DIM:(kvh * GROUP + g + 1) * HEAD_DIM]
                 for g in range(GROUP)], axis=0), k_heads[kvh]) for kvh in range(N_KV_HEADS)],
            axis=0)
        s = s * (HEAD_DIM ** -0.5) + jnp.concatenate([bias] * N_HEADS, axis=0)
        m_old = m_scr[...]
        m_new = jnp.maximum(m_old, jnp.max(s, axis=-1, keepdims=True))
        alpha = jnp.exp(m_old - m_new)
        p = jnp.exp(s - m_new)
        l_scr[...] = alpha * l_scr[...] + jnp.sum(p, axis=-1, keepdims=True)
        pv = jnp.concatenate(
            [_dot(p[kvh * rows:(kvh + 1) * rows].astype(bf16), v_heads[kvh])
             for kvh in range(N_KV_HEADS)], axis=0)
        acc_scr[...] = alpha * acc_scr[...] + pv
        m_scr[...] = m_new

    for copy in page_copies(b, j, slot):
        copy.wait()

    def head_rows(buf, kvh):
        return buf[slot, pl.ds(kvh, pps * PAGE_SIZE, stride=N_KV_HEADS), :].astype(bf16)

    attend([head_rows(kbuf, kvh) for kvh in range(N_KV_HEADS)],
           [head_rows(vbuf, kvh) for kvh in range(N_KV_HEADS)], bias_ref[...])

    @pl.when(j == n_j - 1)
    def _():
        pad = jnp.zeros((LANES - t, HEAD_DIM), bf16)

        def new_heads(ref):
            return [jnp.concatenate([ref[:, kvh * HEAD_DIM:(kvh + 1) * HEAD_DIM], pad], axis=0)
                    for kvh in range(N_KV_HEADS)]

        attend(new_heads(kbn_ref), new_heads(vbn_ref), biasn_ref[...])
        o = acc_scr[...] / l_scr[...]
        att = jnp.concatenate([o[h * t:(h + 1) * t] for h in range(N_HEADS)],
                              axis=1).astype(bf16)
        o_ref[...] = x_ref[...] + _dot(att, wo_ref[...])


def _sample_attend(page_table, bias, q, kb_new, vb_new, x, wo, cache_k, cache_v, pages_per_step):
    nb, n_pages = page_table.shape
    t = q.shape[0] // nb
    steps = n_pages // pages_per_step
    span = pages_per_step * PAGE_SIZE
    row = lambda width: pl.BlockSpec((t, width), lambda b, j, pt: (b, 0))
    rows = N_HEADS * t
    buf = pltpu.VMEM((2, span * N_KV_HEADS, HEAD_DIM), f32)
    return pl.pallas_call(
        functools.partial(_sample_attend_kernel, pps=pages_per_step),
        grid_spec=pltpu.PrefetchScalarGridSpec(
            num_scalar_prefetch=1,
            grid=(nb, steps),
            in_specs=[pl.BlockSpec((t, span), lambda b, j, pt: (b, j)),
                      pl.BlockSpec((t, LANES), lambda b, j, pt: (b, n_pages)),
                      row(Q_W), row(KV_W), row(KV_W), row(D_MODEL),
                      pl.BlockSpec((Q_W, D_MODEL), lambda b, j, pt: (0, 0)),
                      pl.BlockSpec(memory_space=pl.ANY), pl.BlockSpec(memory_space=pl.ANY)],
            out_specs=row(D_MODEL),
            scratch_shapes=[buf, buf, pltpu.SemaphoreType.DMA((2,)), pltpu.SemaphoreType.DMA((2,)),
                            pltpu.VMEM((rows, 1), f32),
                            pltpu.VMEM((rows, 1), f32),
                            pltpu.VMEM((rows, HEAD_DIM), f32)],
        ),
        out_shape=jax.ShapeDtypeStruct((nb * t, D_MODEL), f32),
        compiler_params=_params(2),
        name="dsa_sample_attend",
    )(page_table, bias, bias, q, kb_new, vb_new, x, wo, cache_k, cache_v)


def _ffn_kernel(x_ref, g_ref, wgu_ref, wd_ref, o_ref, *, n_ff):
    x = x_ref[...]
    h = _rms(x, g_ref[...]).astype(bf16)
    acc = x
    width = D_FF // n_ff
    for c in range(n_ff):
        gate = _dot(h, wgu_ref[:, c * width:(c + 1) * width])
        up = _dot(h, wgu_ref[:, D_FF + c * width:D_FF + (c + 1) * width])
        acc = acc + _dot((jax.nn.silu(gate) * up).astype(bf16),
                         wd_ref[c * width:(c + 1) * width, :])
    o_ref[...] = acc


def _ffn(x, g, wgu, wd, tm, n_ff):
    t = x.shape[0]
    row = pl.BlockSpec((tm, D_MODEL), lambda i: (i, 0))
    resident = lambda a: pl.BlockSpec(a.shape, lambda i: (0,) * a.ndim,
                                      pipeline_mode=pl.Buffered(1))
    return pl.pallas_call(
        functools.partial(_ffn_kernel, n_ff=n_ff),
        grid=(t // tm,),
        in_specs=[row, _const_spec((1, D_MODEL)), resident(wgu), resident(wd)],
        out_specs=row,
        out_shape=jax.ShapeDtypeStruct((t, D_MODEL), f32),
        compiler_params=_params(1),
        name="ffn",
    )(x, g, wgu, wd)


def _sgu_kernel(x_ref, g_ref, win_ref, gv_ref, ws_ref, bias_ref, wout_ref, *refs,
                chunk_len, emit_v):
    if emit_v:
        o_ref, v_ref, gated_scr = refs
    else:
        o_ref, gated_scr = refs
    tm = x_ref.shape[0]
    n_sub = tm // CHUNK
    x = x_ref[...]
    h = _rms(x, g_ref[...]).astype(bf16)
    z = jax.nn.gelu(_dot(h, win_ref[...]))
    u = z[:, :D_MODEL]
    v = _rms(z[:, D_MODEL:], gv_ref[...])
    if emit_v:
        v_ref[...] = v
    vb = v.astype(bf16)
    ti = lax.broadcasted_iota(jnp.int32, (CHUNK, CHUNK), 0)
    si = lax.broadcasted_iota(jnp.int32, (CHUNK, CHUNK), 1)
    shift = chunk_len.bit_length() - 1
    causal = (si <= ti) & (lax.shift_right_logical(si, shift) == lax.shift_right_logical(ti, shift))
    bias = bias_ref[...]
    for grp in range(SG_GROUPS):
        cols = slice(grp * LANES, (grp + 1) * LANES)
        w = jnp.where(causal, ws_ref[grp], 0.0).astype(bf16)
        vg = jnp.concatenate([vb[c * CHUNK:(c + 1) * CHUNK, cols] for c in range(n_sub)], axis=1)
        mixed = _dot(w, vg) + bias[:, grp:grp + 1]
        for c in range(n_sub):
            rows = slice(c * CHUNK, (c + 1) * CHUNK)
            gated_scr[rows, cols] = (u[rows, cols] * mixed[:, c * LANES:(c + 1) * LANES]).astype(bf16)
    o_ref[...] = x + _dot(gated_scr[...], wout_ref[...])


def _sgu(x, g, w_in, g_v, w_s, bias_t, w_out, tm, chunk_len, emit_v):
    t = x.shape[0]
    row = pl.BlockSpec((tm, D_MODEL), lambda i: (i, 0))
    out_shape = [jax.ShapeDtypeStruct((t, D_MODEL), f32)] * (2 if emit_v else 1)
    return pl.pallas_call(
        functools.partial(_sgu_kernel, chunk_len=chunk_len, emit_v=emit_v),
        grid=(t // tm,),
        in_specs=[row, _const_spec((1, D_MODEL)), _const_spec(w_in.shape),
                  _const_spec((1, D_MODEL)), _const_spec(w_s.shape), _const_spec(bias_t.shape),
                  _const_spec(w_out.shape)],
        out_specs=[row] * len(out_shape),
        out_shape=out_shape,
        scratch_shapes=[pltpu.VMEM((tm, D_MODEL), bf16)],
        compiler_params=_params(1),
        name="sgu",
    )(x, g, w_in, g_v, w_s, bias_t, w_out)


def kernel(x_prompt, x_sample, cache_k, cache_v, cache_kidx, page_table, norm_mix, norm_ffn,
           att_w_in, att_q_norm, att_k_norm, att_w_out, sg_w_in, sg_v_norm, sg_w_s, sg_b_s,
           sg_w_out, ffn_w_gu, ffn_w_down):
    nb_p, t_p, _ = x_prompt.shape
    nb_s, t_s, _ = x_sample.shape
    n_pool = cache_k.shape[1]
    xp = x_prompt.reshape(nb_p * t_p, D_MODEL)
    xs = x_sample.reshape(nb_s * t_s, D_MODEL)
    vec = lambda a: a.reshape(1, -1)

    w_in = jnp.pad(att_w_in[0], ((0, 0), (0, ATT_IN_PAD - ATT_IN))).astype(bf16)
    w_out = att_w_out[0].astype(bf16)
    proj = functools.partial(_att_proj, g=vec(norm_mix[0]), w=w_in,
                             gq=vec(att_q_norm[0]), gk=vec(att_k_norm[0]))
    q_p, k_p, v_p, kb_p, _, qic_p, kic_p, ki_p, kiwi_p, vt_p = proj(xp, tm=512)
    q_s, k_s, v_s, kb_s, vb_s, qic_s, kic_s, ki_s, kiwi_s, _ = proj(xs, tm=nb_s * t_s)

    per_seq = lambda a: a.reshape(nb_p, t_p, a.shape[-1])
    xp = _dsa_prompt(per_seq(q_p), per_seq(qic_p), per_seq(kiwi_p), per_seq(kic_p),
                     per_seq(kb_p), vt_p, per_seq(xp), w_out, tq=128, span=512)
    xp = xp.reshape(nb_p * t_p, D_MODEL)

    scores_s = _sample_scores(page_table, qic_s, kiwi_s, kic_s, jnp.swapaxes(cache_kidx[0], 1, 2))
    bias_s = _sample_select(scores_s, t_s, page_table.shape[1] * PAGE_SIZE, rows_per_step=64)
    kv_rows = n_pool * PAGE_SIZE * N_KV_HEADS
    xs = _sample_attend(page_table, bias_s, q_s, kb_s, vb_s, xs, w_out,
                        cache_k[0].reshape(kv_rows, HEAD_DIM),
                        cache_v[0].reshape(kv_rows, HEAD_DIM), pages_per_step=32)

    def ffn(x, layer, tm):
        return _ffn(x, vec(norm_ffn[layer]), ffn_w_gu[layer].astype(bf16),
                    ffn_w_down[layer].astype(bf16), tm, n_ff=11)

    xp = ffn(xp, 0, 512)
    xs = ffn(xs, 0, nb_s * t_s)

    sg_in = sg_w_in[0].astype(bf16)
    sg_out = sg_w_out[0].astype(bf16)
    sgu = functools.partial(_sgu, g=vec(norm_mix[1]), w_in=sg_in, g_v=vec(sg_v_norm[0]),
                            w_out=sg_out)
    (xp,) = sgu(xp, w_s=sg_w_s[0], bias_t=sg_b_s[0].T, tm=512, chunk_len=CHUNK, emit_v=False)
    reps = CHUNK // t_s
    xs, sgv = sgu(xs, w_s=jnp.tile(sg_w_s[0][:, :t_s, :t_s], (1, reps, reps)),
                  bias_t=jnp.tile(sg_b_s[0][:, :t_s], (1, reps)).T, tm=nb_s * t_s,
                  chunk_len=t_s, emit_v=True)

    xp = ffn(xp, 1, 512)
    xs = ffn(xs, 1, nb_s * t_s)

    y_prompt = xp.reshape(nb_p, t_p, D_MODEL)
    y_sample = xs.reshape(nb_s, t_s, D_MODEL)
    kv_p = lambda a: a.reshape(1, nb_p, t_p, N_KV_HEADS, HEAD_DIM)
    kv_s = lambda a: a.reshape(1, nb_s, t_s, N_KV_HEADS, HEAD_DIM)
    return (y_prompt, y_sample, kv_p(k_p), kv_p(v_p), ki_p.reshape(1, nb_p, t_p, IDX_DIM),
            kv_s(k_s), kv_s(v_s), ki_s.reshape(1, nb_s, t_s, IDX_DIM),
            sgv.reshape(1, nb_s, t_s, D_MODEL))
```

```python
import functools

import jax
import jax.numpy as jnp
from jax import lax
from jax.experimental import pallas as pl
from jax.experimental.pallas import tpu as pltpu

D_MODEL = 1024
HEAD_DIM = 128
N_HEADS = 8
N_KV_HEADS = 2
GROUP = N_HEADS // N_KV_HEADS
IDX_HEADS = 4
IDX_DIM = 64
IDX_SCALE = (IDX_DIM ** -0.5) * (IDX_HEADS ** -0.5)
TOPK_MAX = 256
CHUNK = 128
SG_GROUPS = 8
D_FF = 2816
PAGE_SIZE = 128
EPS = 1e-6

Q_W = N_HEADS * HEAD_DIM
KV_W = N_KV_HEADS * HEAD_DIM
QI_W = IDX_HEADS * IDX_DIM
ATT_IN = Q_W + 2 * KV_W + QI_W + IDX_DIM + IDX_HEADS
LANES = 128
SUBLANES = 8
BF16_ROWS = 16
LOG2E = 1.4426950408889634
SOFTMAX2_SCALE = (HEAD_DIM ** -0.5) * LOG2E
ATT_IN_PAD = -(-ATT_IN // LANES) * LANES
IDX_K = 4 * IDX_DIM
INT_MIN = -(2 ** 31)
NEG_INF = float("-inf")
VMEM_LIMIT = 56 * 1024 * 1024

bf16 = jnp.bfloat16
f32 = jnp.float32


def _rms(x, g):
    return x * lax.rsqrt(jnp.mean(x * x, axis=-1, keepdims=True) + EPS) * g


def _dot(a, b):
    return jnp.dot(a, b, preferred_element_type=f32)


def _dot_nt(a, b):
    return lax.dot_general(a, b, (((1,), (1,)), ((), ())), preferred_element_type=f32)


def _hi(x):
    return x.astype(bf16).astype(f32)


def _params(n_grid):
    return pltpu.CompilerParams(
        dimension_semantics=("arbitrary",) * n_grid, vmem_limit_bytes=VMEM_LIMIT)


def _const_spec(shape):
    nd = len(shape)
    return pl.BlockSpec(shape, lambda *_: (0,) * nd)


def _att_proj_kernel(x_ref, g_ref, w_ref, gq_ref, gk_ref,
                     q_ref, k_ref, v_ref, kb_ref, vb_ref, qic_ref, kic_ref, ki_ref, kiwi_ref, vt_ref):
    tm = x_ref.shape[0]
    h = _rms(x_ref[...], g_ref[...]).astype(bf16)
    z = _dot(h, w_ref[...])
    gq = gq_ref[...]
    gk = gk_ref[...]
    for j in range(N_HEADS):
        sl = slice(j * HEAD_DIM, (j + 1) * HEAD_DIM)
        q_ref[:, sl] = (_rms(z[:, sl], gq) * SOFTMAX2_SCALE).astype(bf16)
    v = z[:, Q_W + KV_W:Q_W + 2 * KV_W]
    for j in range(N_KV_HEADS):
        sl = slice(j * HEAD_DIM, (j + 1) * HEAD_DIM)
        kj = _rms(z[:, Q_W + j * HEAD_DIM:Q_W + (j + 1) * HEAD_DIM], gk)
        kb_ref[:, sl] = kj.astype(bf16)
        head_rows = pl.ds(j, tm, stride=N_KV_HEADS)
        k_ref[head_rows, :] = kj
        v_ref[head_rows, :] = v[:, sl]
    vb_ref[...] = v.astype(bf16)
    vt_ref[...] = v.T.astype(bf16)

    low_half = lax.broadcasted_iota(jnp.int32, (tm, LANES), 1) < IDX_DIM
    qi0 = Q_W + 2 * KV_W
    for p in range(IDX_HEADS // 2):
        pair = z[:, qi0 + p * LANES:qi0 + (p + 1) * LANES]
        rolled = pltpu.roll(pair, IDX_DIM, axis=1)
        for e, (own, other) in enumerate(((pair, rolled), (rolled, pair))):
            dup = jnp.where(low_half, own, other)
            single = jnp.where(low_half, own, 0.0)
            base = (2 * p + e) * IDX_K
            qic_ref[:, base:base + LANES] = dup.astype(bf16)
            qic_ref[:, base + LANES:base + 2 * LANES] = (single - _hi(single)).astype(bf16)
    last = z[:, qi0 + QI_W:qi0 + QI_W + LANES]
    rolled = pltpu.roll(last, IDX_DIM, axis=1)
    kic_ref[:, :LANES] = jnp.where(low_half, _hi(last), rolled - _hi(rolled)).astype(bf16)
    kic_ref[:, LANES:] = jnp.where(low_half, _hi(last), 0.0).astype(bf16)
    ki_ref[...] = last[:, :IDX_DIM]
    kiwi_ref[...] = last


def _att_proj(x, g, w, gq, gk, tm):
    t = x.shape[0]
    row = lambda width: pl.BlockSpec((tm, width), lambda i: (i, 0))
    head_rows = pl.BlockSpec((N_KV_HEADS * tm, HEAD_DIM), lambda i: (i, 0))
    head_rows_shape = jax.ShapeDtypeStruct((N_KV_HEADS * t, HEAD_DIM), f32)
    outs = [(Q_W, bf16), None, None, (KV_W, bf16), (KV_W, bf16),
            (IDX_HEADS * IDX_K, bf16), (IDX_K, bf16), (IDX_DIM, f32), (LANES, f32)]
    return pl.pallas_call(
        _att_proj_kernel,
        grid=(t // tm,),
        in_specs=[row(D_MODEL), _const_spec((1, D_MODEL)), _const_spec((D_MODEL, ATT_IN_PAD)),
                  _const_spec((1, HEAD_DIM)), _const_spec((1, HEAD_DIM))],
        out_specs=[row(o[0]) if o else head_rows for o in outs]
        + [pl.BlockSpec((KV_W, tm), lambda i: (0, i))],
        out_shape=[jax.ShapeDtypeStruct((t, o[0]), o[1]) if o else head_rows_shape for o in outs]
        + [jax.ShapeDtypeStruct((KV_W, t), bf16)],
        compiler_params=_params(1),
        name="att_proj",
    )(x, g, w, gq, gk)


def _key_to_float(key):
    bits = jnp.where(key < 0, key ^ jnp.int32(0x7FFFFFFF), key)
    return lax.bitcast_convert_type(bits, f32)


def _count(mask, axis=1):
    ones = jnp.where(mask, 1.0, 0.0)
    if axis == 0:
        n, cols = ones.shape
        rows = 4 * SUBLANES
        ones = jnp.sum(ones.reshape(n // rows, rows, cols), axis=0)
    return jnp.sum(ones, axis=axis, keepdims=True)


def _rank_threshold(score_ref, k_top, axis):
    shape = tuple(1 if a == axis else d for a, d in enumerate(score_ref.shape))

    def step(i, t):
        cand = t + lax.shift_left(jnp.int32(1), jnp.int32(31) - i)
        enough = _count(score_ref[...] >= _key_to_float(cand), axis) >= k_top
        return jnp.where(enough, cand, t)

    t = lax.fori_loop(0, 32, step, jnp.full(shape, INT_MIN, jnp.int32))
    return _key_to_float(t), t == INT_MIN


def _topk_bias_keys_major(score_ref, bias_ref, tril_ref, k_top):
    n, _ = score_ref.shape
    t, take_all = _rank_threshold(score_ref, k_top, axis=0)

    def above(s):
        return (s > t) | (take_all & (s > NEG_INF))

    need = k_top - _count(above(score_ref[...]), axis=0)
    ties_before = jnp.zeros(t.shape, f32)
    for c in range(n // LANES):
        sl = slice(c * LANES, (c + 1) * LANES)
        s_c = score_ref[sl, :]
        tie = s_c == t
        tie_f = jnp.where(tie, 1.0, 0.0)
        rank = _dot(tril_ref[...], tie_f.astype(bf16)) + ties_before
        ties_before = ties_before + jnp.sum(tie_f, axis=0, keepdims=True)
        bias_ref[sl, :] = jnp.where(
            above(s_c) | (tie & (rank <= need)), 0.0, NEG_INF).astype(bias_ref.dtype)


def _topk_bias(score_ref, tri_ref, before_ref, k_top, store):
    _, n = score_ref.shape
    t, take_all = _rank_threshold(score_ref, k_top, axis=1)
    def above(s):
        return (s > t) | (take_all & (s > NEG_INF))

    s = score_ref[...]
    need = k_top - _count(above(s))
    tie_b = jnp.where(s == t, 1.0, 0.0).astype(bf16)
    before = _dot(tie_b, before_ref[:n, :])
    for c in range(n // LANES):
        sl = slice(c * LANES, (c + 1) * LANES)
        s_c = s[:, sl]
        rank = _dot(tie_b[:, sl], tri_ref[...]) + before[:, c:c + 1]
        store(c, jnp.where(above(s_c) | ((s_c == t) & (rank <= need)), 0.0, NEG_INF))


def _tie_consts(n):
    a = jnp.arange(LANES)
    tri = (a[:, None] <= a[None, :]).astype(bf16)
    before = ((jnp.arange(n)[:, None] // LANES) < a[None, :]).astype(bf16)
    return tri, before


def _indexer_scores(qic, wi, kic):
    rows = qic.shape[0]
    heads = [qic[:, h * IDX_K:(h + 1) * IDX_K] for h in range(IDX_HEADS)]
    if rows % BF16_ROWS == 0:
        logits = _dot_nt(jnp.concatenate(heads, axis=0), kic)
        logits = [logits[h * rows:(h + 1) * rows] for h in range(IDX_HEADS)]
    else:
        logits = [_dot_nt(qh, kic) for qh in heads]
    score = None
    for h in range(IDX_HEADS):
        relu = jnp.maximum(logits[h], 0.0)
        term = wi[:, IDX_DIM + h:IDX_DIM + h + 1] * relu
        score = term if score is None else score + term
    return score * IDX_SCALE


def _dsa_prompt_tile(lk, q0, k_top, q_ref, qic_ref, kiwi_ref, kic_ref, kb_ref, vt_ref, tril_ref,
                     score_scr, bias_scr, s_scr, p_scr, att_scr):
    tq = q_ref.shape[1]
    qic = qic_ref[0]
    q_idx = jnp.concatenate(
        [qic[:, h * IDX_K:(h + 1) * IDX_K] for h in range(IDX_HEADS)], axis=0)
    logits = _dot_nt(kic_ref[0, :lk, :], q_idx)
    wi_t = kiwi_ref[0].T
    score = None
    for h in range(IDX_HEADS):
        term = wi_t[IDX_DIM + h:IDX_DIM + h + 1, :] * jnp.maximum(logits[:, h * tq:(h + 1) * tq], 0.0)
        score = term if score is None else score + term
    score = score * IDX_SCALE
    q_pos = q0 + lax.broadcasted_iota(jnp.int32, (1, tq), 1)
    admissible = lax.broadcasted_iota(jnp.int32, (lk, 1), 0) <= q_pos
    scores = score_scr.at[:lk, :]
    scores[...] = jnp.where(admissible, score, NEG_INF)

    q = q_ref[0]
    for kvh in range(N_KV_HEADS):
        qs = jnp.concatenate(
            [q[:, (kvh * GROUP + g) * HEAD_DIM:(kvh * GROUP + g + 1) * HEAD_DIM]
             for g in range(GROUP)], axis=0)
        kv_sl = slice(kvh * HEAD_DIM, (kvh + 1) * HEAD_DIM)
        s_scr[kvh, :lk, :] = _dot_nt(kb_ref[0, :lk, kv_sl], qs).astype(bf16)

    bias_ref = bias_scr.at[:lk, :]
    _topk_bias_keys_major(scores, bias_ref, tril_ref, k_top)

    ones = jnp.ones((BF16_ROWS, lk), bf16)
    for kvh in range(N_KV_HEADS):
        kv_sl = slice(kvh * HEAD_DIM, (kvh + 1) * HEAD_DIM)
        for g in range(GROUP):
            cols = slice(g * tq, (g + 1) * tq)
            sb = s_scr[kvh, :lk, cols] + bias_ref[...]
            p_scr[kvh, :lk, cols] = jnp.exp2(sb - jnp.max(sb, axis=0, keepdims=True))
        values = jnp.concatenate([vt_ref[kv_sl, :lk], ones], axis=0)
        o_t = _dot(values, p_scr[kvh, :lk, :])
        o_t = o_t[:HEAD_DIM] / o_t[HEAD_DIM:HEAD_DIM + 1]
        for g in range(GROUP):
            col = (kvh * GROUP + g) * HEAD_DIM
            att_scr[:, col:col + HEAD_DIM] = o_t[:, g * tq:(g + 1) * tq].T.astype(bf16)


def _dsa_prompt_kernel(q_ref, qic_ref, kiwi_ref, kic_ref, kb_ref, vt_ref, x_ref, wo_ref,
                       tril_ref, o_ref, *scratch, k_top, span):
    tq = q_ref.shape[1]
    n = kb_ref.shape[1]
    q0 = pl.program_id(1) * tq
    span_class = (q0 + tq - 1) // span
    for c in range(n // span):
        @pl.when(span_class == c)
        def _(lk=(c + 1) * span):
            _dsa_prompt_tile(lk, q0, k_top, q_ref, qic_ref, kiwi_ref, kic_ref, kb_ref, vt_ref,
                             tril_ref, *scratch)
    att_scr = scratch[-1]
    o_ref[0] = x_ref[0] + _dot(att_scr[...], wo_ref[...])


def _dsa_prompt(q, qic, kiwi, kic, kb, vt, x, wo, tq, span):
    b, t, _ = q.shape
    k_top = min(TOPK_MAX, t // 4)
    a = jnp.arange(LANES)
    tril = (a[None, :] <= a[:, None]).astype(bf16)
    cols = GROUP * tq
    tile = lambda width: pl.BlockSpec((1, tq, width), lambda i, j: (i, j, 0))
    full = lambda width: pl.BlockSpec((1, t, width), lambda i, j: (i, 0, 0))
    return pl.pallas_call(
        functools.partial(_dsa_prompt_kernel, k_top=k_top, span=span),
        grid=(b, t // tq),
        in_specs=[tile(Q_W), tile(IDX_HEADS * IDX_K), tile(LANES), full(IDX_K), full(KV_W),
                  pl.BlockSpec((KV_W, t), lambda i, j: (0, i)), tile(D_MODEL),
                  _const_spec((Q_W, D_MODEL)), _const_spec((LANES, LANES))],
        out_specs=tile(D_MODEL),
        out_shape=jax.ShapeDtypeStruct((b, t, D_MODEL), f32),
        scratch_shapes=[pltpu.VMEM((t, tq), f32),
                        pltpu.VMEM((t, tq), bf16),
                        pltpu.VMEM((N_KV_HEADS, t, cols), bf16),
                        pltpu.VMEM((N_KV_HEADS, t, cols), bf16),
                        pltpu.VMEM((tq, Q_W), bf16)],
        compiler_params=_params(2),
        name="dsa_prompt",
    )(q, qic, kiwi, kic, kb, vt, x, wo, tril)


def _sample_scores_kernel(pt_ref, qic_ref, kiwi_ref, kicn_ref, cache_ref, o_ref, buf, sem):
    b = pl.program_id(0)
    n_pages = pt_ref.shape[1]
    n_past = n_pages * PAGE_SIZE
    t = qic_ref.shape[0]

    def page_copies(batch, slot):
        return [pltpu.make_async_copy(
            cache_ref.at[pt_ref[batch, p]],
            buf.at[slot, :, pl.ds(p * PAGE_SIZE, PAGE_SIZE)], sem.at[slot])
            for p in range(n_pages)]

    slot = lax.rem(b, 2)

    @pl.when(b == 0)
    def _():
        for copy in page_copies(0, 0):
            copy.start()

    @pl.when(b + 1 < pl.num_programs(0))
    def _():
        for copy in page_copies(b + 1, 1 - slot):
            copy.start()

    for copy in page_copies(b, slot):
        copy.wait()

    keys_t = buf[slot]
    k_hi = keys_t.astype(bf16)
    k_lo = (keys_t - k_hi.astype(f32)).astype(bf16)
    qic = qic_ref[...].astype(f32)

    def stacked(offset):
        return jnp.concatenate(
            [qic[:, h * IDX_K + offset:h * IDX_K + offset + IDX_DIM] for h in range(IDX_HEADS)],
            axis=0).astype(bf16)

    q_hi, q_lo = stacked(0), stacked(2 * IDX_DIM)
    logits = _dot(q_hi, k_hi) + _dot(q_hi, k_lo) + _dot(q_lo, k_hi)
    wi = kiwi_ref[...]
    score = None
    for h in range(IDX_HEADS):
        term = wi[:, IDX_DIM + h:IDX_DIM + h + 1] * jnp.maximum(logits[h * t:(h + 1) * t], 0.0)
        score = term if score is None else score + term
    o_ref[:, :n_past] = score * IDX_SCALE
    new = _indexer_scores(qic_ref[...], wi, kicn_ref[...])
    o_ref[:, n_past:] = jnp.concatenate([new, jnp.zeros((t, LANES - t), f32)], axis=1)


def _sample_scores(page_table, qic, kiwi, kic_new, cache_ki_t):
    nb, n_pages = page_table.shape
    t = qic.shape[0] // nb
    n_past = n_pages * PAGE_SIZE
    row = lambda width: pl.BlockSpec((t, width), lambda b, pt: (b, 0))
    return pl.pallas_call(
        _sample_scores_kernel,
        grid_spec=pltpu.PrefetchScalarGridSpec(
            num_scalar_prefetch=1,
            grid=(nb,),
            in_specs=[row(IDX_HEADS * IDX_K), row(LANES), row(IDX_K),
                      pl.BlockSpec(memory_space=pl.ANY)],
            out_specs=row(n_past + LANES),
            scratch_shapes=[pltpu.VMEM((2, IDX_DIM, n_past), f32),
                            pltpu.SemaphoreType.DMA((2,))],
        ),
        out_shape=jax.ShapeDtypeStruct((nb * t, n_past + LANES), f32),
        compiler_params=_params(1),
        name="dsa_sample_scores",
    )(page_table, qic, kiwi, kic_new, cache_ki_t)


def _sample_select_kernel(sc_ref, tri_ref, before_ref, o_ref, score_scr, *, k_top, t, n_past):
    rows, n = sc_ref.shape
    qi = lax.broadcasted_iota(jnp.int32, (rows, n), 0) & (t - 1)
    kj = lax.broadcasted_iota(jnp.int32, (rows, n), 1)
    admissible = kj <= n_past + qi
    score_scr[...] = jnp.where(admissible, sc_ref[...], NEG_INF)

    def store(c, piece):
        o_ref[:, c * LANES:(c + 1) * LANES] = piece

    _topk_bias(score_scr, tri_ref, before_ref, k_top, store)


def _sample_select(scores, t, n_past, rows_per_step):
    n_rows, n_all = scores.shape
    assert t & (t - 1) == 0 and rows_per_step % t == 0, (t, rows_per_step)
    k_top = min(TOPK_MAX, (n_past + t) // 4)
    tri, before = _tie_consts(n_all)
    block = pl.BlockSpec((rows_per_step, n_all), lambda i: (i, 0))
    return pl.pallas_call(
        functools.partial(_sample_select_kernel, k_top=k_top, t=t, n_past=n_past),
        grid=(n_rows // rows_per_step,),
        in_specs=[block, _const_spec((LANES, LANES)), _const_spec((n_all, LANES))],
        out_specs=block,
        out_shape=jax.ShapeDtypeStruct((n_rows, n_all), f32),
        scratch_shapes=[pltpu.VMEM((rows_per_step, n_all), f32)],
        compiler_params=_params(1),
        name="dsa_sample_select",
    )(scores, tri, before)


def _sample_attend_kernel(pt_ref, bias_ref, biasn_ref, q_ref, kbn_ref, vbn_ref, x_ref, wo_ref,
                          ck_ref, cv_ref, o_ref, kbuf, vbuf, ksem, vsem, m_scr, l_scr, acc_scr,
                          *, pps):
    b, j = pl.program_id(0), pl.program_id(1)
    n_j = pl.num_programs(1)
    step = b * n_j + j
    t = q_ref.shape[0]
    page_rows = PAGE_SIZE * N_KV_HEADS

    def page_copies(batch, chunk, slot):
        copies = []
        for i in range(pps):
            src = pl.ds(pl.multiple_of(pt_ref[batch, chunk * pps + i] * page_rows, page_rows),
                        page_rows)
            dst = pl.ds(i * page_rows, page_rows)
            copies.append(pltpu.make_async_copy(ck_ref.at[src, :], kbuf.at[slot, dst, :], ksem.at[slot]))
            copies.append(pltpu.make_async_copy(cv_ref.at[src, :], vbuf.at[slot, dst, :], vsem.at[slot]))
        return copies

    slot = lax.rem(step, 2)

    @pl.when(step == 0)
    def _():
        for copy in page_copies(0, 0, 0):
            copy.start()

    @pl.when(step + 1 < pl.num_programs(0) * n_j)
    def _():
        nxt = step + 1
        for copy in page_copies(nxt // n_j, lax.rem(nxt, n_j), 1 - slot):
            copy.start()

    @pl.when(j == 0)
    def _():
        m_scr[...] = jnp.full(m_scr.shape, -1e30, f32)
        l_scr[...] = jnp.zeros(l_scr.shape, f32)
        acc_scr[...] = jnp.zeros(acc_scr.shape, f32)

    q = q_ref[...]

    rows = GROUP * t

    def attend(k_heads, v_heads, bias):
        s = jnp.concatenate(
            [_dot_nt(jnp.concatenate(
                [q[:, (kvh * GROUP + g) * HEAD_DIM:(kvh * GROUP + g + 1) * HEAD_DIM]
                 for g in range(GROUP)], axis=0), k_heads[kvh]) for kvh in range(N_KV_HEADS)],
            axis=0)
        s = s + jnp.concatenate([bias] * N_HEADS, axis=0)
        m_old = m_scr[...]
        m_new = jnp.maximum(m_old, jnp.max(s, axis=-1, keepdims=True))
        alpha = jnp.exp2(m_old - m_new)
        p = jnp.exp2(s - m_new)
        l_scr[...] = alpha * l_scr[...] + jnp.sum(p, axis=-1, keepdims=True)
        pv = jnp.concatenate(
            [_dot(p[kvh * rows:(kvh + 1) * rows].astype(bf16), v_heads[kvh])
             for kvh in range(N_KV_HEADS)], axis=0)
        acc_scr[...] = alpha * acc_scr[...] + pv
        m_scr[...] = m_new

    for copy in page_copies(b, j, slot):
        copy.wait()

    def head_rows(buf, kvh):
        return buf[slot, pl.ds(kvh, pps * PAGE_SIZE, stride=N_KV_HEADS), :].astype(bf16)

    attend([head_rows(kbuf, kvh) for kvh in range(N_KV_HEADS)],
           [head_rows(vbuf, kvh) for kvh in range(N_KV_HEADS)], bias_ref[...])

    @pl.when(j == n_j - 1)
    def _():
        pad = jnp.zeros((LANES - t, HEAD_DIM), bf16)

        def new_heads(ref):
            return [jnp.concatenate([ref[:, kvh * HEAD_DIM:(kvh + 1) * HEAD_DIM], pad], axis=0)
                    for kvh in range(N_KV_HEADS)]

        attend(new_heads(kbn_ref), new_heads(vbn_ref), biasn_ref[...])
        o = acc_scr[...] / l_scr[...]
        att = jnp.concatenate([o[h * t:(h + 1) * t] for h in range(N_HEADS)],
                              axis=1).astype(bf16)
        o_ref[...] = x_ref[...] + _dot(att, wo_ref[...])


def _sample_attend(page_table, bias, q, kb_new, vb_new, x, wo, cache_k, cache_v, pages_per_step):
    nb, n_pages = page_table.shape
    t = q.shape[0] // nb
    steps = n_pages // pages_per_step
    span = pages_per_step * PAGE_SIZE
    row = lambda width: pl.BlockSpec((t, width), lambda b, j, pt: (b, 0))
    rows = N_HEADS * t
    buf = pltpu.VMEM((2, span * N_KV_HEADS, HEAD_DIM), f32)
    return pl.pallas_call(
        functools.partial(_sample_attend_kernel, pps=pages_per_step),
        grid_spec=pltpu.PrefetchScalarGridSpec(
            num_scalar_prefetch=1,
            grid=(nb, steps),
            in_specs=[pl.BlockSpec((t, span), lambda b, j, pt: (b, j)),
                      pl.BlockSpec((t, LANES), lambda b, j, pt: (b, n_pages)),
                      row(Q_W), row(KV_W), row(KV_W), row(D_MODEL),
                      pl.BlockSpec((Q_W, D_MODEL), lambda b, j, pt: (0, 0)),
                      pl.BlockSpec(memory_space=pl.ANY), pl.BlockSpec(memory_space=pl.ANY)],
            out_specs=row(D_MODEL),
            scratch_shapes=[buf, buf, pltpu.SemaphoreType.DMA((2,)), pltpu.SemaphoreType.DMA((2,)),
                            pltpu.VMEM((rows, 1), f32),
                            pltpu.VMEM((rows, 1), f32),
                            pltpu.VMEM((rows, HEAD_DIM), f32)],
        ),
        out_shape=jax.ShapeDtypeStruct((nb * t, D_MODEL), f32),
        compiler_params=_params(2),
        name="dsa_sample_attend",
    )(page_table, bias, bias, q, kb_new, vb_new, x, wo, cache_k, cache_v)


def _ffn_kernel(x_ref, g_ref, wgu_ref, wd_ref, o_ref, *, n_ff):
    x = x_ref[...]
    h = _rms(x, g_ref[...]).astype(bf16)
    acc = x
    width = D_FF // n_ff
    for c in range(n_ff):
        gate = _dot(h, wgu_ref[:, c * width:(c + 1) * width])
        up = _dot(h, wgu_ref[:, D_FF + c * width:D_FF + (c + 1) * width])
        acc = acc + _dot((jax.nn.silu(gate) * up).astype(bf16),
                         wd_ref[c * width:(c + 1) * width, :])
    o_ref[...] = acc


def _ffn(x, g, wgu, wd, tm, n_ff):
    t = x.shape[0]
    row = pl.BlockSpec((tm, D_MODEL), lambda i: (i, 0))
    resident = lambda a: pl.BlockSpec(a.shape, lambda i: (0,) * a.ndim,
                                      pipeline_mode=pl.Buffered(1))
    return pl.pallas_call(
        functools.partial(_ffn_kernel, n_ff=n_ff),
        grid=(t // tm,),
        in_specs=[row, _const_spec((1, D_MODEL)), resident(wgu), resident(wd)],
        out_specs=row,
        out_shape=jax.ShapeDtypeStruct((t, D_MODEL), f32),
        compiler_params=_params(1),
        name="ffn",
    )(x, g, wgu, wd)


def _sgu_kernel(x_ref, g_ref, win_ref, gv_ref, ws_ref, bias_ref, wout_ref, *refs,
                chunk_len, emit_v):
    if emit_v:
        o_ref, v_ref, gated_scr = refs
    else:
        o_ref, gated_scr = refs
    tm = x_ref.shape[0]
    n_sub = tm // CHUNK
    x = x_ref[...]
    h = _rms(x, g_ref[...]).astype(bf16)
    z = jax.nn.gelu(_dot(h, win_ref[...]))
    u = z[:, :D_MODEL]
    v = _rms(z[:, D_MODEL:], gv_ref[...])
    if emit_v:
        v_ref[...] = v
    vb = v.astype(bf16)
    ti = lax.broadcasted_iota(jnp.int32, (CHUNK, CHUNK), 0)
    si = lax.broadcasted_iota(jnp.int32, (CHUNK, CHUNK), 1)
    shift = chunk_len.bit_length() - 1
    causal = (si <= ti) & (lax.shift_right_logical(si, shift) == lax.shift_right_logical(ti, shift))
    bias = bias_ref[...]
    for grp in range(SG_GROUPS):
        cols = slice(grp * LANES, (grp + 1) * LANES)
        w = jnp.where(causal, ws_ref[grp], 0.0).astype(bf16)
        vg = jnp.concatenate([vb[c * CHUNK:(c + 1) * CHUNK, cols] for c in range(n_sub)], axis=1)
        mixed = _dot(w, vg) + bias[:, grp:grp + 1]
        for c in range(n_sub):
            rows = slice(c * CHUNK, (c + 1) * CHUNK)
            gated_scr[rows, cols] = (u[rows, cols] * mixed[:, c * LANES:(c + 1) * LANES]).astype(bf16)
    o_ref[...] = x + _dot(gated_scr[...], wout_ref[...])


def _sgu(x, g, w_in, g_v, w_s, bias_t, w_out, tm, chunk_len, emit_v):
    t = x.shape[0]
    row = pl.BlockSpec((tm, D_MODEL), lambda i: (i, 0))
    out_shape = [jax.ShapeDtypeStruct((t, D_MODEL), f32)] * (2 if emit_v else 1)
    return pl.pallas_call(
        functools.partial(_sgu_kernel, chunk_len=chunk_len, emit_v=emit_v),
        grid=(t // tm,),
        in_specs=[row, _const_spec((1, D_MODEL)), _const_spec(w_in.shape),
                  _const_spec((1, D_MODEL)), _const_spec(w_s.shape), _const_spec(bias_t.shape),
                  _const_spec(w_out.shape)],
        out_specs=[row] * len(out_shape),
        out_shape=out_shape,
        scratch_shapes=[pltpu.VMEM((tm, D_MODEL), bf16)],
        compiler_params=_params(1),
        name="sgu",
    )(x, g, w_in, g_v, w_s, bias_t, w_out)


def kernel(x_prompt, x_sample, cache_k, cache_v, cache_kidx, page_table, norm_mix, norm_ffn,
           att_w_in, att_q_norm, att_k_norm, att_w_out, sg_w_in, sg_v_norm, sg_w_s, sg_b_s,
           sg_w_out, ffn_w_gu, ffn_w_down):
    nb_p, t_p, _ = x_prompt.shape
    nb_s, t_s, _ = x_sample.shape
    n_pool = cache_k.shape[1]
    xp = x_prompt.reshape(nb_p * t_p, D_MODEL)
    xs = x_sample.reshape(nb_s * t_s, D_MODEL)
    vec = lambda a: a.reshape(1, -1)

    w_in = jnp.pad(att_w_in[0], ((0, 0), (0, ATT_IN_PAD - ATT_IN))).astype(bf16)
    w_out = att_w_out[0].astype(bf16)
    proj = functools.partial(_att_proj, g=vec(norm_mix[0]), w=w_in,
                             gq=vec(att_q_norm[0]), gk=vec(att_k_norm[0]))
    q_p, k_p, v_p, kb_p, _, qic_p, kic_p, ki_p, kiwi_p, vt_p = proj(xp, tm=512)
    q_s, k_s, v_s, kb_s, vb_s, qic_s, kic_s, ki_s, kiwi_s, _ = proj(xs, tm=nb_s * t_s)

    per_seq = lambda a: a.reshape(nb_p, t_p, a.shape[-1])
    xp = _dsa_prompt(per_seq(q_p), per_seq(qic_p), per_seq(kiwi_p), per_seq(kic_p),
                     per_seq(kb_p), vt_p, per_seq(xp), w_out, tq=128, span=512)
    xp = xp.reshape(nb_p * t_p, D_MODEL)

    scores_s = _sample_scores(page_table, qic_s, kiwi_s, kic_s, jnp.swapaxes(cache_kidx[0], 1, 2))
    bias_s = _sample_select(scores_s, t_s, page_table.shape[1] * PAGE_SIZE, rows_per_step=64)
    kv_rows = n_pool * PAGE_SIZE * N_KV_HEADS
    xs = _sample_attend(page_table, bias_s, q_s, kb_s, vb_s, xs, w_out,
                        cache_k[0].reshape(kv_rows, HEAD_DIM),
                        cache_v[0].reshape(kv_rows, HEAD_DIM), pages_per_step=32)

    def ffn(x, layer, tm):
        return _ffn(x, vec(norm_ffn[layer]), ffn_w_gu[layer].astype(bf16),
                    ffn_w_down[layer].astype(bf16), tm, n_ff=11)

    xp = ffn(xp, 0, 512)
    xs = ffn(xs, 0, nb_s * t_s)

    sg_in = sg_w_in[0].astype(bf16)
    sg_out = sg_w_out[0].astype(bf16)
    sgu = functools.partial(_sgu, g=vec(norm_mix[1]), w_in=sg_in, g_v=vec(sg_v_norm[0]),
                            w_out=sg_out)
    (xp,) = sgu(xp, w_s=sg_w_s[0], bias_t=sg_b_s[0].T, tm=512, chunk_len=CHUNK, emit_v=False)
    reps = CHUNK // t_s
    xs, sgv = sgu(xs, w_s=jnp.tile(sg_w_s[0][:, :t_s, :t_s], (1, reps, reps)),
                  bias_t=jnp.tile(sg_b_s[0][:, :t_s], (1, reps)).T, tm=nb_s * t_s,
                  chunk_len=t_s, emit_v=True)

    xp = ffn(xp, 1, 512)
    xs = ffn(xs, 1, nb_s * t_s)

    y_prompt = xp.reshape(nb_p, t_p, D_MODEL)
    y_sample = xs.reshape(nb_s, t_s, D_MODEL)
    kv_p = lambda a: a.reshape(1, nb_p, t_p, N_KV_HEADS, HEAD_DIM)
    kv_s = lambda a: a.reshape(1, nb_s, t_s, N_KV_HEADS, HEAD_DIM)
    return (y_prompt, y_sample, kv_p(k_p), kv_p(v_p), ki_p.reshape(1, nb_p, t_p, IDX_DIM),
            kv_s(k_s), kv_s(v_s), ki_s.reshape(1, nb_s, t_s, IDX_DIM),
            sgv.reshape(1, nb_s, t_s, D_MODEL))
```

```python
import functools

import jax
import jax.numpy as jnp
from jax import lax
from jax.experimental import pallas as pl
from jax.experimental.pallas import tpu as pltpu

D_MODEL = 1024
HEAD_DIM = 128
N_HEADS = 8
N_KV_HEADS = 2
GROUP = N_HEADS // N_KV_HEADS
IDX_HEADS = 4
IDX_DIM = 64
IDX_SCALE = (IDX_DIM ** -0.5) * (IDX_HEADS ** -0.5)
TOPK_MAX = 256
CHUNK = 128
SG_GROUPS = 8
D_FF = 2816
PAGE_SIZE = 128
EPS = 1e-6

Q_W = N_HEADS * HEAD_DIM
KV_W = N_KV_HEADS * HEAD_DIM
QI_W = IDX_HEADS * IDX_DIM
ATT_IN = Q_W + 2 * KV_W + QI_W + IDX_DIM + IDX_HEADS
LANES = 128
SUBLANES = 8
BF16_ROWS = 16
LOG2E = 1.4426950408889634
ATT_IN_PAD = -(-ATT_IN // LANES) * LANES
IDX_K = 4 * IDX_DIM
INT_MIN = -(2 ** 31)
NEG_INF = float("-inf")
VMEM_LIMIT = 56 * 1024 * 1024

bf16 = jnp.bfloat16
f32 = jnp.float32


def _rms(x, g):
    return x * lax.rsqrt(jnp.mean(x * x, axis=-1, keepdims=True) + EPS) * g


def _dot(a, b):
    return jnp.dot(a, b, preferred_element_type=f32)


def _dot_nt(a, b):
    return lax.dot_general(a, b, (((1,), (1,)), ((), ())), preferred_element_type=f32)


def _hi(x):
    return x.astype(bf16).astype(f32)


def _params(n_grid):
    return pltpu.CompilerParams(
        dimension_semantics=("arbitrary",) * n_grid, vmem_limit_bytes=VMEM_LIMIT)


def _const_spec(shape):
    nd = len(shape)
    return pl.BlockSpec(shape, lambda *_: (0,) * nd)


def _weight_spec(shape):
    nd = len(shape)
    return pl.BlockSpec(shape, lambda *_: (0,) * nd, pipeline_mode=pl.Buffered(1))


def _cast_weights_once(is_first_step, pairs):
    @pl.when(is_first_step)
    def _():
        for src, dst in pairs:
            dst[...] = src[...].astype(dst.dtype)


def _att_proj_kernel(x_ref, g_ref, w_ref, gq_ref, gk_ref,
                     q_ref, k_ref, v_ref, kb_ref, vb_ref, qic_ref, kic_ref, ki_ref, kiwi_ref, vt_ref,
                     w_b):
    @pl.when(pl.program_id(0) == 0)
    def _():
        w_b[:, ATT_IN_PAD - LANES:] = jnp.zeros((D_MODEL, LANES), bf16)
        w_b[:, :ATT_IN] = w_ref[...].astype(bf16)

    tm = x_ref.shape[0]
    h = _rms(x_ref[...], g_ref[...]).astype(bf16)
    z = _dot(h, w_b[...])
    gq = gq_ref[...]
    gk = gk_ref[...]
    for j in range(N_HEADS):
        sl = slice(j * HEAD_DIM, (j + 1) * HEAD_DIM)
        q_ref[:, sl] = _rms(z[:, sl], gq).astype(bf16)
    v = z[:, Q_W + KV_W:Q_W + 2 * KV_W]
    for j in range(N_KV_HEADS):
        sl = slice(j * HEAD_DIM, (j + 1) * HEAD_DIM)
        kj = _rms(z[:, Q_W + j * HEAD_DIM:Q_W + (j + 1) * HEAD_DIM], gk)
        kb_ref[:, sl] = kj.astype(bf16)
        head_rows = pl.ds(j, tm, stride=N_KV_HEADS)
        k_ref[head_rows, :] = kj
        v_ref[head_rows, :] = v[:, sl]
    vb_ref[...] = v.astype(bf16)
    vt_ref[...] = v.T.astype(bf16)

    low_half = lax.broadcasted_iota(jnp.int32, (tm, LANES), 1) < IDX_DIM
    qi0 = Q_W + 2 * KV_W
    for p in range(IDX_HEADS // 2):
        pair = z[:, qi0 + p * LANES:qi0 + (p + 1) * LANES]
        rolled = pltpu.roll(pair, IDX_DIM, axis=1)
        for e, (own, other) in enumerate(((pair, rolled), (rolled, pair))):
            dup = jnp.where(low_half, own, other)
            single = jnp.where(low_half, own, 0.0)
            base = (2 * p + e) * IDX_K
            qic_ref[:, base:base + LANES] = dup.astype(bf16)
            qic_ref[:, base + LANES:base + 2 * LANES] = (single - _hi(single)).astype(bf16)
    last = z[:, qi0 + QI_W:qi0 + QI_W + LANES]
    rolled = pltpu.roll(last, IDX_DIM, axis=1)
    kic_ref[:, :LANES] = jnp.where(low_half, _hi(last), rolled - _hi(rolled)).astype(bf16)
    kic_ref[:, LANES:] = jnp.where(low_half, _hi(last), 0.0).astype(bf16)
    ki_ref[...] = last[:, :IDX_DIM]
    kiwi_ref[...] = last


def _att_proj(x, g, w, gq, gk, tm):
    t = x.shape[0]
    row = lambda width: pl.BlockSpec((tm, width), lambda i: (i, 0))
    head_rows = pl.BlockSpec((N_KV_HEADS * tm, HEAD_DIM), lambda i: (i, 0))
    head_rows_shape = jax.ShapeDtypeStruct((N_KV_HEADS * t, HEAD_DIM), f32)
    outs = [(Q_W, bf16), None, None, (KV_W, bf16), (KV_W, bf16),
            (IDX_HEADS * IDX_K, bf16), (IDX_K, bf16), (IDX_DIM, f32), (LANES, f32)]
    return pl.pallas_call(
        _att_proj_kernel,
        grid=(t // tm,),
        in_specs=[row(D_MODEL), _const_spec((1, D_MODEL)), _weight_spec((D_MODEL, ATT_IN)),
                  _const_spec((1, HEAD_DIM)), _const_spec((1, HEAD_DIM))],
        out_specs=[row(o[0]) if o else head_rows for o in outs]
        + [pl.BlockSpec((KV_W, tm), lambda i: (0, i))],
        out_shape=[jax.ShapeDtypeStruct((t, o[0]), o[1]) if o else head_rows_shape for o in outs]
        + [jax.ShapeDtypeStruct((KV_W, t), bf16)],
        scratch_shapes=[pltpu.VMEM((D_MODEL, ATT_IN_PAD), bf16)],
        compiler_params=_params(1),
        name="att_proj",
    )(x, g, w, gq, gk)


def _key_to_float(key):
    bits = jnp.where(key < 0, key ^ jnp.int32(0x7FFFFFFF), key)
    return lax.bitcast_convert_type(bits, f32)


def _count(mask, axis=1):
    ones = jnp.where(mask, 1.0, 0.0)
    if axis == 0:
        n, cols = ones.shape
        rows = 4 * SUBLANES
        ones = jnp.sum(ones.reshape(n // rows, rows, cols), axis=0)
    return jnp.sum(ones, axis=axis, keepdims=True)


def _rank_threshold(score_ref, k_top, axis):
    shape = tuple(1 if a == axis else d for a, d in enumerate(score_ref.shape))

    def step(i, t):
        cand = t + lax.shift_left(jnp.int32(1), jnp.int32(31) - i)
        enough = _count(score_ref[...] >= _key_to_float(cand), axis) >= k_top
        return jnp.where(enough, cand, t)

    t = lax.fori_loop(0, 32, step, jnp.full(shape, INT_MIN, jnp.int32))
    return _key_to_float(t), t == INT_MIN


def _topk_bias_keys_major(score_ref, bias_ref, tril_ref, k_top):
    n, _ = score_ref.shape
    t, take_all = _rank_threshold(score_ref, k_top, axis=0)

    def above(s):
        return (s > t) | (take_all & (s > NEG_INF))

    need = k_top - _count(above(score_ref[...]), axis=0)
    ties_before = jnp.zeros(t.shape, f32)
    for c in range(n // LANES):
        sl = slice(c * LANES, (c + 1) * LANES)
        s_c = score_ref[sl, :]
        tie = s_c == t
        tie_f = jnp.where(tie, 1.0, 0.0)
        rank = _dot(tril_ref[...], tie_f.astype(bf16)) + ties_before
        ties_before = ties_before + jnp.sum(tie_f, axis=0, keepdims=True)
        bias_ref[sl, :] = jnp.where(above(s_c) | (tie & (rank <= need)), 0.0, NEG_INF)


def _topk_bias(score_ref, tri_ref, before_ref, k_top, store):
    _, n = score_ref.shape
    t, take_all = _rank_threshold(score_ref, k_top, axis=1)
    def above(s):
        return (s > t) | (take_all & (s > NEG_INF))

    s = score_ref[...]
    need = k_top - _count(above(s))
    tie_b = jnp.where(s == t, 1.0, 0.0).astype(bf16)
    before = _dot(tie_b, before_ref[:n, :])
    for c in range(n // LANES):
        sl = slice(c * LANES, (c + 1) * LANES)
        s_c = s[:, sl]
        rank = _dot(tie_b[:, sl], tri_ref[...]) + before[:, c:c + 1]
        store(c, jnp.where(above(s_c) | ((s_c == t) & (rank <= need)), 0.0, NEG_INF))


def _tie_consts(n):
    a = jnp.arange(LANES)
    tri = (a[:, None] <= a[None, :]).astype(bf16)
    before = ((jnp.arange(n)[:, None] // LANES) < a[None, :]).astype(bf16)
    return tri, before


def _indexer_scores(qic, wi, kic):
    rows = qic.shape[0]
    heads = [qic[:, h * IDX_K:(h + 1) * IDX_K] for h in range(IDX_HEADS)]
    if rows % BF16_ROWS == 0:
        logits = _dot_nt(jnp.concatenate(heads, axis=0), kic)
        logits = [logits[h * rows:(h + 1) * rows] for h in range(IDX_HEADS)]
    else:
        logits = [_dot_nt(qh, kic) for qh in heads]
    score = None
    for h in range(IDX_HEADS):
        relu = jnp.maximum(logits[h], 0.0)
        term = wi[:, IDX_DIM + h:IDX_DIM + h + 1] * relu
        score = term if score is None else score + term
    return score * IDX_SCALE


def _dsa_prompt_tile(lk, q0, k_top, q_ref, qic_ref, kiwi_ref, kic_ref, kb_ref, vt_ref, tril_ref,
                     score_scr, bias_scr, s_scr, p_scr, att_scr):
    tq = q_ref.shape[1]
    qic = qic_ref[0]
    q_idx = jnp.concatenate(
        [qic[:, h * IDX_K:(h + 1) * IDX_K] for h in range(IDX_HEADS)], axis=0)
    logits = _dot_nt(kic_ref[0, :lk, :], q_idx)
    wi_t = kiwi_ref[0].T
    score = None
    for h in range(IDX_HEADS):
        term = wi_t[IDX_DIM + h:IDX_DIM + h + 1, :] * jnp.maximum(logits[:, h * tq:(h + 1) * tq], 0.0)
        score = term if score is None else score + term
    score = score * IDX_SCALE
    q_pos = q0 + lax.broadcasted_iota(jnp.int32, (1, tq), 1)
    admissible = lax.broadcasted_iota(jnp.int32, (lk, 1), 0) <= q_pos
    scores = score_scr.at[:lk, :]
    scores[...] = jnp.where(admissible, score, NEG_INF)

    q = q_ref[0]
    for kvh in range(N_KV_HEADS):
        qs = jnp.concatenate(
            [q[:, (kvh * GROUP + g) * HEAD_DIM:(kvh * GROUP + g + 1) * HEAD_DIM]
             for g in range(GROUP)], axis=0)
        kv_sl = slice(kvh * HEAD_DIM, (kvh + 1) * HEAD_DIM)
        s_scr[kvh, :lk, :] = _dot_nt(kb_ref[0, :lk, kv_sl], qs)

    bias_ref = bias_scr.at[:lk, :]
    _topk_bias_keys_major(scores, bias_ref, tril_ref, k_top)

    exp_scale = (HEAD_DIM ** -0.5) * LOG2E
    for kvh in range(N_KV_HEADS):
        kv_sl = slice(kvh * HEAD_DIM, (kvh + 1) * HEAD_DIM)
        denoms = []
        for g in range(GROUP):
            cols = slice(g * tq, (g + 1) * tq)
            sb = s_scr[kvh, :lk, cols] + bias_ref[...]
            m = jnp.max(sb, axis=0, keepdims=True)
            p = jnp.exp2((sb - m) * exp_scale)
            denoms.append(jnp.sum(p, axis=0, keepdims=True))
            p_scr[kvh, :lk, cols] = p.astype(bf16)
        o_t = _dot(vt_ref[kv_sl, :lk], p_scr[kvh, :lk, :])
        o_t = o_t / jnp.concatenate(denoms, axis=1)
        for g in range(GROUP):
            col = (kvh * GROUP + g) * HEAD_DIM
            att_scr[:, col:col + HEAD_DIM] = o_t[:, g * tq:(g + 1) * tq].T.astype(bf16)


def _dsa_prompt_kernel(q_ref, qic_ref, kiwi_ref, kic_ref, kb_ref, vt_ref, x_ref, wo_ref,
                       tril_ref, o_ref, wo_b, *scratch, k_top, span):
    _cast_weights_once((pl.program_id(0) == 0) & (pl.program_id(1) == 0), [(wo_ref, wo_b)])
    tq = q_ref.shape[1]
    n = kb_ref.shape[1]
    q0 = pl.program_id(1) * tq
    span_class = (q0 + tq - 1) // span
    for c in range(n // span):
        @pl.when(span_class == c)
        def _(lk=(c + 1) * span):
            _dsa_prompt_tile(lk, q0, k_top, q_ref, qic_ref, kiwi_ref, kic_ref, kb_ref, vt_ref,
                             tril_ref, *scratch)
    att_scr = scratch[-1]
    o_ref[0] = x_ref[0] + _dot(att_scr[...], wo_b[...])


def _dsa_prompt(q, qic, kiwi, kic, kb, vt, x, wo, tq, span):
    b, t, _ = q.shape
    k_top = min(TOPK_MAX, t // 4)
    a = jnp.arange(LANES)
    tril = (a[None, :] <= a[:, None]).astype(bf16)
    cols = GROUP * tq
    tile = lambda width: pl.BlockSpec((1, tq, width), lambda i, j: (i, j, 0))
    full = lambda width: pl.BlockSpec((1, t, width), lambda i, j: (i, 0, 0))
    return pl.pallas_call(
        functools.partial(_dsa_prompt_kernel, k_top=k_top, span=span),
        grid=(b, t // tq),
        in_specs=[tile(Q_W), tile(IDX_HEADS * IDX_K), tile(LANES), full(IDX_K), full(KV_W),
                  pl.BlockSpec((KV_W, t), lambda i, j: (0, i)), tile(D_MODEL),
                  _weight_spec((Q_W, D_MODEL)), _const_spec((LANES, LANES))],
        out_specs=tile(D_MODEL),
        out_shape=jax.ShapeDtypeStruct((b, t, D_MODEL), f32),
        scratch_shapes=[pltpu.VMEM((Q_W, D_MODEL), bf16),
                        pltpu.VMEM((t, tq), f32),
                        pltpu.VMEM((t, tq), f32),
                        pltpu.VMEM((N_KV_HEADS, t, cols), f32),
                        pltpu.VMEM((N_KV_HEADS, t, cols), bf16),
                        pltpu.VMEM((tq, Q_W), bf16)],
        compiler_params=_params(2),
        name="dsa_prompt",
    )(q, qic, kiwi, kic, kb, vt, x, wo, tril)


def _sample_scores_kernel(pt_ref, qic_ref, kiwi_ref, kicn_ref, cache_ref, o_ref, buf, sem):
    b = pl.program_id(0)
    n_pages = pt_ref.shape[1]
    n_past = n_pages * PAGE_SIZE
    t = qic_ref.shape[0]

    def page_copies(batch, slot):
        return [pltpu.make_async_copy(
            cache_ref.at[pt_ref[batch, p]],
            buf.at[slot, :, pl.ds(p * PAGE_SIZE, PAGE_SIZE)], sem.at[slot])
            for p in range(n_pages)]

    slot = lax.rem(b, 2)

    @pl.when(b == 0)
    def _():
        for copy in page_copies(0, 0):
            copy.start()

    @pl.when(b + 1 < pl.num_programs(0))
    def _():
        for copy in page_copies(b + 1, 1 - slot):
            copy.start()

    for copy in page_copies(b, slot):
        copy.wait()

    keys_t = buf[slot]
    k_hi = keys_t.astype(bf16)
    k_lo = (keys_t - k_hi.astype(f32)).astype(bf16)
    qic = qic_ref[...].astype(f32)

    def stacked(offset):
        return jnp.concatenate(
            [qic[:, h * IDX_K + offset:h * IDX_K + offset + IDX_DIM] for h in range(IDX_HEADS)],
            axis=0).astype(bf16)

    q_hi, q_lo = stacked(0), stacked(2 * IDX_DIM)
    logits = _dot(q_hi, k_hi) + _dot(q_hi, k_lo) + _dot(q_lo, k_hi)
    wi = kiwi_ref[...]
    score = None
    for h in range(IDX_HEADS):
        term = wi[:, IDX_DIM + h:IDX_DIM + h + 1] * jnp.maximum(logits[h * t:(h + 1) * t], 0.0)
        score = term if score is None else score + term
    o_ref[:, :n_past] = score * IDX_SCALE
    new = _indexer_scores(qic_ref[...], wi, kicn_ref[...])
    o_ref[:, n_past:] = jnp.concatenate([new, jnp.zeros((t, LANES - t), f32)], axis=1)


def _sample_scores(page_table, qic, kiwi, kic_new, cache_ki_t):
    nb, n_pages = page_table.shape
    t = qic.shape[0] // nb
    n_past = n_pages * PAGE_SIZE
    row = lambda width: pl.BlockSpec((t, width), lambda b, pt: (b, 0))
    return pl.pallas_call(
        _sample_scores_kernel,
        grid_spec=pltpu.PrefetchScalarGridSpec(
            num_scalar_prefetch=1,
            grid=(nb,),
            in_specs=[row(IDX_HEADS * IDX_K), row(LANES), row(IDX_K),
                      pl.BlockSpec(memory_space=pl.ANY)],
            out_specs=row(n_past + LANES),
            scratch_shapes=[pltpu.VMEM((2, IDX_DIM, n_past), f32),
                            pltpu.SemaphoreType.DMA((2,))],
        ),
        out_shape=jax.ShapeDtypeStruct((nb * t, n_past + LANES), f32),
        compiler_params=_params(1),
        name="dsa_sample_scores",
    )(page_table, qic, kiwi, kic_new, cache_ki_t)


def _sample_select_kernel(sc_ref, tri_ref, before_ref, o_ref, score_scr, *, k_top, t, n_past):
    rows, n = sc_ref.shape
    qi = lax.broadcasted_iota(jnp.int32, (rows, n), 0) & (t - 1)
    kj = lax.broadcasted_iota(jnp.int32, (rows, n), 1)
    admissible = kj <= n_past + qi
    score_scr[...] = jnp.where(admissible, sc_ref[...], NEG_INF)

    def store(c, piece):
        o_ref[:, c * LANES:(c + 1) * LANES] = piece

    _topk_bias(score_scr, tri_ref, before_ref, k_top, store)


def _sample_select(scores, t, n_past, rows_per_step):
    n_rows, n_all = scores.shape
    assert t & (t - 1) == 0 and rows_per_step % t == 0, (t, rows_per_step)
    k_top = min(TOPK_MAX, (n_past + t) // 4)
    tri, before = _tie_consts(n_all)
    block = pl.BlockSpec((rows_per_step, n_all), lambda i: (i, 0))
    return pl.pallas_call(
        functools.partial(_sample_select_kernel, k_top=k_top, t=t, n_past=n_past),
        grid=(n_rows // rows_per_step,),
        in_specs=[block, _const_spec((LANES, LANES)), _const_spec((n_all, LANES))],
        out_specs=block,
        out_shape=jax.ShapeDtypeStruct((n_rows, n_all), f32),
        scratch_shapes=[pltpu.VMEM((rows_per_step, n_all), f32)],
        compiler_params=_params(1),
        name="dsa_sample_select",
    )(scores, tri, before)


def _sample_attend_kernel(pt_ref, bias_ref, biasn_ref, q_ref, kbn_ref, vbn_ref, x_ref, wo_ref,
                          ck_ref, cv_ref, o_ref, kbuf, vbuf, ksem, vsem, m_scr, l_scr, acc_scr,
                          wo_b, *, pps):
    b, j = pl.program_id(0), pl.program_id(1)
    n_j = pl.num_programs(1)
    step = b * n_j + j
    _cast_weights_once(step == 0, [(wo_ref, wo_b)])
    t = q_ref.shape[0]
    page_rows = PAGE_SIZE * N_KV_HEADS

    def page_copies(batch, chunk, slot):
        copies = []
        for i in range(pps):
            src = pl.ds(pl.multiple_of(pt_ref[batch, chunk * pps + i] * page_rows, page_rows),
                        page_rows)
            dst = pl.ds(i * page_rows, page_rows)
            copies.append(pltpu.make_async_copy(ck_ref.at[src, :], kbuf.at[slot, dst, :], ksem.at[slot]))
            copies.append(pltpu.make_async_copy(cv_ref.at[src, :], vbuf.at[slot, dst, :], vsem.at[slot]))
        return copies

    slot = lax.rem(step, 2)

    @pl.when(step == 0)
    def _():
        for copy in page_copies(0, 0, 0):
            copy.start()

    @pl.when(step + 1 < pl.num_programs(0) * n_j)
    def _():
        nxt = step + 1
        for copy in page_copies(nxt // n_j, lax.rem(nxt, n_j), 1 - slot):
            copy.start()

    @pl.when(j == 0)
    def _():
        m_scr[...] = jnp.full(m_scr.shape, -1e30, f32)
        l_scr[...] = jnp.zeros(l_scr.shape, f32)
        acc_scr[...] = jnp.zeros(acc_scr.shape, f32)

    q = q_ref[...]

    rows = GROUP * t

    def attend(k_heads, v_heads, bias):
        s = jnp.concatenate(
            [_dot_nt(jnp.concatenate(
                [q[:, (kvh * GROUP + g) * HEAD_DIM:(kvh * GROUP + g + 1) * HEAD_DIM]
                 for g in range(GROUP)], axis=0), k_heads[kvh]) for kvh in range(N_KV_HEADS)],
            axis=0)
        s = s * (HEAD_DIM ** -0.5) + jnp.concatenate([bias] * N_HEADS, axis=0)
        m_old = m_scr[...]
        m_new = jnp.maximum(m_old, jnp.max(s, axis=-1, keepdims=True))
        alpha = jnp.exp(m_old - m_new)
        p = jnp.exp(s - m_new)
        l_scr[...] = alpha * l_scr[...] + jnp.sum(p, axis=-1, keepdims=True)
        pv = jnp.concatenate(
            [_dot(p[kvh * rows:(kvh + 1) * rows].astype(bf16), v_heads[kvh])
             for kvh in range(N_KV_HEADS)], axis=0)
        acc_scr[...] = alpha * acc_scr[...] + pv
        m_scr[...] = m_new

    for copy in page_copies(b, j, slot):
        copy.wait()

    def head_rows(buf, kvh):
        return buf[slot, pl.ds(kvh, pps * PAGE_SIZE, stride=N_KV_HEADS), :].astype(bf16)

    attend([head_rows(kbuf, kvh) for kvh in range(N_KV_HEADS)],
           [head_rows(vbuf, kvh) for kvh in range(N_KV_HEADS)], bias_ref[...])

    @pl.when(j == n_j - 1)
    def _():
        pad = jnp.zeros((LANES - t, HEAD_DIM), bf16)

        def new_heads(ref):
            return [jnp.concatenate([ref[:, kvh * HEAD_DIM:(kvh + 1) * HEAD_DIM], pad], axis=0)
                    for kvh in range(N_KV_HEADS)]

        attend(new_heads(kbn_ref), new_heads(vbn_ref), biasn_ref[...])
        o = acc_scr[...] / l_scr[...]
        att = jnp.concatenate([o[h * t:(h + 1) * t] for h in range(N_HEADS)],
                              axis=1).astype(bf16)
        o_ref[...] = x_ref[...] + _dot(att, wo_b[...])


def _sample_attend(page_table, bias, q, kb_new, vb_new, x, wo, cache_k, cache_v, pages_per_step):
    nb, n_pages = page_table.shape
    t = q.shape[0] // nb
    steps = n_pages // pages_per_step
    span = pages_per_step * PAGE_SIZE
    row = lambda width: pl.BlockSpec((t, width), lambda b, j, pt: (b, 0))
    rows = N_HEADS * t
    buf = pltpu.VMEM((2, span * N_KV_HEADS, HEAD_DIM), f32)
    return pl.pallas_call(
        functools.partial(_sample_attend_kernel, pps=pages_per_step),
        grid_spec=pltpu.PrefetchScalarGridSpec(
            num_scalar_prefetch=1,
            grid=(nb, steps),
            in_specs=[pl.BlockSpec((t, span), lambda b, j, pt: (b, j)),
                      pl.BlockSpec((t, LANES), lambda b, j, pt: (b, n_pages)),
                      row(Q_W), row(KV_W), row(KV_W), row(D_MODEL),
                      _weight_spec((Q_W, D_MODEL)),
                      pl.BlockSpec(memory_space=pl.ANY), pl.BlockSpec(memory_space=pl.ANY)],
            out_specs=row(D_MODEL),
            scratch_shapes=[buf, buf, pltpu.SemaphoreType.DMA((2,)), pltpu.SemaphoreType.DMA((2,)),
                            pltpu.VMEM((rows, 1), f32),
                            pltpu.VMEM((rows, 1), f32),
                            pltpu.VMEM((rows, HEAD_DIM), f32),
                            pltpu.VMEM((Q_W, D_MODEL), bf16)],
        ),
        out_shape=jax.ShapeDtypeStruct((nb * t, D_MODEL), f32),
        compiler_params=_params(2),
        name="dsa_sample_attend",
    )(page_table, bias, bias, q, kb_new, vb_new, x, wo, cache_k, cache_v)


def _ffn_kernel(x_ref, g_ref, wgu_ref, wd_ref, o_ref, *, n_ff):
    x = x_ref[...]
    h = _rms(x, g_ref[...]).astype(bf16)
    acc = x
    width = D_FF // n_ff
    for c in range(n_ff):
        gate = _dot(h, wgu_ref[0, :, c * width:(c + 1) * width])
        up = _dot(h, wgu_ref[0, :, D_FF + c * width:D_FF + (c + 1) * width])
        acc = acc + _dot((jax.nn.silu(gate) * up).astype(bf16),
                         wd_ref[0, c * width:(c + 1) * width, :])
    o_ref[...] = acc


def _ffn(x, g, wgu, wd, layer, tm, n_ff):
    t = x.shape[0]
    row = pl.BlockSpec((tm, D_MODEL), lambda i: (i, 0))
    resident = lambda a: pl.BlockSpec((1,) + a.shape[1:], lambda i: (layer, 0, 0),
                                      pipeline_mode=pl.Buffered(1))
    return pl.pallas_call(
        functools.partial(_ffn_kernel, n_ff=n_ff),
        grid=(t // tm,),
        in_specs=[row, _const_spec((1, D_MODEL)), resident(wgu), resident(wd)],
        out_specs=row,
        out_shape=jax.ShapeDtypeStruct((t, D_MODEL), f32),
        compiler_params=_params(1),
        name="ffn",
    )(x, g, wgu, wd)


def _sgu_kernel(x_ref, g_ref, win_ref, gv_ref, ws_ref, bias_ref, wout_ref, *refs,
                chunk_len, emit_v):
    if emit_v:
        o_ref, v_ref, gated_scr, win_b, wout_b = refs
    else:
        o_ref, gated_scr, win_b, wout_b = refs
    _cast_weights_once(pl.program_id(0) == 0, [(win_ref, win_b), (wout_ref, wout_b)])
    tm = x_ref.shape[0]
    n_sub = tm // CHUNK
    x = x_ref[...]
    h = _rms(x, g_ref[...]).astype(bf16)
    z = jax.nn.gelu(_dot(h, win_b[...]))
    u = z[:, :D_MODEL]
    v = _rms(z[:, D_MODEL:], gv_ref[...])
    if emit_v:
        v_ref[...] = v
    vb = v.astype(bf16)
    ti = lax.broadcasted_iota(jnp.int32, (CHUNK, CHUNK), 0)
    si = lax.broadcasted_iota(jnp.int32, (CHUNK, CHUNK), 1)
    shift = chunk_len.bit_length() - 1
    causal = (si <= ti) & (lax.shift_right_logical(si, shift) == lax.shift_right_logical(ti, shift))
    bias = bias_ref[...]
    for grp in range(SG_GROUPS):
        cols = slice(grp * LANES, (grp + 1) * LANES)
        w = jnp.where(causal, ws_ref[grp], 0.0).astype(bf16)
        vg = jnp.concatenate([vb[c * CHUNK:(c + 1) * CHUNK, cols] for c in range(n_sub)], axis=1)
        mixed = _dot(w, vg) + bias[:, grp:grp + 1]
        for c in range(n_sub):
            rows = slice(c * CHUNK, (c + 1) * CHUNK)
            gated_scr[rows, cols] = (u[rows, cols] * mixed[:, c * LANES:(c + 1) * LANES]).astype(bf16)
    o_ref[...] = x + _dot(gated_scr[...], wout_b[...])


def _sgu(x, g, w_in, g_v, w_s, bias_t, w_out, tm, chunk_len, emit_v):
    t = x.shape[0]
    row = pl.BlockSpec((tm, D_MODEL), lambda i: (i, 0))
    out_shape = [jax.ShapeDtypeStruct((t, D_MODEL), f32)] * (2 if emit_v else 1)
    return pl.pallas_call(
        functools.partial(_sgu_kernel, chunk_len=chunk_len, emit_v=emit_v),
        grid=(t // tm,),
        in_specs=[row, _const_spec((1, D_MODEL)), _weight_spec(w_in.shape),
                  _const_spec((1, D_MODEL)), _const_spec(w_s.shape), _const_spec(bias_t.shape),
                  _weight_spec(w_out.shape)],
        out_specs=[row] * len(out_shape),
        out_shape=out_shape,
        scratch_shapes=[pltpu.VMEM((tm, D_MODEL), bf16), pltpu.VMEM(w_in.shape, bf16),
                        pltpu.VMEM(w_out.shape, bf16)],
        compiler_params=_params(1),
        name="sgu",
    )(x, g, w_in, g_v, w_s, bias_t, w_out)


def kernel(x_prompt, x_sample, cache_k, cache_v, cache_kidx, page_table, norm_mix, norm_ffn,
           att_w_in, att_q_norm, att_k_norm, att_w_out, sg_w_in, sg_v_norm, sg_w_s, sg_b_s,
           sg_w_out, ffn_w_gu, ffn_w_down):
    nb_p, t_p, _ = x_prompt.shape
    nb_s, t_s, _ = x_sample.shape
    n_pool = cache_k.shape[1]
    xp = x_prompt.reshape(nb_p * t_p, D_MODEL)
    xs = x_sample.reshape(nb_s * t_s, D_MODEL)
    vec = lambda a: a.reshape(1, -1)

    w_out = att_w_out[0]
    proj = functools.partial(_att_proj, g=vec(norm_mix[0]), w=att_w_in[0],
                             gq=vec(att_q_norm[0]), gk=vec(att_k_norm[0]))
    q_p, k_p, v_p, kb_p, _, qic_p, kic_p, ki_p, kiwi_p, vt_p = proj(xp, tm=512)
    q_s, k_s, v_s, kb_s, vb_s, qic_s, kic_s, ki_s, kiwi_s, _ = proj(xs, tm=nb_s * t_s)

    per_seq = lambda a: a.reshape(nb_p, t_p, a.shape[-1])
    xp = _dsa_prompt(per_seq(q_p), per_seq(qic_p), per_seq(kiwi_p), per_seq(kic_p),
                     per_seq(kb_p), vt_p, per_seq(xp), w_out, tq=128, span=512)
    xp = xp.reshape(nb_p * t_p, D_MODEL)

    scores_s = _sample_scores(page_table, qic_s, kiwi_s, kic_s, jnp.swapaxes(cache_kidx[0], 1, 2))
    bias_s = _sample_select(scores_s, t_s, page_table.shape[1] * PAGE_SIZE, rows_per_step=64)
    kv_rows = n_pool * PAGE_SIZE * N_KV_HEADS
    xs = _sample_attend(page_table, bias_s, q_s, kb_s, vb_s, xs, w_out,
                        cache_k[0].reshape(kv_rows, HEAD_DIM),
                        cache_v[0].reshape(kv_rows, HEAD_DIM), pages_per_step=64)

    ffn_gu, ffn_down = ffn_w_gu.astype(bf16), ffn_w_down.astype(bf16)

    def ffn(x, layer, tm):
        return _ffn(x, vec(norm_ffn[layer]), ffn_gu, ffn_down, layer, tm, n_ff=11)

    xp = ffn(xp, 0, 512)
    xs = ffn(xs, 0, nb_s * t_s)

    sgu = functools.partial(_sgu, g=vec(norm_mix[1]), w_in=sg_w_in[0], g_v=vec(sg_v_norm[0]),
                            w_out=sg_w_out[0])
    (xp,) = sgu(xp, w_s=sg_w_s[0], bias_t=sg_b_s[0].T, tm=512, chunk_len=CHUNK, emit_v=False)
    reps = CHUNK // t_s
    xs, sgv = sgu(xs, w_s=jnp.tile(sg_w_s[0][:, :t_s, :t_s], (1, reps, reps)),
                  bias_t=jnp.tile(sg_b_s[0][:, :t_s], (1, reps)).T, tm=nb_s * t_s,
                  chunk_len=t_s, emit_v=True)

    xp = ffn(xp, 1, 512)
    xs = ffn(xs, 1, nb_s * t_s)

    y_prompt = xp.reshape(nb_p, t_p, D_MODEL)
    y_sample = xs.reshape(nb_s, t_s, D_MODEL)
    kv_p = lambda a: a.reshape(1, nb_p, t_p, N_KV_HEADS, HEAD_DIM)
    kv_s = lambda a: a.reshape(1, nb_s, t_s, N_KV_HEADS, HEAD_DIM)
    return (y_prompt, y_sample, kv_p(k_p), kv_p(v_p), ki_p.reshape(1, nb_p, t_p, IDX_DIM),
            kv_s(k_s), kv_s(v_s), ki_s.reshape(1, nb_s, t_s, IDX_DIM),
            sgv.reshape(1, nb_s, t_s, D_MODEL))
```

```python
import functools

import jax
import jax.numpy as jnp
from jax import lax
from jax.experimental import pallas as pl
from jax.experimental.pallas import tpu as pltpu

D_MODEL = 1024
HEAD_DIM = 128
N_HEADS = 8
N_KV_HEADS = 2
GROUP = N_HEADS // N_KV_HEADS
IDX_HEADS = 4
IDX_DIM = 64
IDX_SCALE = (IDX_DIM ** -0.5) * (IDX_HEADS ** -0.5)
TOPK_MAX = 256
CHUNK = 128
SG_GROUPS = 8
D_FF = 2816
PAGE_SIZE = 128
EPS = 1e-6

Q_W = N_HEADS * HEAD_DIM
KV_W = N_KV_HEADS * HEAD_DIM
QI_W = IDX_HEADS * IDX_DIM
ATT_IN = Q_W + 2 * KV_W + QI_W + IDX_DIM + IDX_HEADS
LANES = 128
SUBLANES = 8
BF16_ROWS = 16
SOFTMAX2_SCALE = (HEAD_DIM ** -0.5) * 1.4426950408889634
ATT_IN_PAD = -(-ATT_IN // LANES) * LANES
IDX_K = 4 * IDX_DIM
INT_MIN = -(2 ** 31)
NEG_INF = float("-inf")
VMEM_LIMIT = 56 * 1024 * 1024

bf16 = jnp.bfloat16
f32 = jnp.float32


def _rms(x, g):
    return x * lax.rsqrt(jnp.mean(x * x, axis=-1, keepdims=True) + EPS) * g


def _dot(a, b):
    return jnp.dot(a, b, preferred_element_type=f32)


def _dot_nt(a, b):
    return lax.dot_general(a, b, (((1,), (1,)), ((), ())), preferred_element_type=f32)


def _hi(x):
    return x.astype(bf16).astype(f32)


def _params(n_grid):
    return pltpu.CompilerParams(
        dimension_semantics=("arbitrary",) * n_grid, vmem_limit_bytes=VMEM_LIMIT)


def _const_spec(shape):
    nd = len(shape)
    return pl.BlockSpec(shape, lambda *_: (0,) * nd)


def _weight_spec(shape):
    nd = len(shape)
    return pl.BlockSpec(shape, lambda *_: (0,) * nd, pipeline_mode=pl.Buffered(1))


def _cast_weights_once(is_first_step, pairs):
    @pl.when(is_first_step)
    def _():
        for src, dst in pairs:
            dst[...] = src[...].astype(dst.dtype)


def _att_proj_kernel(x_ref, g_ref, w_ref, gq_ref, gk_ref,
                     q_ref, k_ref, v_ref, kb_ref, vb_ref, qic_ref, kic_ref, ki_ref, kiwi_ref, vt_ref,
                     w_b):
    @pl.when(pl.program_id(0) == 0)
    def _():
        w_b[:, ATT_IN_PAD - LANES:] = jnp.zeros((D_MODEL, LANES), bf16)
        w_b[:, :ATT_IN] = w_ref[...].astype(bf16)

    tm = x_ref.shape[0]
    h = _rms(x_ref[...], g_ref[...]).astype(bf16)
    z = _dot(h, w_b[...])
    gq = gq_ref[...]
    gk = gk_ref[...]
    for j in range(N_HEADS):
        sl = slice(j * HEAD_DIM, (j + 1) * HEAD_DIM)
        q_ref[:, sl] = (_rms(z[:, sl], gq) * SOFTMAX2_SCALE).astype(bf16)
    v = z[:, Q_W + KV_W:Q_W + 2 * KV_W]
    for j in range(N_KV_HEADS):
        sl = slice(j * HEAD_DIM, (j + 1) * HEAD_DIM)
        kj = _rms(z[:, Q_W + j * HEAD_DIM:Q_W + (j + 1) * HEAD_DIM], gk)
        kb_ref[:, sl] = kj.astype(bf16)
        head_rows = pl.ds(j, tm, stride=N_KV_HEADS)
        k_ref[head_rows, :] = kj
        v_ref[head_rows, :] = v[:, sl]
    vb_ref[...] = v.astype(bf16)
    vt_ref[...] = v.T.astype(bf16)

    low_half = lax.broadcasted_iota(jnp.int32, (tm, LANES), 1) < IDX_DIM
    qi0 = Q_W + 2 * KV_W
    for p in range(IDX_HEADS // 2):
        pair = z[:, qi0 + p * LANES:qi0 + (p + 1) * LANES]
        rolled = pltpu.roll(pair, IDX_DIM, axis=1)
        for e, (own, other) in enumerate(((pair, rolled), (rolled, pair))):
            dup = jnp.where(low_half, own, other)
            single = jnp.where(low_half, own, 0.0)
            base = (2 * p + e) * IDX_K
            qic_ref[:, base:base + LANES] = dup.astype(bf16)
            qic_ref[:, base + LANES:base + 2 * LANES] = (single - _hi(single)).astype(bf16)
    last = z[:, qi0 + QI_W:qi0 + QI_W + LANES]
    rolled = pltpu.roll(last, IDX_DIM, axis=1)
    kic_ref[:, :LANES] = jnp.where(low_half, _hi(last), rolled - _hi(rolled)).astype(bf16)
    kic_ref[:, LANES:] = jnp.where(low_half, _hi(last), 0.0).astype(bf16)
    ki_ref[...] = last[:, :IDX_DIM]
    kiwi_ref[...] = last


def _att_proj(x, g, w, gq, gk, tm):
    t = x.shape[0]
    row = lambda width: pl.BlockSpec((tm, width), lambda i: (i, 0))
    head_rows = pl.BlockSpec((N_KV_HEADS * tm, HEAD_DIM), lambda i: (i, 0))
    head_rows_shape = jax.ShapeDtypeStruct((N_KV_HEADS * t, HEAD_DIM), f32)
    outs = [(Q_W, bf16), None, None, (KV_W, bf16), (KV_W, bf16),
            (IDX_HEADS * IDX_K, bf16), (IDX_K, bf16), (IDX_DIM, f32), (LANES, f32)]
    return pl.pallas_call(
        _att_proj_kernel,
        grid=(t // tm,),
        in_specs=[row(D_MODEL), _const_spec((1, D_MODEL)), _weight_spec((D_MODEL, ATT_IN)),
                  _const_spec((1, HEAD_DIM)), _const_spec((1, HEAD_DIM))],
        out_specs=[row(o[0]) if o else head_rows for o in outs]
        + [pl.BlockSpec((KV_W, tm), lambda i: (0, i))],
        out_shape=[jax.ShapeDtypeStruct((t, o[0]), o[1]) if o else head_rows_shape for o in outs]
        + [jax.ShapeDtypeStruct((KV_W, t), bf16)],
        scratch_shapes=[pltpu.VMEM((D_MODEL, ATT_IN_PAD), bf16)],
        compiler_params=_params(1),
        name="att_proj",
    )(x, g, w, gq, gk)


def _key_to_float(key):
    bits = jnp.where(key < 0, key ^ jnp.int32(0x7FFFFFFF), key)
    return lax.bitcast_convert_type(bits, f32)


def _count(mask, axis=1):
    ones = jnp.where(mask, 1.0, 0.0)
    if axis == 0:
        n, cols = ones.shape
        rows = 4 * SUBLANES
        ones = jnp.sum(ones.reshape(n // rows, rows, cols), axis=0)
    return jnp.sum(ones, axis=axis, keepdims=True)


def _rank_threshold(score_ref, k_top, axis):
    shape = tuple(1 if a == axis else d for a, d in enumerate(score_ref.shape))

    def step(i, t):
        cand = t + lax.shift_left(jnp.int32(1), jnp.int32(31) - i)
        enough = _count(score_ref[...] >= _key_to_float(cand), axis) >= k_top
        return jnp.where(enough, cand, t)

    key = lax.fori_loop(0, 32, step, jnp.full(shape, INT_MIN, jnp.int32))
    t = _key_to_float(key)
    return t, jnp.where(key == INT_MIN, NEG_INF, t)


def _topk_bias_keys_major(score_ref, bias_ref, tril_ref, k_top):
    n, _ = score_ref.shape
    t, t_above = _rank_threshold(score_ref, k_top, axis=0)
    need = k_top - _count(score_ref[...] > t_above, axis=0)
    ties_before = jnp.zeros(t.shape, f32)
    for c in range(n // LANES):
        sl = slice(c * LANES, (c + 1) * LANES)
        s_c = score_ref[sl, :]
        tie = s_c == t
        tie_f = jnp.where(tie, 1.0, 0.0)
        rank = _dot(tril_ref[...], tie_f.astype(bf16)) + ties_before
        ties_before = ties_before + jnp.sum(tie_f, axis=0, keepdims=True)
        bias_ref[sl, :] = jnp.where((s_c > t_above) | (tie & (rank <= need)), 0.0, NEG_INF)


def _topk_bias(score_ref, tri_ref, before_ref, k_top, store):
    _, n = score_ref.shape
    t, t_above = _rank_threshold(score_ref, k_top, axis=1)
    s = score_ref[...]
    need = k_top - _count(s > t_above)
    tie_b = jnp.where(s == t, 1.0, 0.0).astype(bf16)
    before = _dot(tie_b, before_ref[:n, :])
    for c in range(n // LANES):
        sl = slice(c * LANES, (c + 1) * LANES)
        s_c = s[:, sl]
        rank = _dot(tie_b[:, sl], tri_ref[...]) + before[:, c:c + 1]
        store(c, jnp.where((s_c > t_above) | ((s_c == t) & (rank <= need)), 0.0, NEG_INF))


def _tie_consts(n):
    a = jnp.arange(LANES)
    tri = (a[:, None] <= a[None, :]).astype(bf16)
    before = ((jnp.arange(n)[:, None] // LANES) < a[None, :]).astype(bf16)
    return tri, before


def _indexer_scores(qic, wi, kic):
    rows = qic.shape[0]
    heads = [qic[:, h * IDX_K:(h + 1) * IDX_K] for h in range(IDX_HEADS)]
    if rows % BF16_ROWS == 0:
        logits = _dot_nt(jnp.concatenate(heads, axis=0), kic)
        logits = [logits[h * rows:(h + 1) * rows] for h in range(IDX_HEADS)]
    else:
        logits = [_dot_nt(qh, kic) for qh in heads]
    score = None
    for h in range(IDX_HEADS):
        relu = jnp.maximum(logits[h], 0.0)
        term = wi[:, IDX_DIM + h:IDX_DIM + h + 1] * relu
        score = term if score is None else score + term
    return score * IDX_SCALE


def _dsa_prompt_tile(lk, q0, k_top, q_ref, qic_ref, kiwi_ref, kic_ref, kb_ref, vt_ref, tril_ref,
                     score_scr, bias_scr, s_scr, p_scr, att_scr):
    tq = q_ref.shape[1]
    qic = qic_ref[0]
    q_idx = jnp.concatenate(
        [qic[:, h * IDX_K:(h + 1) * IDX_K] for h in range(IDX_HEADS)], axis=0)
    logits = _dot_nt(kic_ref[0, :lk, :], q_idx)
    wi_t = kiwi_ref[0].T
    score = None
    for h in range(IDX_HEADS):
        term = wi_t[IDX_DIM + h:IDX_DIM + h + 1, :] * jnp.maximum(logits[:, h * tq:(h + 1) * tq], 0.0)
        score = term if score is None else score + term
    score = score * IDX_SCALE
    q_pos = q0 + lax.broadcasted_iota(jnp.int32, (1, tq), 1)
    admissible = lax.broadcasted_iota(jnp.int32, (lk, 1), 0) <= q_pos
    scores = score_scr.at[:lk, :]
    scores[...] = jnp.where(admissible, score, NEG_INF)

    q = q_ref[0]
    for kvh in range(N_KV_HEADS):
        qs = jnp.concatenate(
            [q[:, (kvh * GROUP + g) * HEAD_DIM:(kvh * GROUP + g + 1) * HEAD_DIM]
             for g in range(GROUP)], axis=0)
        kv_sl = slice(kvh * HEAD_DIM, (kvh + 1) * HEAD_DIM)
        s_scr[kvh, :lk, :] = _dot_nt(kb_ref[0, :lk, kv_sl], qs)

    bias_ref = bias_scr.at[:lk, :]
    _topk_bias_keys_major(scores, bias_ref, tril_ref, k_top)

    for kvh in range(N_KV_HEADS):
        kv_sl = slice(kvh * HEAD_DIM, (kvh + 1) * HEAD_DIM)
        denoms = []
        for g in range(GROUP):
            cols = slice(g * tq, (g + 1) * tq)
            sb = s_scr[kvh, :lk, cols] + bias_ref[...]
            m = jnp.max(sb, axis=0, keepdims=True)
            p = jnp.exp2(sb - m)
            denoms.append(jnp.sum(p, axis=0, keepdims=True))
            p_scr[kvh, :lk, cols] = p.astype(bf16)
        o_t = _dot(vt_ref[kv_sl, :lk], p_scr[kvh, :lk, :])
        o_t = o_t / jnp.concatenate(denoms, axis=1)
        for g in range(GROUP):
            col = (kvh * GROUP + g) * HEAD_DIM
            att_scr[:, col:col + HEAD_DIM] = o_t[:, g * tq:(g + 1) * tq].T.astype(bf16)


def _dsa_prompt_kernel(q_ref, qic_ref, kiwi_ref, kic_ref, kb_ref, vt_ref, x_ref, wo_ref,
                       tril_ref, o_ref, wo_b, *scratch, k_top, span):
    _cast_weights_once((pl.program_id(0) == 0) & (pl.program_id(1) == 0), [(wo_ref, wo_b)])
    tq = q_ref.shape[1]
    n = kb_ref.shape[1]
    q0 = pl.program_id(1) * tq
    span_class = (q0 + tq - 1) // span
    for c in range(n // span):
        @pl.when(span_class == c)
        def _(lk=(c + 1) * span):
            _dsa_prompt_tile(lk, q0, k_top, q_ref, qic_ref, kiwi_ref, kic_ref, kb_ref, vt_ref,
                             tril_ref, *scratch)
    att_scr = scratch[-1]
    o_ref[0] = x_ref[0] + _dot(att_scr[...], wo_b[...])


def _dsa_prompt(q, qic, kiwi, kic, kb, vt, x, wo, tq, span):
    b, t, _ = q.shape
    k_top = min(TOPK_MAX, t // 4)
    a = jnp.arange(LANES)
    tril = (a[None, :] <= a[:, None]).astype(bf16)
    cols = GROUP * tq
    tile = lambda width: pl.BlockSpec((1, tq, width), lambda i, j: (i, j, 0))
    full = lambda width: pl.BlockSpec((1, t, width), lambda i, j: (i, 0, 0))
    return pl.pallas_call(
        functools.partial(_dsa_prompt_kernel, k_top=k_top, span=span),
        grid=(b, t // tq),
        in_specs=[tile(Q_W), tile(IDX_HEADS * IDX_K), tile(LANES), full(IDX_K), full(KV_W),
                  pl.BlockSpec((KV_W, t), lambda i, j: (0, i)), tile(D_MODEL),
                  _weight_spec((Q_W, D_MODEL)), _const_spec((LANES, LANES))],
        out_specs=tile(D_MODEL),
        out_shape=jax.ShapeDtypeStruct((b, t, D_MODEL), f32),
        scratch_shapes=[pltpu.VMEM((Q_W, D_MODEL), bf16),
                        pltpu.VMEM((t, tq), f32),
                        pltpu.VMEM((t, tq), f32),
                        pltpu.VMEM((N_KV_HEADS, t, cols), f32),
                        pltpu.VMEM((N_KV_HEADS, t, cols), bf16),
                        pltpu.VMEM((tq, Q_W), bf16)],
        compiler_params=_params(2),
        name="dsa_prompt",
    )(q, qic, kiwi, kic, kb, vt, x, wo, tril)


def _sample_scores_kernel(pt_ref, qic_ref, kiwi_ref, kicn_ref, cache_ref, o_ref, buf, sem):
    b = pl.program_id(0)
    n_pages = pt_ref.shape[1]
    n_past = n_pages * PAGE_SIZE
    t = qic_ref.shape[0]

    def page_copies(batch, slot):
        return [pltpu.make_async_copy(
            cache_ref.at[pt_ref[batch, p]],
            buf.at[slot, :, pl.ds(p * PAGE_SIZE, PAGE_SIZE)], sem.at[slot])
            for p in range(n_pages)]

    slot = lax.rem(b, 2)

    @pl.when(b == 0)
    def _():
        for copy in page_copies(0, 0):
            copy.start()

    @pl.when(b + 1 < pl.num_programs(0))
    def _():
        for copy in page_copies(b + 1, 1 - slot):
            copy.start()

    for copy in page_copies(b, slot):
        copy.wait()

    keys_t = buf[slot]
    k_hi = keys_t.astype(bf16)
    k_lo = (keys_t - k_hi.astype(f32)).astype(bf16)
    qic = qic_ref[...].astype(f32)

    def stacked(offset):
        return jnp.concatenate(
            [qic[:, h * IDX_K + offset:h * IDX_K + offset + IDX_DIM] for h in range(IDX_HEADS)],
            axis=0).astype(bf16)

    q_hi, q_lo = stacked(0), stacked(2 * IDX_DIM)
    logits = _dot(q_hi, k_hi) + _dot(q_hi, k_lo) + _dot(q_lo, k_hi)
    wi = kiwi_ref[...]
    score = None
    for h in range(IDX_HEADS):
        term = wi[:, IDX_DIM + h:IDX_DIM + h + 1] * jnp.maximum(logits[h * t:(h + 1) * t], 0.0)
        score = term if score is None else score + term
    o_ref[:, :n_past] = score * IDX_SCALE
    new = _indexer_scores(qic_ref[...], wi, kicn_ref[...])
    o_ref[:, n_past:] = jnp.concatenate([new, jnp.zeros((t, LANES - t), f32)], axis=1)


def _sample_scores(page_table, qic, kiwi, kic_new, cache_ki_t):
    nb, n_pages = page_table.shape
    t = qic.shape[0] // nb
    n_past = n_pages * PAGE_SIZE
    row = lambda width: pl.BlockSpec((t, width), lambda b, pt: (b, 0))
    return pl.pallas_call(
        _sample_scores_kernel,
        grid_spec=pltpu.PrefetchScalarGridSpec(
            num_scalar_prefetch=1,
            grid=(nb,),
            in_specs=[row(IDX_HEADS * IDX_K), row(LANES), row(IDX_K),
                      pl.BlockSpec(memory_space=pl.ANY)],
            out_specs=row(n_past + LANES),
            scratch_shapes=[pltpu.VMEM((2, IDX_DIM, n_past), f32),
                            pltpu.SemaphoreType.DMA((2,))],
        ),
        out_shape=jax.ShapeDtypeStruct((nb * t, n_past + LANES), f32),
        compiler_params=_params(1),
        name="dsa_sample_scores",
    )(page_table, qic, kiwi, kic_new, cache_ki_t)


def _sample_select_kernel(sc_ref, tri_ref, before_ref, o_ref, score_scr, *, k_top, t, n_past):
    rows, n = sc_ref.shape
    qi = lax.broadcasted_iota(jnp.int32, (rows, n), 0) & (t - 1)
    kj = lax.broadcasted_iota(jnp.int32, (rows, n), 1)
    admissible = kj <= n_past + qi
    score_scr[...] = jnp.where(admissible, sc_ref[...], NEG_INF)

    def store(c, piece):
        o_ref[:, c * LANES:(c + 1) * LANES] = piece

    _topk_bias(score_scr, tri_ref, before_ref, k_top, store)


def _sample_select(scores, t, n_past, rows_per_step):
    n_rows, n_all = scores.shape
    assert t & (t - 1) == 0 and rows_per_step % t == 0, (t, rows_per_step)
    k_top = min(TOPK_MAX, (n_past + t) // 4)
    tri, before = _tie_consts(n_all)
    block = pl.BlockSpec((rows_per_step, n_all), lambda i: (i, 0))
    return pl.pallas_call(
        functools.partial(_sample_select_kernel, k_top=k_top, t=t, n_past=n_past),
        grid=(n_rows // rows_per_step,),
        in_specs=[block, _const_spec((LANES, LANES)), _const_spec((n_all, LANES))],
        out_specs=block,
        out_shape=jax.ShapeDtypeStruct((n_rows, n_all), f32),
        scratch_shapes=[pltpu.VMEM((rows_per_step, n_all), f32)],
        compiler_params=_params(1),
        name="dsa_sample_select",
    )(scores, tri, before)


def _sample_attend_kernel(pt_ref, bias_ref, biasn_ref, q_ref, kbn_ref, vbn_ref, x_ref, wo_ref,
                          ck_ref, cv_ref, o_ref, kbuf, vbuf, ksem, vsem, m_scr, l_scr, acc_scr,
                          wo_b, *, pps):
    b, j = pl.program_id(0), pl.program_id(1)
    n_j = pl.num_programs(1)
    step = b * n_j + j
    _cast_weights_once(step == 0, [(wo_ref, wo_b)])
    t = q_ref.shape[0]
    page_rows = PAGE_SIZE * N_KV_HEADS

    def page_copies(batch, chunk, slot):
        copies = []
        for i in range(pps):
            src = pl.ds(pl.multiple_of(pt_ref[batch, chunk * pps + i] * page_rows, page_rows),
                        page_rows)
            dst = pl.ds(i * page_rows, page_rows)
            copies.append(pltpu.make_async_copy(ck_ref.at[src, :], kbuf.at[slot, dst, :], ksem.at[slot]))
            copies.append(pltpu.make_async_copy(cv_ref.at[src, :], vbuf.at[slot, dst, :], vsem.at[slot]))
        return copies

    slot = lax.rem(step, 2)

    @pl.when(step == 0)
    def _():
        for copy in page_copies(0, 0, 0):
            copy.start()

    @pl.when(step + 1 < pl.num_programs(0) * n_j)
    def _():
        nxt = step + 1
        for copy in page_copies(nxt // n_j, lax.rem(nxt, n_j), 1 - slot):
            copy.start()

    @pl.when(j == 0)
    def _():
        m_scr[...] = jnp.full(m_scr.shape, -1e30, f32)
        l_scr[...] = jnp.zeros(l_scr.shape, f32)
        acc_scr[...] = jnp.zeros(acc_scr.shape, f32)

    q = q_ref[...]

    rows = GROUP * t

    def attend(k_heads, v_heads, bias):
        s = jnp.concatenate(
            [_dot_nt(jnp.concatenate(
                [q[:, (kvh * GROUP + g) * HEAD_DIM:(kvh * GROUP + g + 1) * HEAD_DIM]
                 for g in range(GROUP)], axis=0), k_heads[kvh]) for kvh in range(N_KV_HEADS)],
            axis=0)
        s = s + jnp.concatenate([bias] * N_HEADS, axis=0)
        m_old = m_scr[...]
        m_new = jnp.maximum(m_old, jnp.max(s, axis=-1, keepdims=True))
        alpha = jnp.exp2(m_old - m_new)
        p = jnp.exp2(s - m_new)
        l_scr[...] = alpha * l_scr[...] + jnp.sum(p, axis=-1, keepdims=True)
        pv = jnp.concatenate(
            [_dot(p[kvh * rows:(kvh + 1) * rows].astype(bf16), v_heads[kvh])
             for kvh in range(N_KV_HEADS)], axis=0)
        acc_scr[...] = alpha * acc_scr[...] + pv
        m_scr[...] = m_new

    for copy in page_copies(b, j, slot):
        copy.wait()

    def head_rows(buf, kvh):
        return buf[slot, pl.ds(kvh, pps * PAGE_SIZE, stride=N_KV_HEADS), :].astype(bf16)

    attend([head_rows(kbuf, kvh) for kvh in range(N_KV_HEADS)],
           [head_rows(vbuf, kvh) for kvh in range(N_KV_HEADS)], bias_ref[...])

    @pl.when(j == n_j - 1)
    def _():
        pad = jnp.zeros((LANES - t, HEAD_DIM), bf16)

        def new_heads(ref):
            return [jnp.concatenate([ref[:, kvh * HEAD_DIM:(kvh + 1) * HEAD_DIM], pad], axis=0)
                    for kvh in range(N_KV_HEADS)]

        attend(new_heads(kbn_ref), new_heads(vbn_ref), biasn_ref[...])
        o = acc_scr[...] / l_scr[...]
        att = jnp.concatenate([o[h * t:(h + 1) * t] for h in range(N_HEADS)],
                              axis=1).astype(bf16)
        o_ref[...] = x_ref[...] + _dot(att, wo_b[...])


def _sample_attend(page_table, bias, q, kb_new, vb_new, x, wo, cache_k, cache_v, pages_per_step):
    nb, n_pages = page_table.shape
    t = q.shape[0] // nb
    steps = n_pages // pages_per_step
    span = pages_per_step * PAGE_SIZE
    row = lambda width: pl.BlockSpec((t, width), lambda b, j, pt: (b, 0))
    rows = N_HEADS * t
    buf = pltpu.VMEM((2, span * N_KV_HEADS, HEAD_DIM), f32)
    return pl.pallas_call(
        functools.partial(_sample_attend_kernel, pps=pages_per_step),
        grid_spec=pltpu.PrefetchScalarGridSpec(
            num_scalar_prefetch=1,
            grid=(nb, steps),
            in_specs=[pl.BlockSpec((t, span), lambda b, j, pt: (b, j)),
                      pl.BlockSpec((t, LANES), lambda b, j, pt: (b, n_pages)),
                      row(Q_W), row(KV_W), row(KV_W), row(D_MODEL),
                      _weight_spec((Q_W, D_MODEL)),
                      pl.BlockSpec(memory_space=pl.ANY), pl.BlockSpec(memory_space=pl.ANY)],
            out_specs=row(D_MODEL),
            scratch_shapes=[buf, buf, pltpu.SemaphoreType.DMA((2,)), pltpu.SemaphoreType.DMA((2,)),
                            pltpu.VMEM((rows, 1), f32),
                            pltpu.VMEM((rows, 1), f32),
                            pltpu.VMEM((rows, HEAD_DIM), f32),
                            pltpu.VMEM((Q_W, D_MODEL), bf16)],
        ),
        out_shape=jax.ShapeDtypeStruct((nb * t, D_MODEL), f32),
        compiler_params=_params(2),
        name="dsa_sample_attend",
    )(page_table, bias, bias, q, kb_new, vb_new, x, wo, cache_k, cache_v)


def _ffn_kernel(x_ref, g_ref, wgu_ref, wd_ref, o_ref, *, n_ff):
    x = x_ref[...]
    h = _rms(x, g_ref[...]).astype(bf16)
    acc = x
    width = D_FF // n_ff
    for c in range(n_ff):
        gate = _dot(h, wgu_ref[0, :, c * width:(c + 1) * width])
        up = _dot(h, wgu_ref[0, :, D_FF + c * width:D_FF + (c + 1) * width])
        acc = acc + _dot((jax.nn.silu(gate) * up).astype(bf16),
                         wd_ref[0, c * width:(c + 1) * width, :])
    o_ref[...] = acc


def _ffn(x, g, wgu, wd, layer, tm, n_ff):
    t = x.shape[0]
    row = pl.BlockSpec((tm, D_MODEL), lambda i: (i, 0))
    resident = lambda a: pl.BlockSpec((1,) + a.shape[1:], lambda i: (layer, 0, 0),
                                      pipeline_mode=pl.Buffered(1))
    return pl.pallas_call(
        functools.partial(_ffn_kernel, n_ff=n_ff),
        grid=(t // tm,),
        in_specs=[row, _const_spec((1, D_MODEL)), resident(wgu), resident(wd)],
        out_specs=row,
        out_shape=jax.ShapeDtypeStruct((t, D_MODEL), f32),
        compiler_params=_params(1),
        name="ffn",
    )(x, g, wgu, wd)


def _sgu_kernel(x_ref, g_ref, win_ref, gv_ref, ws_ref, bias_ref, wout_ref, *refs,
                chunk_len, emit_v):
    if emit_v:
        o_ref, v_ref, gated_scr, win_b, wout_b = refs
    else:
        o_ref, gated_scr, win_b, wout_b = refs
    _cast_weights_once(pl.program_id(0) == 0, [(win_ref, win_b), (wout_ref, wout_b)])
    tm = x_ref.shape[0]
    n_sub = tm // CHUNK
    x = x_ref[...]
    h = _rms(x, g_ref[...]).astype(bf16)
    z = jax.nn.gelu(_dot(h, win_b[...]))
    u = z[:, :D_MODEL]
    v = _rms(z[:, D_MODEL:], gv_ref[...])
    if emit_v:
        v_ref[...] = v
    vb = v.astype(bf16)
    ti = lax.broadcasted_iota(jnp.int32, (CHUNK, CHUNK), 0)
    si = lax.broadcasted_iota(jnp.int32, (CHUNK, CHUNK), 1)
    shift = chunk_len.bit_length() - 1
    causal = (si <= ti) & (lax.shift_right_logical(si, shift) == lax.shift_right_logical(ti, shift))
    bias = bias_ref[...]
    for grp in range(SG_GROUPS):
        cols = slice(grp * LANES, (grp + 1) * LANES)
        w = jnp.where(causal, ws_ref[grp], 0.0).astype(bf16)
        vg = jnp.concatenate([vb[c * CHUNK:(c + 1) * CHUNK, cols] for c in range(n_sub)], axis=1)
        mixed = _dot(w, vg) + bias[:, grp:grp + 1]
        for c in range(n_sub):
            rows = slice(c * CHUNK, (c + 1) * CHUNK)
            gated_scr[rows, cols] = (u[rows, cols] * mixed[:, c * LANES:(c + 1) * LANES]).astype(bf16)
    o_ref[...] = x + _dot(gated_scr[...], wout_b[...])


def _sgu(x, g, w_in, g_v, w_s, bias_t, w_out, tm, chunk_len, emit_v):
    t = x.shape[0]
    row = pl.BlockSpec((tm, D_MODEL), lambda i: (i, 0))
    out_shape = [jax.ShapeDtypeStruct((t, D_MODEL), f32)] * (2 if emit_v else 1)
    return pl.pallas_call(
        functools.partial(_sgu_kernel, chunk_len=chunk_len, emit_v=emit_v),
        grid=(t // tm,),
        in_specs=[row, _const_spec((1, D_MODEL)), _weight_spec(w_in.shape),
                  _const_spec((1, D_MODEL)), _const_spec(w_s.shape), _const_spec(bias_t.shape),
                  _weight_spec(w_out.shape)],
        out_specs=[row] * len(out_shape),
        out_shape=out_shape,
        scratch_shapes=[pltpu.VMEM((tm, D_MODEL), bf16), pltpu.VMEM(w_in.shape, bf16),
                        pltpu.VMEM(w_out.shape, bf16)],
        compiler_params=_params(1),
        name="sgu",
    )(x, g, w_in, g_v, w_s, bias_t, w_out)


def kernel(x_prompt, x_sample, cache_k, cache_v, cache_kidx, page_table, norm_mix, norm_ffn,
           att_w_in, att_q_norm, att_k_norm, att_w_out, sg_w_in, sg_v_norm, sg_w_s, sg_b_s,
           sg_w_out, ffn_w_gu, ffn_w_down):
    nb_p, t_p, _ = x_prompt.shape
    nb_s, t_s, _ = x_sample.shape
    n_pool = cache_k.shape[1]
    xp = x_prompt.reshape(nb_p * t_p, D_MODEL)
    xs = x_sample.reshape(nb_s * t_s, D_MODEL)
    vec = lambda a: a.reshape(1, -1)

    w_out = att_w_out[0]
    proj = functools.partial(_att_proj, g=vec(norm_mix[0]), w=att_w_in[0],
                             gq=vec(att_q_norm[0]), gk=vec(att_k_norm[0]))
    q_p, k_p, v_p, kb_p, _, qic_p, kic_p, ki_p, kiwi_p, vt_p = proj(xp, tm=512)
    q_s, k_s, v_s, kb_s, vb_s, qic_s, kic_s, ki_s, kiwi_s, _ = proj(xs, tm=nb_s * t_s)

    per_seq = lambda a: a.reshape(nb_p, t_p, a.shape[-1])
    xp = _dsa_prompt(per_seq(q_p), per_seq(qic_p), per_seq(kiwi_p), per_seq(kic_p),
                     per_seq(kb_p), vt_p, per_seq(xp), w_out, tq=128, span=512)
    xp = xp.reshape(nb_p * t_p, D_MODEL)

    scores_s = _sample_scores(page_table, qic_s, kiwi_s, kic_s, jnp.swapaxes(cache_kidx[0], 1, 2))
    bias_s = _sample_select(scores_s, t_s, page_table.shape[1] * PAGE_SIZE, rows_per_step=64)
    kv_rows = n_pool * PAGE_SIZE * N_KV_HEADS
    xs = _sample_attend(page_table, bias_s, q_s, kb_s, vb_s, xs, w_out,
                        cache_k[0].reshape(kv_rows, HEAD_DIM),
                        cache_v[0].reshape(kv_rows, HEAD_DIM), pages_per_step=64)

    ffn_gu, ffn_down = ffn_w_gu.astype(bf16), ffn_w_down.astype(bf16)

    def ffn(x, layer, tm):
        return _ffn(x, vec(norm_ffn[layer]), ffn_gu, ffn_down, layer, tm, n_ff=11)

    xp = ffn(xp, 0, 512)
    xs = ffn(xs, 0, nb_s * t_s)

    sgu = functools.partial(_sgu, g=vec(norm_mix[1]), w_in=sg_w_in[0], g_v=vec(sg_v_norm[0]),
                            w_out=sg_w_out[0])
    (xp,) = sgu(xp, w_s=sg_w_s[0], bias_t=sg_b_s[0].T, tm=512, chunk_len=CHUNK, emit_v=False)
    reps = CHUNK // t_s
    xs, sgv = sgu(xs, w_s=jnp.tile(sg_w_s[0][:, :t_s, :t_s], (1, reps, reps)),
                  bias_t=jnp.tile(sg_b_s[0][:, :t_s], (1, reps)).T, tm=nb_s * t_s,
                  chunk_len=t_s, emit_v=True)

    xp = ffn(xp, 1, 512)
    xs = ffn(xs, 1, nb_s * t_s)

    y_prompt = xp.reshape(nb_p, t_p, D_MODEL)
    y_sample = xs.reshape(nb_s, t_s, D_MODEL)
    kv_p = lambda a: a.reshape(1, nb_p, t_p, N_KV_HEADS, HEAD_DIM)
    kv_s = lambda a: a.reshape(1, nb_s, t_s, N_KV_HEADS, HEAD_DIM)
    return (y_prompt, y_sample, kv_p(k_p), kv_p(v_p), ki_p.reshape(1, nb_p, t_p, IDX_DIM),
            kv_s(k_s), kv_s(v_s), ki_s.reshape(1, nb_s, t_s, IDX_DIM),
            sgv.reshape(1, nb_s, t_s, D_MODEL))
```

```python
import functools

import jax
import jax.numpy as jnp
from jax import lax
from jax.experimental import pallas as pl
from jax.experimental.pallas import tpu as pltpu

D_MODEL = 1024
HEAD_DIM = 128
N_HEADS = 8
N_KV_HEADS = 2
GROUP = N_HEADS // N_KV_HEADS
IDX_HEADS = 4
IDX_DIM = 64
IDX_SCALE = (IDX_DIM ** -0.5) * (IDX_HEADS ** -0.5)
TOPK_MAX = 256
CHUNK = 128
SG_GROUPS = 8
D_FF = 2816
PAGE_SIZE = 128
EPS = 1e-6

Q_W = N_HEADS * HEAD_DIM
KV_W = N_KV_HEADS * HEAD_DIM
QI_W = IDX_HEADS * IDX_DIM
ATT_IN = Q_W + 2 * KV_W + QI_W + IDX_DIM + IDX_HEADS
LANES = 128
SUBLANES = 8
BF16_ROWS = 16
SOFTMAX2_SCALE = (HEAD_DIM ** -0.5) * 1.4426950408889634
ATT_IN_PAD = -(-ATT_IN // LANES) * LANES
IDX_K = 4 * IDX_DIM
INT_MIN = -(2 ** 31)
NEG_INF = float("-inf")
VMEM_LIMIT = 56 * 1024 * 1024

bf16 = jnp.bfloat16
f32 = jnp.float32


def _rms(x, g):
    return x * lax.rsqrt(jnp.mean(x * x, axis=-1, keepdims=True) + EPS) * g


def _dot(a, b):
    return jnp.dot(a, b, preferred_element_type=f32)


def _dot_nt(a, b):
    return lax.dot_general(a, b, (((1,), (1,)), ((), ())), preferred_element_type=f32)


def _hi(x):
    return x.astype(bf16).astype(f32)


def _params(n_grid):
    return pltpu.CompilerParams(
        dimension_semantics=("arbitrary",) * n_grid, vmem_limit_bytes=VMEM_LIMIT)


def _const_spec(shape):
    nd = len(shape)
    return pl.BlockSpec(shape, lambda *_: (0,) * nd)


def _weight_spec(shape):
    nd = len(shape)
    return pl.BlockSpec(shape, lambda *_: (0,) * nd, pipeline_mode=pl.Buffered(1))


def _cast_weights_once(is_first_step, pairs):
    @pl.when(is_first_step)
    def _():
        for src, dst in pairs:
            dst[...] = src[...].astype(dst.dtype)


def _att_proj_kernel(x_ref, g_ref, w_ref, gq_ref, gk_ref,
                     q_ref, k_ref, v_ref, kb_ref, vb_ref, qic_ref, kic_ref, ki_ref, kiwi_ref, vt_ref,
                     w_b):
    @pl.when(pl.program_id(0) == 0)
    def _():
        w_b[:, ATT_IN_PAD - LANES:] = jnp.zeros((D_MODEL, LANES), bf16)
        w_b[:, :ATT_IN] = w_ref[...].astype(bf16)

    tm = x_ref.shape[0]
    h = _rms(x_ref[...], g_ref[...]).astype(bf16)
    z = _dot(h, w_b[...])
    gq = gq_ref[...]
    gk = gk_ref[...]
    for j in range(N_HEADS):
        sl = slice(j * HEAD_DIM, (j + 1) * HEAD_DIM)
        q_ref[:, sl] = (_rms(z[:, sl], gq) * SOFTMAX2_SCALE).astype(bf16)
    v = z[:, Q_W + KV_W:Q_W + 2 * KV_W]
    for j in range(N_KV_HEADS):
        sl = slice(j * HEAD_DIM, (j + 1) * HEAD_DIM)
        kj = _rms(z[:, Q_W + j * HEAD_DIM:Q_W + (j + 1) * HEAD_DIM], gk)
        kb_ref[:, sl] = kj.astype(bf16)
        head_rows = pl.ds(j, tm, stride=N_KV_HEADS)
        k_ref[head_rows, :] = kj
        v_ref[head_rows, :] = v[:, sl]
    vb_ref[...] = v.astype(bf16)
    vt_ref[...] = v.T.astype(bf16)

    low_half = lax.broadcasted_iota(jnp.int32, (tm, LANES), 1) < IDX_DIM
    qi0 = Q_W + 2 * KV_W
    for p in range(IDX_HEADS // 2):
        pair = z[:, qi0 + p * LANES:qi0 + (p + 1) * LANES]
        rolled = pltpu.roll(pair, IDX_DIM, axis=1)
        for e, (own, other) in enumerate(((pair, rolled), (rolled, pair))):
            dup = jnp.where(low_half, own, other)
            single = jnp.where(low_half, own, 0.0)
            base = (2 * p + e) * IDX_K
            qic_ref[:, base:base + LANES] = dup.astype(bf16)
            qic_ref[:, base + LANES:base + 2 * LANES] = (single - _hi(single)).astype(bf16)
    last = z[:, qi0 + QI_W:qi0 + QI_W + LANES]
    rolled = pltpu.roll(last, IDX_DIM, axis=1)
    kic_ref[:, :LANES] = jnp.where(low_half, _hi(last), rolled - _hi(rolled)).astype(bf16)
    kic_ref[:, LANES:] = jnp.where(low_half, _hi(last), 0.0).astype(bf16)
    ki_ref[...] = last[:, :IDX_DIM]
    kiwi_ref[...] = last


def _att_proj(x, g, w, gq, gk, tm):
    t = x.shape[0]
    row = lambda width: pl.BlockSpec((tm, width), lambda i: (i, 0))
    head_rows = pl.BlockSpec((N_KV_HEADS * tm, HEAD_DIM), lambda i: (i, 0))
    head_rows_shape = jax.ShapeDtypeStruct((N_KV_HEADS * t, HEAD_DIM), f32)
    outs = [(Q_W, bf16), None, None, (KV_W, bf16), (KV_W, bf16),
            (IDX_HEADS * IDX_K, bf16), (IDX_K, bf16), (IDX_DIM, f32), (LANES, f32)]
    return pl.pallas_call(
        _att_proj_kernel,
        grid=(t // tm,),
        in_specs=[row(D_MODEL), _const_spec((1, D_MODEL)), _weight_spec((D_MODEL, ATT_IN)),
                  _const_spec((1, HEAD_DIM)), _const_spec((1, HEAD_DIM))],
        out_specs=[row(o[0]) if o else head_rows for o in outs]
        + [pl.BlockSpec((KV_W, tm), lambda i: (0, i))],
        out_shape=[jax.ShapeDtypeStruct((t, o[0]), o[1]) if o else head_rows_shape for o in outs]
        + [jax.ShapeDtypeStruct((KV_W, t), bf16)],
        scratch_shapes=[pltpu.VMEM((D_MODEL, ATT_IN_PAD), bf16)],
        compiler_params=_params(1),
        name="att_proj",
    )(x, g, w, gq, gk)


def _key_to_float(key):
    bits = jnp.where(key < 0, key ^ jnp.int32(0x7FFFFFFF), key)
    return lax.bitcast_convert_type(bits, f32)


def _count(mask, axis=1):
    ones = jnp.where(mask, 1.0, 0.0)
    if axis == 0:
        n, cols = ones.shape
        rows = 4 * SUBLANES
        ones = jnp.sum(ones.reshape(n // rows, rows, cols), axis=0)
    return jnp.sum(ones, axis=axis, keepdims=True)


def _rank_threshold(score_ref, k_top, axis):
    shape = tuple(1 if a == axis else d for a, d in enumerate(score_ref.shape))

    def step(i, t):
        cand = t + lax.shift_left(jnp.int32(1), jnp.int32(31) - i)
        enough = _count(score_ref[...] >= _key_to_float(cand), axis) >= k_top
        return jnp.where(enough, cand, t)

    key = lax.fori_loop(0, 32, step, jnp.full(shape, INT_MIN, jnp.int32))
    t = _key_to_float(key)
    return t, jnp.where(key == INT_MIN, NEG_INF, t)


def _topk_bias_keys_major(score_ref, bias_ref, tril_ref, k_top):
    n, _ = score_ref.shape
    t, t_above = _rank_threshold(score_ref, k_top, axis=0)
    need = k_top - _count(score_ref[...] > t_above, axis=0)
    ties_before = jnp.zeros(t.shape, f32)
    for c in range(n // LANES):
        sl = slice(c * LANES, (c + 1) * LANES)
        s_c = score_ref[sl, :]
        tie = s_c == t
        tie_f = jnp.where(tie, 1.0, 0.0)
        rank = _dot(tril_ref[...], tie_f.astype(bf16)) + ties_before
        ties_before = ties_before + jnp.sum(tie_f, axis=0, keepdims=True)
        bias_ref[sl, :] = jnp.where((s_c > t_above) | (tie & (rank <= need)), 0.0, NEG_INF)


def _topk_bias(score_ref, tri_ref, before_ref, k_top, store):
    _, n = score_ref.shape
    t, t_above = _rank_threshold(score_ref, k_top, axis=1)
    s = score_ref[...]
    need = k_top - _count(s > t_above)
    tie_b = jnp.where(s == t, 1.0, 0.0).astype(bf16)
    before = _dot(tie_b, before_ref[:n, :])
    for c in range(n // LANES):
        sl = slice(c * LANES, (c + 1) * LANES)
        s_c = s[:, sl]
        rank = _dot(tie_b[:, sl], tri_ref[...]) + before[:, c:c + 1]
        store(c, jnp.where((s_c > t_above) | ((s_c == t) & (rank <= need)), 0.0, NEG_INF))


def _tie_consts(n):
    a = jnp.arange(LANES)
    tri = (a[:, None] <= a[None, :]).astype(bf16)
    before = ((jnp.arange(n)[:, None] // LANES) < a[None, :]).astype(bf16)
    return tri, before


def _indexer_scores(qic, wi, kic):
    rows = qic.shape[0]
    heads = [qic[:, h * IDX_K:(h + 1) * IDX_K] for h in range(IDX_HEADS)]
    if rows % BF16_ROWS == 0:
        logits = _dot_nt(jnp.concatenate(heads, axis=0), kic)
        logits = [logits[h * rows:(h + 1) * rows] for h in range(IDX_HEADS)]
    else:
        logits = [_dot_nt(qh, kic) for qh in heads]
    score = None
    for h in range(IDX_HEADS):
        relu = jnp.maximum(logits[h], 0.0)
        term = wi[:, IDX_DIM + h:IDX_DIM + h + 1] * relu
        score = term if score is None else score + term
    return score * IDX_SCALE


def _dsa_prompt_tile(lk, q0, k_top, q_ref, qic_ref, kiwi_ref, kic_ref, kb_ref, vt_ref, tril_ref,
                     score_scr, bias_scr, s_scr, p_scr, att_scr):
    tq = q_ref.shape[1]
    qic = qic_ref[0]
    q_idx = jnp.concatenate(
        [qic[:, h * IDX_K:(h + 1) * IDX_K] for h in range(IDX_HEADS)], axis=0)
    logits = _dot_nt(kic_ref[0, :lk, :], q_idx)
    wi_t = kiwi_ref[0].T
    score = None
    for h in range(IDX_HEADS):
        term = wi_t[IDX_DIM + h:IDX_DIM + h + 1, :] * jnp.maximum(logits[:, h * tq:(h + 1) * tq], 0.0)
        score = term if score is None else score + term
    score = score * IDX_SCALE
    q_pos = q0 + lax.broadcasted_iota(jnp.int32, (1, tq), 1)
    admissible = lax.broadcasted_iota(jnp.int32, (lk, 1), 0) <= q_pos
    scores = score_scr.at[:lk, :]
    scores[...] = jnp.where(admissible, score, NEG_INF)

    bias_ref = bias_scr.at[:lk, :]
    _topk_bias_keys_major(scores, bias_ref, tril_ref, k_top)

    q = q_ref[0]
    bias = jnp.concatenate([bias_ref[...]] * GROUP, axis=1)
    for kvh in range(N_KV_HEADS):
        qs = jnp.concatenate(
            [q[:, (kvh * GROUP + g) * HEAD_DIM:(kvh * GROUP + g + 1) * HEAD_DIM]
             for g in range(GROUP)], axis=0)
        kv_sl = slice(kvh * HEAD_DIM, (kvh + 1) * HEAD_DIM)
        s_scr[kvh, :lk, :] = _dot_nt(kb_ref[0, :lk, kv_sl], qs) + bias

    for kvh in range(N_KV_HEADS):
        kv_sl = slice(kvh * HEAD_DIM, (kvh + 1) * HEAD_DIM)
        denoms = []
        for g in range(GROUP):
            cols = slice(g * tq, (g + 1) * tq)
            m = jnp.max(s_scr[kvh, :lk, cols], axis=0, keepdims=True)
            p = jnp.exp2(s_scr[kvh, :lk, cols] - m)
            denoms.append(jnp.sum(p, axis=0, keepdims=True))
            p_scr[kvh, :lk, cols] = p.astype(bf16)
        o_t = _dot(vt_ref[kv_sl, :lk], p_scr[kvh, :lk, :])
        o_t = o_t / jnp.concatenate(denoms, axis=1)
        for g in range(GROUP):
            col = (kvh * GROUP + g) * HEAD_DIM
            att_scr[:, col:col + HEAD_DIM] = o_t[:, g * tq:(g + 1) * tq].T.astype(bf16)


def _dsa_prompt_kernel(q_ref, qic_ref, kiwi_ref, kic_ref, kb_ref, vt_ref, x_ref, wo_ref,
                       tril_ref, o_ref, wo_b, *scratch, k_top, span):
    _cast_weights_once((pl.program_id(0) == 0) & (pl.program_id(1) == 0), [(wo_ref, wo_b)])
    tq = q_ref.shape[1]
    n = kb_ref.shape[1]
    q0 = pl.program_id(1) * tq
    span_class = (q0 + tq - 1) // span
    for c in range(n // span):
        @pl.when(span_class == c)
        def _(lk=(c + 1) * span):
            _dsa_prompt_tile(lk, q0, k_top, q_ref, qic_ref, kiwi_ref, kic_ref, kb_ref, vt_ref,
                             tril_ref, *scratch)
    att_scr = scratch[-1]
    o_ref[0] = x_ref[0] + _dot(att_scr[...], wo_b[...])


def _dsa_prompt(q, qic, kiwi, kic, kb, vt, x, wo, tq, span):
    b, t, _ = q.shape
    k_top = min(TOPK_MAX, t // 4)
    a = jnp.arange(LANES)
    tril = (a[None, :] <= a[:, None]).astype(bf16)
    cols = GROUP * tq
    tile = lambda width: pl.BlockSpec((1, tq, width), lambda i, j: (i, j, 0))
    full = lambda width: pl.BlockSpec((1, t, width), lambda i, j: (i, 0, 0))
    return pl.pallas_call(
        functools.partial(_dsa_prompt_kernel, k_top=k_top, span=span),
        grid=(b, t // tq),
        in_specs=[tile(Q_W), tile(IDX_HEADS * IDX_K), tile(LANES), full(IDX_K), full(KV_W),
                  pl.BlockSpec((KV_W, t), lambda i, j: (0, i)), tile(D_MODEL),
                  _weight_spec((Q_W, D_MODEL)), _const_spec((LANES, LANES))],
        out_specs=tile(D_MODEL),
        out_shape=jax.ShapeDtypeStruct((b, t, D_MODEL), f32),
        scratch_shapes=[pltpu.VMEM((Q_W, D_MODEL), bf16),
                        pltpu.VMEM((t, tq), f32),
                        pltpu.VMEM((t, tq), f32),
                        pltpu.VMEM((N_KV_HEADS, t, cols), f32),
                        pltpu.VMEM((N_KV_HEADS, t, cols), bf16),
                        pltpu.VMEM((tq, Q_W), bf16)],
        compiler_params=_params(2),
        name="dsa_prompt",
    )(q, qic, kiwi, kic, kb, vt, x, wo, tril)


def _sample_scores_kernel(pt_ref, qic_ref, kiwi_ref, kicn_ref, cache_ref, o_ref, buf, sem):
    b = pl.program_id(0)
    n_pages = pt_ref.shape[1]
    n_past = n_pages * PAGE_SIZE
    t = qic_ref.shape[0]

    def page_copies(batch, slot):
        return [pltpu.make_async_copy(
            cache_ref.at[pt_ref[batch, p]],
            buf.at[slot, :, pl.ds(p * PAGE_SIZE, PAGE_SIZE)], sem.at[slot])
            for p in range(n_pages)]

    slot = lax.rem(b, 2)

    @pl.when(b == 0)
    def _():
        for copy in page_copies(0, 0):
            copy.start()

    @pl.when(b + 1 < pl.num_programs(0))
    def _():
        for copy in page_copies(b + 1, 1 - slot):
            copy.start()

    for copy in page_copies(b, slot):
        copy.wait()

    keys_t = buf[slot]
    k_hi = keys_t.astype(bf16)
    k_lo = (keys_t - k_hi.astype(f32)).astype(bf16)
    qic = qic_ref[...].astype(f32)

    def stacked(offset):
        return jnp.concatenate(
            [qic[:, h * IDX_K + offset:h * IDX_K + offset + IDX_DIM] for h in range(IDX_HEADS)],
            axis=0).astype(bf16)

    q_hi, q_lo = stacked(0), stacked(2 * IDX_DIM)
    logits = _dot(q_hi, k_hi) + _dot(q_hi, k_lo) + _dot(q_lo, k_hi)
    wi = kiwi_ref[...]
    score = None
    for h in range(IDX_HEADS):
        term = wi[:, IDX_DIM + h:IDX_DIM + h + 1] * jnp.maximum(logits[h * t:(h + 1) * t], 0.0)
        score = term if score is None else score + term
    o_ref[:, :n_past] = score * IDX_SCALE
    new = _indexer_scores(qic_ref[...], wi, kicn_ref[...])
    o_ref[:, n_past:] = jnp.concatenate([new, jnp.zeros((t, LANES - t), f32)], axis=1)


def _sample_scores(page_table, qic, kiwi, kic_new, cache_ki_t):
    nb, n_pages = page_table.shape
    t = qic.shape[0] // nb
    n_past = n_pages * PAGE_SIZE
    row = lambda width: pl.BlockSpec((t, width), lambda b, pt: (b, 0))
    return pl.pallas_call(
        _sample_scores_kernel,
        grid_spec=pltpu.PrefetchScalarGridSpec(
            num_scalar_prefetch=1,
            grid=(nb,),
            in_specs=[row(IDX_HEADS * IDX_K), row(LANES), row(IDX_K),
                      pl.BlockSpec(memory_space=pl.ANY)],
            out_specs=row(n_past + LANES),
            scratch_shapes=[pltpu.VMEM((2, IDX_DIM, n_past), f32),
                            pltpu.SemaphoreType.DMA((2,))],
        ),
        out_shape=jax.ShapeDtypeStruct((nb * t, n_past + LANES), f32),
        compiler_params=_params(1),
        name="dsa_sample_scores",
    )(page_table, qic, kiwi, kic_new, cache_ki_t)


def _sample_select_kernel(sc_ref, tri_ref, before_ref, o_ref, score_scr, *, k_top, t, n_past):
    rows, n = sc_ref.shape
    qi = lax.broadcasted_iota(jnp.int32, (rows, n), 0) & (t - 1)
    kj = lax.broadcasted_iota(jnp.int32, (rows, n), 1)
    admissible = kj <= n_past + qi
    score_scr[...] = jnp.where(admissible, sc_ref[...], NEG_INF)

    def store(c, piece):
        o_ref[:, c * LANES:(c + 1) * LANES] = piece

    _topk_bias(score_scr, tri_ref, before_ref, k_top, store)


def _sample_select(scores, t, n_past, rows_per_step):
    n_rows, n_all = scores.shape
    assert t & (t - 1) == 0 and rows_per_step % t == 0, (t, rows_per_step)
    k_top = min(TOPK_MAX, (n_past + t) // 4)
    tri, before = _tie_consts(n_all)
    block = pl.BlockSpec((rows_per_step, n_all), lambda i: (i, 0))
    return pl.pallas_call(
        functools.partial(_sample_select_kernel, k_top=k_top, t=t, n_past=n_past),
        grid=(n_rows // rows_per_step,),
        in_specs=[block, _const_spec((LANES, LANES)), _const_spec((n_all, LANES))],
        out_specs=block,
        out_shape=jax.ShapeDtypeStruct((n_rows, n_all), f32),
        scratch_shapes=[pltpu.VMEM((rows_per_step, n_all), f32)],
        compiler_params=_params(1),
        name="dsa_sample_select",
    )(scores, tri, before)


def _sample_attend_kernel(pt_ref, bias_ref, biasn_ref, q_ref, kbn_ref, vbn_ref, x_ref, wo_ref,
                          ck_ref, cv_ref, o_ref, kbuf, vbuf, ksem, vsem, m_scr, l_scr, acc_scr,
                          wo_b, *, pps):
    b, j = pl.program_id(0), pl.program_id(1)
    n_j = pl.num_programs(1)
    step = b * n_j + j
    _cast_weights_once(step == 0, [(wo_ref, wo_b)])
    t = q_ref.shape[0]
    page_rows = PAGE_SIZE * N_KV_HEADS

    def page_copies(batch, chunk, slot):
        copies = []
        for i in range(pps):
            src = pl.ds(pl.multiple_of(pt_ref[batch, chunk * pps + i] * page_rows, page_rows),
                        page_rows)
            dst = pl.ds(i * page_rows, page_rows)
            copies.append(pltpu.make_async_copy(ck_ref.at[src, :], kbuf.at[slot, dst, :], ksem.at[slot]))
            copies.append(pltpu.make_async_copy(cv_ref.at[src, :], vbuf.at[slot, dst, :], vsem.at[slot]))
        return copies

    slot = lax.rem(step, 2)

    @pl.when(step == 0)
    def _():
        for copy in page_copies(0, 0, 0):
            copy.start()

    @pl.when(step + 1 < pl.num_programs(0) * n_j)
    def _():
        nxt = step + 1
        for copy in page_copies(nxt // n_j, lax.rem(nxt, n_j), 1 - slot):
            copy.start()

    @pl.when(j == 0)
    def _():
        m_scr[...] = jnp.full(m_scr.shape, -1e30, f32)
        l_scr[...] = jnp.zeros(l_scr.shape, f32)
        acc_scr[...] = jnp.zeros(acc_scr.shape, f32)

    q = q_ref[...]

    rows = GROUP * t

    def attend(k_heads, v_heads, bias):
        s = jnp.concatenate(
            [_dot_nt(jnp.concatenate(
                [q[:, (kvh * GROUP + g) * HEAD_DIM:(kvh * GROUP + g + 1) * HEAD_DIM]
                 for g in range(GROUP)], axis=0), k_heads[kvh]) for kvh in range(N_KV_HEADS)],
            axis=0)
        s = s + jnp.concatenate([bias] * N_HEADS, axis=0)
        m_old = m_scr[...]
        m_new = jnp.maximum(m_old, jnp.max(s, axis=-1, keepdims=True))
        alpha = jnp.exp2(m_old - m_new)
        p = jnp.exp2(s - m_new)
        l_scr[...] = alpha * l_scr[...] + jnp.sum(p, axis=-1, keepdims=True)
        pv = jnp.concatenate(
            [_dot(p[kvh * rows:(kvh + 1) * rows].astype(bf16), v_heads[kvh])
             for kvh in range(N_KV_HEADS)], axis=0)
        acc_scr[...] = alpha * acc_scr[...] + pv
        m_scr[...] = m_new

    for copy in page_copies(b, j, slot):
        copy.wait()

    def head_rows(buf, kvh):
        return buf[slot, pl.ds(kvh, pps * PAGE_SIZE, stride=N_KV_HEADS), :].astype(bf16)

    attend([head_rows(kbuf, kvh) for kvh in range(N_KV_HEADS)],
           [head_rows(vbuf, kvh) for kvh in range(N_KV_HEADS)], bias_ref[...])

    @pl.when(j == n_j - 1)
    def _():
        pad = jnp.zeros((LANES - t, HEAD_DIM), bf16)

        def new_heads(ref):
            return [jnp.concatenate([ref[:, kvh * HEAD_DIM:(kvh + 1) * HEAD_DIM], pad], axis=0)
                    for kvh in range(N_KV_HEADS)]

        attend(new_heads(kbn_ref), new_heads(vbn_ref), biasn_ref[...])
        o = acc_scr[...] / l_scr[...]
        att = jnp.concatenate([o[h * t:(h + 1) * t] for h in range(N_HEADS)],
                              axis=1).astype(bf16)
        o_ref[...] = x_ref[...] + _dot(att, wo_b[...])


def _sample_attend(page_table, bias, q, kb_new, vb_new, x, wo, cache_k, cache_v, pages_per_step):
    nb, n_pages = page_table.shape
    t = q.shape[0] // nb
    steps = n_pages // pages_per_step
    span = pages_per_step * PAGE_SIZE
    row = lambda width: pl.BlockSpec((t, width), lambda b, j, pt: (b, 0))
    rows = N_HEADS * t
    buf = pltpu.VMEM((2, span * N_KV_HEADS, HEAD_DIM), f32)
    return pl.pallas_call(
        functools.partial(_sample_attend_kernel, pps=pages_per_step),
        grid_spec=pltpu.PrefetchScalarGridSpec(
            num_scalar_prefetch=1,
            grid=(nb, steps),
            in_specs=[pl.BlockSpec((t, span), lambda b, j, pt: (b, j)),
                      pl.BlockSpec((t, LANES), lambda b, j, pt: (b, n_pages)),
                      row(Q_W), row(KV_W), row(KV_W), row(D_MODEL),
                      _weight_spec((Q_W, D_MODEL)),
                      pl.BlockSpec(memory_space=pl.ANY), pl.BlockSpec(memory_space=pl.ANY)],
            out_specs=row(D_MODEL),
            scratch_shapes=[buf, buf, pltpu.SemaphoreType.DMA((2,)), pltpu.SemaphoreType.DMA((2,)),
                            pltpu.VMEM((rows, 1), f32),
                            pltpu.VMEM((rows, 1), f32),
                            pltpu.VMEM((rows, HEAD_DIM), f32),
                            pltpu.VMEM((Q_W, D_MODEL), bf16)],
        ),
        out_shape=jax.ShapeDtypeStruct((nb * t, D_MODEL), f32),
        compiler_params=_params(2),
        name="dsa_sample_attend",
    )(page_table, bias, bias, q, kb_new, vb_new, x, wo, cache_k, cache_v)


def _ffn_kernel(x_ref, g_ref, wgu_ref, wd_ref, o_ref, *, n_ff):
    x = x_ref[...]
    h = _rms(x, g_ref[...]).astype(bf16)
    acc = x
    width = D_FF // n_ff
    for c in range(n_ff):
        gate = _dot(h, wgu_ref[0, :, c * width:(c + 1) * width])
        up = _dot(h, wgu_ref[0, :, D_FF + c * width:D_FF + (c + 1) * width])
        acc = acc + _dot((jax.nn.silu(gate) * up).astype(bf16),
                         wd_ref[0, c * width:(c + 1) * width, :])
    o_ref[...] = acc


def _ffn(x, g, wgu, wd, layer, tm, n_ff):
    t = x.shape[0]
    row = pl.BlockSpec((tm, D_MODEL), lambda i: (i, 0))
    resident = lambda a: pl.BlockSpec((1,) + a.shape[1:], lambda i: (layer, 0, 0),
                                      pipeline_mode=pl.Buffered(1))
    return pl.pallas_call(
        functools.partial(_ffn_kernel, n_ff=n_ff),
        grid=(t // tm,),
        in_specs=[row, _const_spec((1, D_MODEL)), resident(wgu), resident(wd)],
        out_specs=row,
        out_shape=jax.ShapeDtypeStruct((t, D_MODEL), f32),
        compiler_params=_params(1),
        name="ffn",
    )(x, g, wgu, wd)


def _sgu_kernel(x_ref, g_ref, win_ref, gv_ref, ws_ref, bias_ref, wout_ref, *refs,
                chunk_len, emit_v):
    if emit_v:
        o_ref, v_ref, gated_scr, win_b, wout_b = refs
    else:
        o_ref, gated_scr, win_b, wout_b = refs
    _cast_weights_once(pl.program_id(0) == 0, [(win_ref, win_b), (wout_ref, wout_b)])
    tm = x_ref.shape[0]
    n_sub = tm // CHUNK
    x = x_ref[...]
    h = _rms(x, g_ref[...]).astype(bf16)
    z = jax.nn.gelu(_dot(h, win_b[...]))
    u = z[:, :D_MODEL]
    v = _rms(z[:, D_MODEL:], gv_ref[...])
    if emit_v:
        v_ref[...] = v
    vb = v.astype(bf16)
    ti = lax.broadcasted_iota(jnp.int32, (CHUNK, CHUNK), 0)
    si = lax.broadcasted_iota(jnp.int32, (CHUNK, CHUNK), 1)
    shift = chunk_len.bit_length() - 1
    causal = (si <= ti) & (lax.shift_right_logical(si, shift) == lax.shift_right_logical(ti, shift))
    bias = bias_ref[...]
    for grp in range(SG_GROUPS):
        cols = slice(grp * LANES, (grp + 1) * LANES)
        w = jnp.where(causal, ws_ref[grp], 0.0).astype(bf16)
        vg = jnp.concatenate([vb[c * CHUNK:(c + 1) * CHUNK, cols] for c in range(n_sub)], axis=1)
        mixed = _dot(w, vg) + bias[:, grp:grp + 1]
        for c in range(n_sub):
            rows = slice(c * CHUNK, (c + 1) * CHUNK)
            gated_scr[rows, cols] = (u[rows, cols] * mixed[:, c * LANES:(c + 1) * LANES]).astype(bf16)
    o_ref[...] = x + _dot(gated_scr[...], wout_b[...])


def _sgu(x, g, w_in, g_v, w_s, bias_t, w_out, tm, chunk_len, emit_v):
    t = x.shape[0]
    row = pl.BlockSpec((tm, D_MODEL), lambda i: (i, 0))
    out_shape = [jax.ShapeDtypeStruct((t, D_MODEL), f32)] * (2 if emit_v else 1)
    return pl.pallas_call(
        functools.partial(_sgu_kernel, chunk_len=chunk_len, emit_v=emit_v),
        grid=(t // tm,),
        in_specs=[row, _const_spec((1, D_MODEL)), _weight_spec(w_in.shape),
                  _const_spec((1, D_MODEL)), _const_spec(w_s.shape), _const_spec(bias_t.shape),
                  _weight_spec(w_out.shape)],
        out_specs=[row] * len(out_shape),
        out_shape=out_shape,
        scratch_shapes=[pltpu.VMEM((tm, D_MODEL), bf16), pltpu.VMEM(w_in.shape, bf16),
                        pltpu.VMEM(w_out.shape, bf16)],
        compiler_params=_params(1),
        name="sgu",
    )(x, g, w_in, g_v, w_s, bias_t, w_out)


def kernel(x_prompt, x_sample, cache_k, cache_v, cache_kidx, page_table, norm_mix, norm_ffn,
           att_w_in, att_q_norm, att_k_norm, att_w_out, sg_w_in, sg_v_norm, sg_w_s, sg_b_s,
           sg_w_out, ffn_w_gu, ffn_w_down):
    nb_p, t_p, _ = x_prompt.shape
    nb_s, t_s, _ = x_sample.shape
    n_pool = cache_k.shape[1]
    xp = x_prompt.reshape(nb_p * t_p, D_MODEL)
    xs = x_sample.reshape(nb_s * t_s, D_MODEL)
    vec = lambda a: a.reshape(1, -1)

    w_out = att_w_out[0]
    proj = functools.partial(_att_proj, g=vec(norm_mix[0]), w=att_w_in[0],
                             gq=vec(att_q_norm[0]), gk=vec(att_k_norm[0]))
    q_p, k_p, v_p, kb_p, _, qic_p, kic_p, ki_p, kiwi_p, vt_p = proj(xp, tm=512)
    q_s, k_s, v_s, kb_s, vb_s, qic_s, kic_s, ki_s, kiwi_s, _ = proj(xs, tm=nb_s * t_s)

    per_seq = lambda a: a.reshape(nb_p, t_p, a.shape[-1])
    xp = _dsa_prompt(per_seq(q_p), per_seq(qic_p), per_seq(kiwi_p), per_seq(kic_p),
                     per_seq(kb_p), vt_p, per_seq(xp), w_out, tq=128, span=512)
    xp = xp.reshape(nb_p * t_p, D_MODEL)

    scores_s = _sample_scores(page_table, qic_s, kiwi_s, kic_s, jnp.swapaxes(cache_kidx[0], 1, 2))
    bias_s = _sample_select(scores_s, t_s, page_table.shape[1] * PAGE_SIZE, rows_per_step=64)
    kv_rows = n_pool * PAGE_SIZE * N_KV_HEADS
    xs = _sample_attend(page_table, bias_s, q_s, kb_s, vb_s, xs, w_out,
                        cache_k[0].reshape(kv_rows, HEAD_DIM),
                        cache_v[0].reshape(kv_rows, HEAD_DIM), pages_per_step=64)

    ffn_gu, ffn_down = ffn_w_gu.astype(bf16), ffn_w_down.astype(bf16)

    def ffn(x, layer, tm):
        return _ffn(x, vec(norm_ffn[layer]), ffn_gu, ffn_down, layer, tm, n_ff=11)

    xp = ffn(xp, 0, 512)
    xs = ffn(xs, 0, nb_s * t_s)

    sgu = functools.partial(_sgu, g=vec(norm_mix[1]), w_in=sg_w_in[0], g_v=vec(sg_v_norm[0]),
                            w_out=sg_w_out[0])
    (xp,) = sgu(xp, w_s=sg_w_s[0], bias_t=sg_b_s[0].T, tm=512, chunk_len=CHUNK, emit_v=False)
    reps = CHUNK // t_s
    xs, sgv = sgu(xs, w_s=jnp.tile(sg_w_s[0][:, :t_s, :t_s], (1, reps, reps)),
                  bias_t=jnp.tile(sg_b_s[0][:, :t_s], (1, reps)).T, tm=nb_s * t_s,
                  chunk_len=t_s, emit_v=True)

    xp = ffn(xp, 1, 512)
    xs = ffn(xs, 1, nb_s * t_s)

    y_prompt = xp.reshape(nb_p, t_p, D_MODEL)
    y_sample = xs.reshape(nb_s, t_s, D_MODEL)
    kv_p = lambda a: a.reshape(1, nb_p, t_p, N_KV_HEADS, HEAD_DIM)
    kv_s = lambda a: a.reshape(1, nb_s, t_s, N_KV_HEADS, HEAD_DIM)
    return (y_prompt, y_sample, kv_p(k_p), kv_p(v_p), ki_p.reshape(1, nb_p, t_p, IDX_DIM),
            kv_s(k_s), kv_s(v_s), ki_s.reshape(1, nb_s, t_s, IDX_DIM),
            sgv.reshape(1, nb_s, t_s, D_MODEL))
```

```python
import functools

import jax
import jax.numpy as jnp
from jax import lax
from jax.experimental import pallas as pl
from jax.experimental.pallas import tpu as pltpu

D_MODEL = 1024
HEAD_DIM = 128
N_HEADS = 8
N_KV_HEADS = 2
GROUP = N_HEADS // N_KV_HEADS
IDX_HEADS = 4
IDX_DIM = 64
IDX_SCALE = (IDX_DIM ** -0.5) * (IDX_HEADS ** -0.5)
TOPK_MAX = 256
CHUNK = 128
SG_GROUPS = 8
D_FF = 2816
PAGE_SIZE = 128
EPS = 1e-6

Q_W = N_HEADS * HEAD_DIM
KV_W = N_KV_HEADS * HEAD_DIM
QI_W = IDX_HEADS * IDX_DIM
ATT_IN = Q_W + 2 * KV_W + QI_W + IDX_DIM + IDX_HEADS
LANES = 128
SUBLANES = 8
BF16_ROWS = 16
SOFTMAX2_SCALE = (HEAD_DIM ** -0.5) * 1.4426950408889634
ATT_IN_PAD = -(-ATT_IN // LANES) * LANES
IDX_K = 4 * IDX_DIM
INT_MIN = -(2 ** 31)
NEG_INF = float("-inf")
VMEM_LIMIT = 56 * 1024 * 1024

bf16 = jnp.bfloat16
f32 = jnp.float32


def _rms(x, g):
    return x * lax.rsqrt(jnp.mean(x * x, axis=-1, keepdims=True) + EPS) * g


def _dot(a, b):
    return jnp.dot(a, b, preferred_element_type=f32)


def _dot_nt(a, b):
    return lax.dot_general(a, b, (((1,), (1,)), ((), ())), preferred_element_type=f32)


def _hi(x):
    return x.astype(bf16).astype(f32)


def _params(n_grid):
    return pltpu.CompilerParams(
        dimension_semantics=("arbitrary",) * n_grid, vmem_limit_bytes=VMEM_LIMIT)


def _const_spec(shape):
    nd = len(shape)
    return pl.BlockSpec(shape, lambda *_: (0,) * nd)


def _weight_spec(shape):
    nd = len(shape)
    return pl.BlockSpec(shape, lambda *_: (0,) * nd, pipeline_mode=pl.Buffered(1))


def _cast_weights_once(is_first_step, pairs):
    @pl.when(is_first_step)
    def _():
        for src, dst in pairs:
            dst[...] = src[...].astype(dst.dtype)


def _att_proj_kernel(x_ref, g_ref, w_ref, gq_ref, gk_ref,
                     q_ref, k_ref, v_ref, kb_ref, vb_ref, qic_ref, kic_ref, ki_ref, kiwi_ref, vt_ref,
                     w_b):
    @pl.when(pl.program_id(0) == 0)
    def _():
        w_b[:, ATT_IN_PAD - LANES:] = jnp.zeros((D_MODEL, LANES), bf16)
        w_b[:, :ATT_IN] = w_ref[...].astype(bf16)

    tm = x_ref.shape[0]
    h = _rms(x_ref[...], g_ref[...]).astype(bf16)
    z = _dot(h, w_b[...])
    gq = gq_ref[...]
    gk = gk_ref[...]
    for j in range(N_HEADS):
        sl = slice(j * HEAD_DIM, (j + 1) * HEAD_DIM)
        q_ref[:, sl] = (_rms(z[:, sl], gq) * SOFTMAX2_SCALE).astype(bf16)
    v = z[:, Q_W + KV_W:Q_W + 2 * KV_W]
    for j in range(N_KV_HEADS):
        sl = slice(j * HEAD_DIM, (j + 1) * HEAD_DIM)
        kj = _rms(z[:, Q_W + j * HEAD_DIM:Q_W + (j + 1) * HEAD_DIM], gk)
        kb_ref[:, sl] = kj.astype(bf16)
        head_rows = pl.ds(j, tm, stride=N_KV_HEADS)
        k_ref[head_rows, :] = kj
        v_ref[head_rows, :] = v[:, sl]
    vb_ref[...] = v.astype(bf16)
    vt_ref[...] = v.T.astype(bf16)

    low_half = lax.broadcasted_iota(jnp.int32, (tm, LANES), 1) < IDX_DIM
    qi0 = Q_W + 2 * KV_W
    for p in range(IDX_HEADS // 2):
        pair = z[:, qi0 + p * LANES:qi0 + (p + 1) * LANES]
        rolled = pltpu.roll(pair, IDX_DIM, axis=1)
        for e, (own, other) in enumerate(((pair, rolled), (rolled, pair))):
            dup = jnp.where(low_half, own, other)
            single = jnp.where(low_half, own, 0.0)
            base = (2 * p + e) * IDX_K
            qic_ref[:, base:base + LANES] = dup.astype(bf16)
            qic_ref[:, base + LANES:base + 2 * LANES] = (single - _hi(single)).astype(bf16)
    last = z[:, qi0 + QI_W:qi0 + QI_W + LANES]
    rolled = pltpu.roll(last, IDX_DIM, axis=1)
    kic_ref[:, :LANES] = jnp.where(low_half, _hi(last), rolled - _hi(rolled)).astype(bf16)
    kic_ref[:, LANES:] = jnp.where(low_half, _hi(last), 0.0).astype(bf16)
    ki_ref[...] = last[:, :IDX_DIM]
    kiwi_ref[...] = last


def _att_proj(x, g, w, gq, gk, tm):
    t = x.shape[0]
    row = lambda width: pl.BlockSpec((tm, width), lambda i: (i, 0))
    head_rows = pl.BlockSpec((N_KV_HEADS * tm, HEAD_DIM), lambda i: (i, 0))
    head_rows_shape = jax.ShapeDtypeStruct((N_KV_HEADS * t, HEAD_DIM), f32)
    outs = [(Q_W, bf16), None, None, (KV_W, bf16), (KV_W, bf16),
            (IDX_HEADS * IDX_K, bf16), (IDX_K, bf16), (IDX_DIM, f32), (LANES, f32)]
    return pl.pallas_call(
        _att_proj_kernel,
        grid=(t // tm,),
        in_specs=[row(D_MODEL), _const_spec((1, D_MODEL)), _weight_spec((D_MODEL, ATT_IN)),
                  _const_spec((1, HEAD_DIM)), _const_spec((1, HEAD_DIM))],
        out_specs=[row(o[0]) if o else head_rows for o in outs]
        + [pl.BlockSpec((KV_W, tm), lambda i: (0, i))],
        out_shape=[jax.ShapeDtypeStruct((t, o[0]), o[1]) if o else head_rows_shape for o in outs]
        + [jax.ShapeDtypeStruct((KV_W, t), bf16)],
        scratch_shapes=[pltpu.VMEM((D_MODEL, ATT_IN_PAD), bf16)],
        compiler_params=_params(1),
        name="att_proj",
    )(x, g, w, gq, gk)


def _key_to_float(key):
    bits = jnp.where(key < 0, key ^ jnp.int32(0x7FFFFFFF), key)
    return lax.bitcast_convert_type(bits, f32)


def _count(mask, axis=1):
    ones = jnp.where(mask, 1.0, 0.0)
    if axis == 0:
        n, cols = ones.shape
        rows = 4 * SUBLANES
        ones = jnp.sum(ones.reshape(n // rows, rows, cols), axis=0)
    return jnp.sum(ones, axis=axis, keepdims=True)


def _rank_threshold(score_ref, k_top, axis):
    shape = tuple(1 if a == axis else d for a, d in enumerate(score_ref.shape))

    def step(i, t):
        cand = t + lax.shift_left(jnp.int32(1), jnp.int32(31) - i)
        enough = _count(score_ref[...] >= _key_to_float(cand), axis) >= k_top
        return jnp.where(enough, cand, t)

    key = lax.fori_loop(0, 32, step, jnp.full(shape, INT_MIN, jnp.int32))
    t = _key_to_float(key)
    return t, jnp.where(key == INT_MIN, NEG_INF, t)


def _topk_bias_keys_major(score_ref, bias_ref, tril_ref, k_top):
    n, _ = score_ref.shape
    t, t_above = _rank_threshold(score_ref, k_top, axis=0)
    need = k_top - _count(score_ref[...] > t_above, axis=0)
    ties_before = jnp.zeros(t.shape, f32)
    for c in range(n // LANES):
        sl = slice(c * LANES, (c + 1) * LANES)
        s_c = score_ref[sl, :]
        tie = s_c == t
        tie_f = jnp.where(tie, 1.0, 0.0)
        rank = _dot(tril_ref[...], tie_f.astype(bf16)) + ties_before
        ties_before = ties_before + jnp.sum(tie_f, axis=0, keepdims=True)
        bias_ref[sl, :] = jnp.where((s_c > t_above) | (tie & (rank <= need)), 0.0, NEG_INF)


def _topk_bias(score_ref, tri_ref, before_ref, k_top, store):
    _, n = score_ref.shape
    t, t_above = _rank_threshold(score_ref, k_top, axis=1)
    s = score_ref[...]
    need = k_top - _count(s > t_above)
    tie_b = jnp.where(s == t, 1.0, 0.0).astype(bf16)
    before = _dot(tie_b, before_ref[:n, :])
    for c in range(n // LANES):
        sl = slice(c * LANES, (c + 1) * LANES)
        s_c = s[:, sl]
        rank = _dot(tie_b[:, sl], tri_ref[...]) + before[:, c:c + 1]
        store(c, jnp.where((s_c > t_above) | ((s_c == t) & (rank <= need)), 0.0, NEG_INF))


def _tie_consts(n):
    a = jnp.arange(LANES)
    tri = (a[:, None] <= a[None, :]).astype(bf16)
    before = ((jnp.arange(n)[:, None] // LANES) < a[None, :]).astype(bf16)
    return tri, before


def _indexer_scores(qic, wi, kic):
    rows = qic.shape[0]
    heads = [qic[:, h * IDX_K:(h + 1) * IDX_K] for h in range(IDX_HEADS)]
    if rows % BF16_ROWS == 0:
        logits = _dot_nt(jnp.concatenate(heads, axis=0), kic)
        logits = [logits[h * rows:(h + 1) * rows] for h in range(IDX_HEADS)]
    else:
        logits = [_dot_nt(qh, kic) for qh in heads]
    score = None
    for h in range(IDX_HEADS):
        relu = jnp.maximum(logits[h], 0.0)
        term = wi[:, IDX_DIM + h:IDX_DIM + h + 1] * relu
        score = term if score is None else score + term
    return score * IDX_SCALE


def _dsa_prompt_tile(lk, q0, k_top, q_ref, qic_ref, kiwi_ref, kic_ref, kb_ref, vt_ref, tril_ref,
                     score_scr, bias_scr, s_scr, p_scr, att_scr):
    tq = q_ref.shape[1]
    qic = qic_ref[0]
    q_idx = jnp.concatenate(
        [qic[:, h * IDX_K:(h + 1) * IDX_K] for h in range(IDX_HEADS)], axis=0)
    logits = _dot_nt(kic_ref[0, :lk, :], q_idx)
    wi_t = kiwi_ref[0].T
    score = None
    for h in range(IDX_HEADS):
        term = wi_t[IDX_DIM + h:IDX_DIM + h + 1, :] * jnp.maximum(logits[:, h * tq:(h + 1) * tq], 0.0)
        score = term if score is None else score + term
    score = score * IDX_SCALE
    q_pos = q0 + lax.broadcasted_iota(jnp.int32, (1, tq), 1)
    admissible = lax.broadcasted_iota(jnp.int32, (lk, 1), 0) <= q_pos
    scores = score_scr.at[:lk, :]
    scores[...] = jnp.where(admissible, score, NEG_INF)

    bias_ref = bias_scr.at[:lk, :]
    _topk_bias_keys_major(scores, bias_ref, tril_ref, k_top)

    q = q_ref[0]
    bias = jnp.concatenate([bias_ref[...]] * GROUP, axis=1)
    for kvh in range(N_KV_HEADS):
        qs = jnp.concatenate(
            [q[:, (kvh * GROUP + g) * HEAD_DIM:(kvh * GROUP + g + 1) * HEAD_DIM]
             for g in range(GROUP)], axis=0)
        kv_sl = slice(kvh * HEAD_DIM, (kvh + 1) * HEAD_DIM)
        s_scr[kvh, :lk, :] = _dot_nt(kb_ref[0, :lk, kv_sl], qs) + bias

    for kvh in range(N_KV_HEADS):
        kv_sl = slice(kvh * HEAD_DIM, (kvh + 1) * HEAD_DIM)
        denoms = []
        for g in range(GROUP):
            cols = slice(g * tq, (g + 1) * tq)
            m = jnp.max(s_scr[kvh, :lk, cols], axis=0, keepdims=True)
            p = jnp.exp2(s_scr[kvh, :lk, cols] - m)
            denoms.append(jnp.sum(p, axis=0, keepdims=True))
            p_scr[kvh, :lk, cols] = p.astype(bf16)
        o_t = _dot(vt_ref[kv_sl, :lk], p_scr[kvh, :lk, :])
        o_t = o_t / jnp.concatenate(denoms, axis=1)
        for g in range(GROUP):
            col = (kvh * GROUP + g) * HEAD_DIM
            att_scr[:, col:col + HEAD_DIM] = o_t[:, g * tq:(g + 1) * tq].T.astype(bf16)


def _dsa_prompt_kernel(q_ref, qic_ref, kiwi_ref, kic_ref, kb_ref, vt_ref, x_ref, wo_ref,
                       tril_ref, o_ref, wo_b, *scratch, k_top, key_counts):
    _cast_weights_once((pl.program_id(0) == 0) & (pl.program_id(1) == 0), [(wo_ref, wo_b)])
    tq = q_ref.shape[1]
    q0 = pl.program_id(1) * tq
    q_end = q0 + tq
    for lower, lk in zip((0,) + key_counts[:-1], key_counts):
        @pl.when((q_end > lower) & (q_end <= lk))
        def _(lk=lk):
            _dsa_prompt_tile(lk, q0, k_top, q_ref, qic_ref, kiwi_ref, kic_ref, kb_ref, vt_ref,
                             tril_ref, *scratch)
    att_scr = scratch[-1]
    o_ref[0] = x_ref[0] + _dot(att_scr[...], wo_b[...])


def _dsa_prompt(q, qic, kiwi, kic, kb, vt, x, wo, tq, key_counts):
    b, t, _ = q.shape
    k_top = min(TOPK_MAX, t // 4)
    a = jnp.arange(LANES)
    tril = (a[None, :] <= a[:, None]).astype(bf16)
    cols = GROUP * tq
    tile = lambda width: pl.BlockSpec((1, tq, width), lambda i, j: (i, j, 0))
    full = lambda width: pl.BlockSpec((1, t, width), lambda i, j: (i, 0, 0))
    return pl.pallas_call(
        functools.partial(_dsa_prompt_kernel, k_top=k_top, key_counts=key_counts),
        grid=(b, t // tq),
        in_specs=[tile(Q_W), tile(IDX_HEADS * IDX_K), tile(LANES), full(IDX_K), full(KV_W),
                  pl.BlockSpec((KV_W, t), lambda i, j: (0, i)), tile(D_MODEL),
                  _weight_spec((Q_W, D_MODEL)), _const_spec((LANES, LANES))],
        out_specs=tile(D_MODEL),
        out_shape=jax.ShapeDtypeStruct((b, t, D_MODEL), f32),
        scratch_shapes=[pltpu.VMEM((Q_W, D_MODEL), bf16),
                        pltpu.VMEM((t, tq), f32),
                        pltpu.VMEM((t, tq), f32),
                        pltpu.VMEM((N_KV_HEADS, t, cols), f32),
                        pltpu.VMEM((N_KV_HEADS, t, cols), bf16),
                        pltpu.VMEM((tq, Q_W), bf16)],
        compiler_params=_params(2),
        name="dsa_prompt",
    )(q, qic, kiwi, kic, kb, vt, x, wo, tril)


def _sample_scores_kernel(pt_ref, qic_ref, kiwi_ref, kicn_ref, cache_ref, o_ref, buf, sem):
    b = pl.program_id(0)
    n_pages = pt_ref.shape[1]
    n_past = n_pages * PAGE_SIZE
    t = qic_ref.shape[0]

    def page_copies(batch, slot):
        return [pltpu.make_async_copy(
            cache_ref.at[pt_ref[batch, p]],
            buf.at[slot, :, pl.ds(p * PAGE_SIZE, PAGE_SIZE)], sem.at[slot])
            for p in range(n_pages)]

    slot = lax.rem(b, 2)

    @pl.when(b == 0)
    def _():
        for copy in page_copies(0, 0):
            copy.start()

    @pl.when(b + 1 < pl.num_programs(0))
    def _():
        for copy in page_copies(b + 1, 1 - slot):
            copy.start()

    for copy in page_copies(b, slot):
        copy.wait()

    keys_t = buf[slot]
    k_hi = keys_t.astype(bf16)
    k_lo = (keys_t - k_hi.astype(f32)).astype(bf16)
    qic = qic_ref[...].astype(f32)

    def stacked(offset):
        return jnp.concatenate(
            [qic[:, h * IDX_K + offset:h * IDX_K + offset + IDX_DIM] for h in range(IDX_HEADS)],
            axis=0).astype(bf16)

    q_hi, q_lo = stacked(0), stacked(2 * IDX_DIM)
    logits = _dot(q_hi, k_hi) + _dot(q_hi, k_lo) + _dot(q_lo, k_hi)
    wi = kiwi_ref[...]
    score = None
    for h in range(IDX_HEADS):
        term = wi[:, IDX_DIM + h:IDX_DIM + h + 1] * jnp.maximum(logits[h * t:(h + 1) * t], 0.0)
        score = term if score is None else score + term
    o_ref[:, :n_past] = score * IDX_SCALE
    new = _indexer_scores(qic_ref[...], wi, kicn_ref[...])
    o_ref[:, n_past:] = jnp.concatenate([new, jnp.zeros((t, LANES - t), f32)], axis=1)


def _sample_scores(page_table, qic, kiwi, kic_new, cache_ki_t):
    nb, n_pages = page_table.shape
    t = qic.shape[0] // nb
    n_past = n_pages * PAGE_SIZE
    row = lambda width: pl.BlockSpec((t, width), lambda b, pt: (b, 0))
    return pl.pallas_call(
        _sample_scores_kernel,
        grid_spec=pltpu.PrefetchScalarGridSpec(
            num_scalar_prefetch=1,
            grid=(nb,),
            in_specs=[row(IDX_HEADS * IDX_K), row(LANES), row(IDX_K),
                      pl.BlockSpec(memory_space=pl.ANY)],
            out_specs=row(n_past + LANES),
            scratch_shapes=[pltpu.VMEM((2, IDX_DIM, n_past), f32),
                            pltpu.SemaphoreType.DMA((2,))],
        ),
        out_shape=jax.ShapeDtypeStruct((nb * t, n_past + LANES), f32),
        compiler_params=_params(1),
        name="dsa_sample_scores",
    )(page_table, qic, kiwi, kic_new, cache_ki_t)


def _sample_select_kernel(sc_ref, tri_ref, before_ref, o_ref, score_scr, *, k_top, t, n_past):
    rows, n = sc_ref.shape
    qi = lax.broadcasted_iota(jnp.int32, (rows, n), 0) & (t - 1)
    kj = lax.broadcasted_iota(jnp.int32, (rows, n), 1)
    admissible = kj <= n_past + qi
    score_scr[...] = jnp.where(admissible, sc_ref[...], NEG_INF)

    def store(c, piece):
        o_ref[:, c * LANES:(c + 1) * LANES] = piece

    _topk_bias(score_scr, tri_ref, before_ref, k_top, store)


def _sample_select(scores, t, n_past, rows_per_step):
    n_rows, n_all = scores.shape
    assert t & (t - 1) == 0 and rows_per_step % t == 0, (t, rows_per_step)
    k_top = min(TOPK_MAX, (n_past + t) // 4)
    tri, before = _tie_consts(n_all)
    block = pl.BlockSpec((rows_per_step, n_all), lambda i: (i, 0))
    return pl.pallas_call(
        functools.partial(_sample_select_kernel, k_top=k_top, t=t, n_past=n_past),
        grid=(n_rows // rows_per_step,),
        in_specs=[block, _const_spec((LANES, LANES)), _const_spec((n_all, LANES))],
        out_specs=block,
        out_shape=jax.ShapeDtypeStruct((n_rows, n_all), f32),
        scratch_shapes=[pltpu.VMEM((rows_per_step, n_all), f32)],
        compiler_params=_params(1),
        name="dsa_sample_select",
    )(scores, tri, before)


def _sample_attend_kernel(pt_ref, bias_ref, biasn_ref, q_ref, kbn_ref, vbn_ref, x_ref, wo_ref,
                          ck_ref, cv_ref, o_ref, kbuf, vbuf, ksem, vsem, m_scr, l_scr, acc_scr,
                          wo_b, *, pps):
    b, j = pl.program_id(0), pl.program_id(1)
    n_j = pl.num_programs(1)
    step = b * n_j + j
    _cast_weights_once(step == 0, [(wo_ref, wo_b)])
    t = q_ref.shape[0]
    page_rows = PAGE_SIZE * N_KV_HEADS

    def page_copies(batch, chunk, slot):
        copies = []
        for i in range(pps):
            src = pl.ds(pl.multiple_of(pt_ref[batch, chunk * pps + i] * page_rows, page_rows),
                        page_rows)
            dst = pl.ds(i * page_rows, page_rows)
            copies.append(pltpu.make_async_copy(ck_ref.at[src, :], kbuf.at[slot, dst, :], ksem.at[slot]))
            copies.append(pltpu.make_async_copy(cv_ref.at[src, :], vbuf.at[slot, dst, :], vsem.at[slot]))
        return copies

    slot = lax.rem(step, 2)

    @pl.when(step == 0)
    def _():
        for copy in page_copies(0, 0, 0):
            copy.start()

    @pl.when(step + 1 < pl.num_programs(0) * n_j)
    def _():
        nxt = step + 1
        for copy in page_copies(nxt // n_j, lax.rem(nxt, n_j), 1 - slot):
            copy.start()

    @pl.when(j == 0)
    def _():
        m_scr[...] = jnp.full(m_scr.shape, -1e30, f32)
        l_scr[...] = jnp.zeros(l_scr.shape, f32)
        acc_scr[...] = jnp.zeros(acc_scr.shape, f32)

    q = q_ref[...]

    rows = GROUP * t

    def attend(k_heads, v_heads, bias):
        s = jnp.concatenate(
            [_dot_nt(jnp.concatenate(
                [q[:, (kvh * GROUP + g) * HEAD_DIM:(kvh * GROUP + g + 1) * HEAD_DIM]
                 for g in range(GROUP)], axis=0), k_heads[kvh]) for kvh in range(N_KV_HEADS)],
            axis=0)
        s = s + jnp.concatenate([bias] * N_HEADS, axis=0)
        m_old = m_scr[...]
        m_new = jnp.maximum(m_old, jnp.max(s, axis=-1, keepdims=True))
        alpha = jnp.exp2(m_old - m_new)
        p = jnp.exp2(s - m_new)
        l_scr[...] = alpha * l_scr[...] + jnp.sum(p, axis=-1, keepdims=True)
        pv = jnp.concatenate(
            [_dot(p[kvh * rows:(kvh + 1) * rows].astype(bf16), v_heads[kvh])
             for kvh in range(N_KV_HEADS)], axis=0)
        acc_scr[...] = alpha * acc_scr[...] + pv
        m_scr[...] = m_new

    for copy in page_copies(b, j, slot):
        copy.wait()

    def head_rows(buf, kvh):
        return buf[slot, pl.ds(kvh, pps * PAGE_SIZE, stride=N_KV_HEADS), :].astype(bf16)

    attend([head_rows(kbuf, kvh) for kvh in range(N_KV_HEADS)],
           [head_rows(vbuf, kvh) for kvh in range(N_KV_HEADS)], bias_ref[...])

    @pl.when(j == n_j - 1)
    def _():
        pad = jnp.zeros((LANES - t, HEAD_DIM), bf16)

        def new_heads(ref):
            return [jnp.concatenate([ref[:, kvh * HEAD_DIM:(kvh + 1) * HEAD_DIM], pad], axis=0)
                    for kvh in range(N_KV_HEADS)]

        attend(new_heads(kbn_ref), new_heads(vbn_ref), biasn_ref[...])
        o = acc_scr[...] / l_scr[...]
        att = jnp.concatenate([o[h * t:(h + 1) * t] for h in range(N_HEADS)],
                              axis=1).astype(bf16)
        o_ref[...] = x_ref[...] + _dot(att, wo_b[...])


def _sample_attend(page_table, bias, q, kb_new, vb_new, x, wo, cache_k, cache_v, pages_per_step):
    nb, n_pages = page_table.shape
    t = q.shape[0] // nb
    steps = n_pages // pages_per_step
    span = pages_per_step * PAGE_SIZE
    row = lambda width: pl.BlockSpec((t, width), lambda b, j, pt: (b, 0))
    rows = N_HEADS * t
    buf = pltpu.VMEM((2, span * N_KV_HEADS, HEAD_DIM), f32)
    return pl.pallas_call(
        functools.partial(_sample_attend_kernel, pps=pages_per_step),
        grid_spec=pltpu.PrefetchScalarGridSpec(
            num_scalar_prefetch=1,
            grid=(nb, steps),
            in_specs=[pl.BlockSpec((t, span), lambda b, j, pt: (b, j)),
                      pl.BlockSpec((t, LANES), lambda b, j, pt: (b, n_pages)),
                      row(Q_W), row(KV_W), row(KV_W), row(D_MODEL),
                      _weight_spec((Q_W, D_MODEL)),
                      pl.BlockSpec(memory_space=pl.ANY), pl.BlockSpec(memory_space=pl.ANY)],
            out_specs=row(D_MODEL),
            scratch_shapes=[buf, buf, pltpu.SemaphoreType.DMA((2,)), pltpu.SemaphoreType.DMA((2,)),
                            pltpu.VMEM((rows, 1), f32),
                            pltpu.VMEM((rows, 1), f32),
                            pltpu.VMEM((rows, HEAD_DIM), f32),
                            pltpu.VMEM((Q_W, D_MODEL), bf16)],
        ),
        out_shape=jax.ShapeDtypeStruct((nb * t, D_MODEL), f32),
        compiler_params=_params(2),
        name="dsa_sample_attend",
    )(page_table, bias, bias, q, kb_new, vb_new, x, wo, cache_k, cache_v)


def _ffn_kernel(x_ref, g_ref, wgu_ref, wd_ref, o_ref, *, n_ff):
    x = x_ref[...]
    h = _rms(x, g_ref[...]).astype(bf16)
    acc = x
    width = D_FF // n_ff
    for c in range(n_ff):
        gate = _dot(h, wgu_ref[0, :, c * width:(c + 1) * width])
        up = _dot(h, wgu_ref[0, :, D_FF + c * width:D_FF + (c + 1) * width])
        acc = acc + _dot((jax.nn.silu(gate) * up).astype(bf16),
                         wd_ref[0, c * width:(c + 1) * width, :])
    o_ref[...] = acc


def _ffn(x, g, wgu, wd, layer, tm, n_ff):
    t = x.shape[0]
    row = pl.BlockSpec((tm, D_MODEL), lambda i: (i, 0))
    resident = lambda a: pl.BlockSpec((1,) + a.shape[1:], lambda i: (layer, 0, 0),
                                      pipeline_mode=pl.Buffered(1))
    return pl.pallas_call(
        functools.partial(_ffn_kernel, n_ff=n_ff),
        grid=(t // tm,),
        in_specs=[row, _const_spec((1, D_MODEL)), resident(wgu), resident(wd)],
        out_specs=row,
        out_shape=jax.ShapeDtypeStruct((t, D_MODEL), f32),
        compiler_params=_params(1),
        name="ffn",
    )(x, g, wgu, wd)


def _sgu_kernel(x_ref, g_ref, win_ref, gv_ref, ws_ref, bias_ref, wout_ref, *refs,
                chunk_len, emit_v):
    if emit_v:
        o_ref, v_ref, gated_scr, win_b, wout_b = refs
    else:
        o_ref, gated_scr, win_b, wout_b = refs
    _cast_weights_once(pl.program_id(0) == 0, [(win_ref, win_b), (wout_ref, wout_b)])
    tm = x_ref.shape[0]
    n_sub = tm // CHUNK
    x = x_ref[...]
    h = _rms(x, g_ref[...]).astype(bf16)
    z = jax.nn.gelu(_dot(h, win_b[...]))
    u = z[:, :D_MODEL]
    v = _rms(z[:, D_MODEL:], gv_ref[...])
    if emit_v:
        v_ref[...] = v
    vb = v.astype(bf16)
    ti = lax.broadcasted_iota(jnp.int32, (CHUNK, CHUNK), 0)
    si = lax.broadcasted_iota(jnp.int32, (CHUNK, CHUNK), 1)
    shift = chunk_len.bit_length() - 1
    causal = (si <= ti) & (lax.shift_right_logical(si, shift) == lax.shift_right_logical(ti, shift))
    bias = bias_ref[...]
    for grp in range(SG_GROUPS):
        cols = slice(grp * LANES, (grp + 1) * LANES)
        w = jnp.where(causal, ws_ref[grp], 0.0).astype(bf16)
        vg = jnp.concatenate([vb[c * CHUNK:(c + 1) * CHUNK, cols] for c in range(n_sub)], axis=1)
        mixed = _dot(w, vg) + bias[:, grp:grp + 1]
        for c in range(n_sub):
            rows = slice(c * CHUNK, (c + 1) * CHUNK)
            gated_scr[rows, cols] = (u[rows, cols] * mixed[:, c * LANES:(c + 1) * LANES]).astype(bf16)
    o_ref[...] = x + _dot(gated_scr[...], wout_b[...])


def _sgu(x, g, w_in, g_v, w_s, bias_t, w_out, tm, chunk_len, emit_v):
    t = x.shape[0]
    row = pl.BlockSpec((tm, D_MODEL), lambda i: (i, 0))
    out_shape = [jax.ShapeDtypeStruct((t, D_MODEL), f32)] * (2 if emit_v else 1)
    return pl.pallas_call(
        functools.partial(_sgu_kernel, chunk_len=chunk_len, emit_v=emit_v),
        grid=(t // tm,),
        in_specs=[row, _const_spec((1, D_MODEL)), _weight_spec(w_in.shape),
                  _const_spec((1, D_MODEL)), _const_spec(w_s.shape), _const_spec(bias_t.shape),
                  _weight_spec(w_out.shape)],
        out_specs=[row] * len(out_shape),
        out_shape=out_shape,
        scratch_shapes=[pltpu.VMEM((tm, D_MODEL), bf16), pltpu.VMEM(w_in.shape, bf16),
                        pltpu.VMEM(w_out.shape, bf16)],
        compiler_params=_params(1),
        name="sgu",
    )(x, g, w_in, g_v, w_s, bias_t, w_out)


def kernel(x_prompt, x_sample, cache_k, cache_v, cache_kidx, page_table, norm_mix, norm_ffn,
           att_w_in, att_q_norm, att_k_norm, att_w_out, sg_w_in, sg_v_norm, sg_w_s, sg_b_s,
           sg_w_out, ffn_w_gu, ffn_w_down):
    nb_p, t_p, _ = x_prompt.shape
    nb_s, t_s, _ = x_sample.shape
    n_pool = cache_k.shape[1]
    xp = x_prompt.reshape(nb_p * t_p, D_MODEL)
    xs = x_sample.reshape(nb_s * t_s, D_MODEL)
    vec = lambda a: a.reshape(1, -1)

    w_out = att_w_out[0]
    proj = functools.partial(_att_proj, g=vec(norm_mix[0]), w=att_w_in[0],
                             gq=vec(att_q_norm[0]), gk=vec(att_k_norm[0]))
    q_p, k_p, v_p, kb_p, _, qic_p, kic_p, ki_p, kiwi_p, vt_p = proj(xp, tm=512)
    q_s, k_s, v_s, kb_s, vb_s, qic_s, kic_s, ki_s, kiwi_s, _ = proj(xs, tm=nb_s * t_s)

    per_seq = lambda a: a.reshape(nb_p, t_p, a.shape[-1])
    xp = _dsa_prompt(per_seq(q_p), per_seq(qic_p), per_seq(kiwi_p), per_seq(kic_p),
                     per_seq(kb_p), vt_p, per_seq(xp), w_out, tq=128,
                     key_counts=(256, 512, 768, 1024, 1536, t_p))
    xp = xp.reshape(nb_p * t_p, D_MODEL)

    scores_s = _sample_scores(page_table, qic_s, kiwi_s, kic_s, jnp.swapaxes(cache_kidx[0], 1, 2))
    bias_s = _sample_select(scores_s, t_s, page_table.shape[1] * PAGE_SIZE, rows_per_step=64)
    kv_rows = n_pool * PAGE_SIZE * N_KV_HEADS
    xs = _sample_attend(page_table, bias_s, q_s, kb_s, vb_s, xs, w_out,
                        cache_k[0].reshape(kv_rows, HEAD_DIM),
                        cache_v[0].reshape(kv_rows, HEAD_DIM), pages_per_step=64)

    ffn_gu, ffn_down = ffn_w_gu.astype(bf16), ffn_w_down.astype(bf16)

    def ffn(x, layer, tm):
        return _ffn(x, vec(norm_ffn[layer]), ffn_gu, ffn_down, layer, tm, n_ff=11)

    xp = ffn(xp, 0, 512)
    xs = ffn(xs, 0, nb_s * t_s)

    sgu = functools.partial(_sgu, g=vec(norm_mix[1]), w_in=sg_w_in[0], g_v=vec(sg_v_norm[0]),
                            w_out=sg_w_out[0])
    (xp,) = sgu(xp, w_s=sg_w_s[0], bias_t=sg_b_s[0].T, tm=512, chunk_len=CHUNK, emit_v=False)
    reps = CHUNK // t_s
    xs, sgv = sgu(xs, w_s=jnp.tile(sg_w_s[0][:, :t_s, :t_s], (1, reps, reps)),
                  bias_t=jnp.tile(sg_b_s[0][:, :t_s], (1, reps)).T, tm=nb_s * t_s,
                  chunk_len=t_s, emit_v=True)

    xp = ffn(xp, 1, 512)
    xs = ffn(xs, 1, nb_s * t_s)

    y_prompt = xp.reshape(nb_p, t_p, D_MODEL)
    y_sample = xs.reshape(nb_s, t_s, D_MODEL)
    kv_p = lambda a: a.reshape(1, nb_p, t_p, N_KV_HEADS, HEAD_DIM)
    kv_s = lambda a: a.reshape(1, nb_s, t_s, N_KV_HEADS, HEAD_DIM)
    return (y_prompt, y_sample, kv_p(k_p), kv_p(v_p), ki_p.reshape(1, nb_p, t_p, IDX_DIM),
            kv_s(k_s), kv_s(v_s), ki_s.reshape(1, nb_s, t_s, IDX_DIM),
            sgv.reshape(1, nb_s, t_s, D_MODEL))
```

```python
import functools

import jax
import jax.numpy as jnp
from jax import lax
from jax.experimental import pallas as pl
from jax.experimental.pallas import tpu as pltpu

D_MODEL = 1024
HEAD_DIM = 128
N_HEADS = 8
N_KV_HEADS = 2
GROUP = N_HEADS // N_KV_HEADS
IDX_HEADS = 4
IDX_DIM = 64
IDX_SCALE = (IDX_DIM ** -0.5) * (IDX_HEADS ** -0.5)
TOPK_MAX = 256
CHUNK = 128
SG_GROUPS = 8
D_FF = 2816
PAGE_SIZE = 128
EPS = 1e-6

Q_W = N_HEADS * HEAD_DIM
KV_W = N_KV_HEADS * HEAD_DIM
QI_W = IDX_HEADS * IDX_DIM
ATT_IN = Q_W + 2 * KV_W + QI_W + IDX_DIM + IDX_HEADS
LANES = 128
SUBLANES = 8
BF16_ROWS = 16
SOFTMAX2_SCALE = (HEAD_DIM ** -0.5) * 1.4426950408889634
ATT_IN_PAD = -(-ATT_IN // LANES) * LANES
IDX_K = 4 * IDX_DIM
INT_MIN = -(2 ** 31)
NEG_INF = float("-inf")
VMEM_LIMIT = 56 * 1024 * 1024

bf16 = jnp.bfloat16
f32 = jnp.float32


def _rms(x, g):
    return x * lax.rsqrt(jnp.mean(x * x, axis=-1, keepdims=True) + EPS) * g


def _dot(a, b):
    return jnp.dot(a, b, preferred_element_type=f32)


def _dot_nt(a, b):
    return lax.dot_general(a, b, (((1,), (1,)), ((), ())), preferred_element_type=f32)


def _hi(x):
    return x.astype(bf16).astype(f32)


def _params(n_grid):
    return pltpu.CompilerParams(
        dimension_semantics=("arbitrary",) * n_grid, vmem_limit_bytes=VMEM_LIMIT)


def _const_spec(shape):
    nd = len(shape)
    return pl.BlockSpec(shape, lambda *_: (0,) * nd)


def _weight_spec(shape):
    nd = len(shape)
    return pl.BlockSpec(shape, lambda *_: (0,) * nd, pipeline_mode=pl.Buffered(1))


def _cast_weights_once(is_first_step, pairs):
    @pl.when(is_first_step)
    def _():
        for src, dst in pairs:
            dst[...] = src[...].astype(dst.dtype)


def _att_proj_kernel(x_ref, g_ref, w_ref, gq_ref, gk_ref,
                     q_ref, k_ref, v_ref, kb_ref, vb_ref, qic_ref, kic_ref, ki_ref, kiwi_ref, vt_ref,
                     w_b):
    @pl.when(pl.program_id(0) == 0)
    def _():
        w_b[:, ATT_IN_PAD - LANES:] = jnp.zeros((D_MODEL, LANES), bf16)
        w_b[:, :ATT_IN] = w_ref[...].astype(bf16)

    tm = x_ref.shape[0]
    h = _rms(x_ref[...], g_ref[...]).astype(bf16)
    z = _dot(h, w_b[...])
    gq = gq_ref[...]
    gk = gk_ref[...]
    for j in range(N_HEADS):
        sl = slice(j * HEAD_DIM, (j + 1) * HEAD_DIM)
        q_ref[:, sl] = (_rms(z[:, sl], gq) * SOFTMAX2_SCALE).astype(bf16)
    v = z[:, Q_W + KV_W:Q_W + 2 * KV_W]
    for j in range(N_KV_HEADS):
        sl = slice(j * HEAD_DIM, (j + 1) * HEAD_DIM)
        kj = _rms(z[:, Q_W + j * HEAD_DIM:Q_W + (j + 1) * HEAD_DIM], gk)
        kb_ref[:, sl] = kj.astype(bf16)
        head_rows = pl.ds(j, tm, stride=N_KV_HEADS)
        k_ref[head_rows, :] = kj
        v_ref[head_rows, :] = v[:, sl]
    vb_ref[...] = v.astype(bf16)
    vt_ref[...] = v.T.astype(bf16)

    low_half = lax.broadcasted_iota(jnp.int32, (tm, LANES), 1) < IDX_DIM
    qi0 = Q_W + 2 * KV_W
    for p in range(IDX_HEADS // 2):
        pair = z[:, qi0 + p * LANES:qi0 + (p + 1) * LANES]
        rolled = pltpu.roll(pair, IDX_DIM, axis=1)
        for e, (own, other) in enumerate(((pair, rolled), (rolled, pair))):
            dup = jnp.where(low_half, own, other)
            single = jnp.where(low_half, own, 0.0)
            base = (2 * p + e) * IDX_K
            qic_ref[:, base:base + LANES] = dup.astype(bf16)
            qic_ref[:, base + LANES:base + 2 * LANES] = (single - _hi(single)).astype(bf16)
    last = z[:, qi0 + QI_W:qi0 + QI_W + LANES]
    rolled = pltpu.roll(last, IDX_DIM, axis=1)
    kic_ref[:, :LANES] = jnp.where(low_half, _hi(last), rolled - _hi(rolled)).astype(bf16)
    kic_ref[:, LANES:] = jnp.where(low_half, _hi(last), 0.0).astype(bf16)
    ki_ref[...] = last[:, :IDX_DIM]
    kiwi_ref[...] = last


def _att_proj(x, g, w, gq, gk, tm):
    t = x.shape[0]
    row = lambda width: pl.BlockSpec((tm, width), lambda i: (i, 0))
    head_rows = pl.BlockSpec((N_KV_HEADS * tm, HEAD_DIM), lambda i: (i, 0))
    head_rows_shape = jax.ShapeDtypeStruct((N_KV_HEADS * t, HEAD_DIM), f32)
    outs = [(Q_W, bf16), None, None, (KV_W, bf16), (KV_W, bf16),
            (IDX_HEADS * IDX_K, bf16), (IDX_K, bf16), (IDX_DIM, f32), (LANES, f32)]
    return pl.pallas_call(
        _att_proj_kernel,
        grid=(t // tm,),
        in_specs=[row(D_MODEL), _const_spec((1, D_MODEL)), _weight_spec((D_MODEL, ATT_IN)),
                  _const_spec((1, HEAD_DIM)), _const_spec((1, HEAD_DIM))],
        out_specs=[row(o[0]) if o else head_rows for o in outs]
        + [pl.BlockSpec((KV_W, tm), lambda i: (0, i))],
        out_shape=[jax.ShapeDtypeStruct((t, o[0]), o[1]) if o else head_rows_shape for o in outs]
        + [jax.ShapeDtypeStruct((KV_W, t), bf16)],
        scratch_shapes=[pltpu.VMEM((D_MODEL, ATT_IN_PAD), bf16)],
        compiler_params=_params(1),
        name="att_proj",
    )(x, g, w, gq, gk)


def _key_to_float(key):
    bits = jnp.where(key < 0, key ^ jnp.int32(0x7FFFFFFF), key)
    return lax.bitcast_convert_type(bits, f32)


def _count(mask, axis=1):
    ones = jnp.where(mask, 1.0, 0.0)
    if axis == 0:
        n, cols = ones.shape
        rows = 4 * SUBLANES
        ones = jnp.sum(ones.reshape(n // rows, rows, cols), axis=0)
    return jnp.sum(ones, axis=axis, keepdims=True)


def _rank_threshold(score_ref, k_top, axis):
    shape = tuple(1 if a == axis else d for a, d in enumerate(score_ref.shape))

    def step(i, t):
        cand = t + lax.shift_left(jnp.int32(1), jnp.int32(31) - i)
        enough = _count(score_ref[...] >= _key_to_float(cand), axis) >= k_top
        return jnp.where(enough, cand, t)

    key = lax.fori_loop(0, 32, step, jnp.full(shape, INT_MIN, jnp.int32))
    t = _key_to_float(key)
    return t, jnp.where(key == INT_MIN, NEG_INF, t)


def _topk_bias_keys_major(score_ref, bias_ref, tril_ref, k_top):
    n, _ = score_ref.shape
    t, t_above = _rank_threshold(score_ref, k_top, axis=0)
    need = k_top - _count(score_ref[...] > t_above, axis=0)
    ties_before = jnp.zeros(t.shape, f32)
    for c in range(n // LANES):
        sl = slice(c * LANES, (c + 1) * LANES)
        s_c = score_ref[sl, :]
        tie = s_c == t
        tie_f = jnp.where(tie, 1.0, 0.0)
        rank = _dot(tril_ref[...], tie_f.astype(bf16)) + ties_before
        ties_before = ties_before + jnp.sum(tie_f, axis=0, keepdims=True)
        bias_ref[sl, :] = jnp.where((s_c > t_above) | (tie & (rank <= need)), 0.0, NEG_INF)


def _topk_bias(score_ref, tri_ref, before_ref, k_top, store):
    _, n = score_ref.shape
    t, t_above = _rank_threshold(score_ref, k_top, axis=1)
    s = score_ref[...]
    need = k_top - _count(s > t_above)
    tie_b = jnp.where(s == t, 1.0, 0.0).astype(bf16)
    before = _dot(tie_b, before_ref[:n, :])
    for c in range(n // LANES):
        sl = slice(c * LANES, (c + 1) * LANES)
        s_c = s[:, sl]
        rank = _dot(tie_b[:, sl], tri_ref[...]) + before[:, c:c + 1]
        store(c, jnp.where((s_c > t_above) | ((s_c == t) & (rank <= need)), 0.0, NEG_INF))


def _tie_consts(n):
    a = jnp.arange(LANES)
    tri = (a[:, None] <= a[None, :]).astype(bf16)
    before = ((jnp.arange(n)[:, None] // LANES) < a[None, :]).astype(bf16)
    return tri, before


def _indexer_scores(qic, wi, kic):
    rows = qic.shape[0]
    heads = [qic[:, h * IDX_K:(h + 1) * IDX_K] for h in range(IDX_HEADS)]
    if rows % BF16_ROWS == 0:
        logits = _dot_nt(jnp.concatenate(heads, axis=0), kic)
        logits = [logits[h * rows:(h + 1) * rows] for h in range(IDX_HEADS)]
    else:
        logits = [_dot_nt(qh, kic) for qh in heads]
    score = None
    for h in range(IDX_HEADS):
        relu = jnp.maximum(logits[h], 0.0)
        term = wi[:, IDX_DIM + h:IDX_DIM + h + 1] * relu
        score = term if score is None else score + term
    return score * IDX_SCALE


def _dsa_prompt_tile(lk, q0, k_top, q_ref, qic_ref, kiwi_ref, kic_ref, kb_ref, vt_ref, tril_ref,
                     score_scr, bias_scr, s_scr, p_scr, att_scr):
    tq = q_ref.shape[1]
    qic = qic_ref[0]
    q_idx = jnp.concatenate(
        [qic[:, h * IDX_K:(h + 1) * IDX_K] for h in range(IDX_HEADS)], axis=0)
    logits = _dot_nt(kic_ref[0, :lk, :], q_idx)
    wi_t = kiwi_ref[0].T
    score = None
    for h in range(IDX_HEADS):
        term = wi_t[IDX_DIM + h:IDX_DIM + h + 1, :] * jnp.maximum(logits[:, h * tq:(h + 1) * tq], 0.0)
        score = term if score is None else score + term
    score = score * IDX_SCALE
    q_pos = q0 + lax.broadcasted_iota(jnp.int32, (1, tq), 1)
    admissible = lax.broadcasted_iota(jnp.int32, (lk, 1), 0) <= q_pos
    scores = score_scr.at[:lk, :]
    scores[...] = jnp.where(admissible, score, NEG_INF)

    bias_ref = bias_scr.at[:lk, :]
    _topk_bias_keys_major(scores, bias_ref, tril_ref, k_top)

    q = q_ref[0]
    bias = jnp.concatenate([bias_ref[...]] * GROUP, axis=1)
    for kvh in range(N_KV_HEADS):
        qs = jnp.concatenate(
            [q[:, (kvh * GROUP + g) * HEAD_DIM:(kvh * GROUP + g + 1) * HEAD_DIM]
             for g in range(GROUP)], axis=0)
        kv_sl = slice(kvh * HEAD_DIM, (kvh + 1) * HEAD_DIM)
        s_scr[kvh, :lk, :] = _dot_nt(kb_ref[0, :lk, kv_sl], qs) + bias

    for kvh in range(N_KV_HEADS):
        kv_sl = slice(kvh * HEAD_DIM, (kvh + 1) * HEAD_DIM)
        denoms = []
        for g in range(GROUP):
            cols = slice(g * tq, (g + 1) * tq)
            m = jnp.max(s_scr[kvh, :lk, cols], axis=0, keepdims=True)
            p = jnp.exp2(s_scr[kvh, :lk, cols] - m)
            denoms.append(jnp.sum(p, axis=0, keepdims=True))
            p_scr[kvh, :lk, cols] = p.astype(bf16)
        o_t = _dot(vt_ref[kv_sl, :lk], p_scr[kvh, :lk, :])
        o_t = o_t / jnp.concatenate(denoms, axis=1)
        for g in range(GROUP):
            col = (kvh * GROUP + g) * HEAD_DIM
            att_scr[:, col:col + HEAD_DIM] = o_t[:, g * tq:(g + 1) * tq].T.astype(bf16)


def _dsa_prompt_kernel(q_ref, qic_ref, kiwi_ref, kic_ref, kb_ref, vt_ref, x_ref, wo_ref,
                       tril_ref, o_ref, wo_b, *scratch, k_top, key_counts):
    _cast_weights_once((pl.program_id(0) == 0) & (pl.program_id(1) == 0), [(wo_ref, wo_b)])
    tq = q_ref.shape[1]
    q0 = pl.program_id(1) * tq
    q_end = q0 + tq
    for lower, lk in zip((0,) + key_counts[:-1], key_counts):
        @pl.when((q_end > lower) & (q_end <= lk))
        def _(lk=lk):
            _dsa_prompt_tile(lk, q0, k_top, q_ref, qic_ref, kiwi_ref, kic_ref, kb_ref, vt_ref,
                             tril_ref, *scratch)
    att_scr = scratch[-1]
    o_ref[0] = x_ref[0] + _dot(att_scr[...], wo_b[...])


def _dsa_prompt(q, qic, kiwi, kic, kb, vt, x, wo, tq, key_counts):
    b, t, _ = q.shape
    k_top = min(TOPK_MAX, t // 4)
    a = jnp.arange(LANES)
    tril = (a[None, :] <= a[:, None]).astype(bf16)
    cols = GROUP * tq
    tile = lambda width: pl.BlockSpec((1, tq, width), lambda i, j: (i, j, 0))
    full = lambda width: pl.BlockSpec((1, t, width), lambda i, j: (i, 0, 0))
    return pl.pallas_call(
        functools.partial(_dsa_prompt_kernel, k_top=k_top, key_counts=key_counts),
        grid=(b, t // tq),
        in_specs=[tile(Q_W), tile(IDX_HEADS * IDX_K), tile(LANES), full(IDX_K), full(KV_W),
                  pl.BlockSpec((KV_W, t), lambda i, j: (0, i)), tile(D_MODEL),
                  _weight_spec((Q_W, D_MODEL)), _const_spec((LANES, LANES))],
        out_specs=tile(D_MODEL),
        out_shape=jax.ShapeDtypeStruct((b, t, D_MODEL), f32),
        scratch_shapes=[pltpu.VMEM((Q_W, D_MODEL), bf16),
                        pltpu.VMEM((t, tq), f32),
                        pltpu.VMEM((t, tq), f32),
                        pltpu.VMEM((N_KV_HEADS, t, cols), f32),
                        pltpu.VMEM((N_KV_HEADS, t, cols), bf16),
                        pltpu.VMEM((tq, Q_W), bf16)],
        compiler_params=_params(2),
        name="dsa_prompt",
    )(q, qic, kiwi, kic, kb, vt, x, wo, tril)


def _sample_scores_kernel(pt_ref, qic_ref, kiwi_ref, kicn_ref, cache_ref, o_ref, buf, sem):
    b = pl.program_id(0)
    n_pages = pt_ref.shape[1]
    n_past = n_pages * PAGE_SIZE
    t = qic_ref.shape[0]

    def page_copies(batch, slot):
        return [pltpu.make_async_copy(
            cache_ref.at[pt_ref[batch, p]],
            buf.at[slot, :, pl.ds(p * PAGE_SIZE, PAGE_SIZE)], sem.at[slot])
            for p in range(n_pages)]

    slot = lax.rem(b, 2)

    @pl.when(b == 0)
    def _():
        for copy in page_copies(0, 0):
            copy.start()

    @pl.when(b + 1 < pl.num_programs(0))
    def _():
        for copy in page_copies(b + 1, 1 - slot):
            copy.start()

    for copy in page_copies(b, slot):
        copy.wait()

    keys_t = buf[slot]
    k_hi = keys_t.astype(bf16)
    k_lo = (keys_t - k_hi.astype(f32)).astype(bf16)
    qic = qic_ref[...].astype(f32)

    def stacked(offset):
        return jnp.concatenate(
            [qic[:, h * IDX_K + offset:h * IDX_K + offset + IDX_DIM] for h in range(IDX_HEADS)],
            axis=0).astype(bf16)

    q_hi, q_lo = stacked(0), stacked(2 * IDX_DIM)
    logits = _dot(q_hi, k_hi) + _dot(q_hi, k_lo) + _dot(q_lo, k_hi)
    wi = kiwi_ref[...]
    score = None
    for h in range(IDX_HEADS):
        term = wi[:, IDX_DIM + h:IDX_DIM + h + 1] * jnp.maximum(logits[h * t:(h + 1) * t], 0.0)
        score = term if score is None else score + term
    o_ref[:, :n_past] = score * IDX_SCALE
    new = _indexer_scores(qic_ref[...], wi, kicn_ref[...])
    o_ref[:, n_past:] = jnp.concatenate([new, jnp.zeros((t, LANES - t), f32)], axis=1)


def _sample_scores(page_table, qic, kiwi, kic_new, cache_ki_t):
    nb, n_pages = page_table.shape
    t = qic.shape[0] // nb
    n_past = n_pages * PAGE_SIZE
    row = lambda width: pl.BlockSpec((t, width), lambda b, pt: (b, 0))
    return pl.pallas_call(
        _sample_scores_kernel,
        grid_spec=pltpu.PrefetchScalarGridSpec(
            num_scalar_prefetch=1,
            grid=(nb,),
            in_specs=[row(IDX_HEADS * IDX_K), row(LANES), row(IDX_K),
                      pl.BlockSpec(memory_space=pl.ANY)],
            out_specs=row(n_past + LANES),
            scratch_shapes=[pltpu.VMEM((2, IDX_DIM, n_past), f32),
                            pltpu.SemaphoreType.DMA((2,))],
        ),
        out_shape=jax.ShapeDtypeStruct((nb * t, n_past + LANES), f32),
        compiler_params=_params(1),
        name="dsa_sample_scores",
    )(page_table, qic, kiwi, kic_new, cache_ki_t)


def _sample_select_kernel(sc_ref, tri_ref, before_ref, o_ref, score_scr, *, k_top, t, n_past):
    rows, n = sc_ref.shape
    qi = lax.broadcasted_iota(jnp.int32, (rows, n), 0) & (t - 1)
    kj = lax.broadcasted_iota(jnp.int32, (rows, n), 1)
    admissible = kj <= n_past + qi
    score_scr[...] = jnp.where(admissible, sc_ref[...], NEG_INF)

    def store(c, piece):
        o_ref[:, c * LANES:(c + 1) * LANES] = piece

    _topk_bias(score_scr, tri_ref, before_ref, k_top, store)


def _sample_select(scores, t, n_past, rows_per_step):
    n_rows, n_all = scores.shape
    assert t & (t - 1) == 0 and rows_per_step % t == 0, (t, rows_per_step)
    k_top = min(TOPK_MAX, (n_past + t) // 4)
    tri, before = _tie_consts(n_all)
    block = pl.BlockSpec((rows_per_step, n_all), lambda i: (i, 0))
    return pl.pallas_call(
        functools.partial(_sample_select_kernel, k_top=k_top, t=t, n_past=n_past),
        grid=(n_rows // rows_per_step,),
        in_specs=[block, _const_spec((LANES, LANES)), _const_spec((n_all, LANES))],
        out_specs=block,
        out_shape=jax.ShapeDtypeStruct((n_rows, n_all), f32),
        scratch_shapes=[pltpu.VMEM((rows_per_step, n_all), f32)],
        compiler_params=_params(1),
        name="dsa_sample_select",
    )(scores, tri, before)


def _sample_attend_kernel(pt_ref, bias_ref, biasn_ref, q_ref, kbn_ref, vbn_ref, x_ref, wo_ref,
                          ck_ref, cv_ref, o_ref, kbuf, vbuf, ksem, vsem, m_scr, l_scr, acc_scr,
                          wo_b, *, pps):
    b, j = pl.program_id(0), pl.program_id(1)
    n_j = pl.num_programs(1)
    step = b * n_j + j
    _cast_weights_once(step == 0, [(wo_ref, wo_b)])
    t = q_ref.shape[0]
    page_rows = PAGE_SIZE * N_KV_HEADS

    def page_copies(batch, chunk, slot):
        copies = []
        for i in range(pps):
            src = pl.ds(pl.multiple_of(pt_ref[batch, chunk * pps + i] * page_rows, page_rows),
                        page_rows)
            dst = pl.ds(i * page_rows, page_rows)
            copies.append(pltpu.make_async_copy(ck_ref.at[src, :], kbuf.at[slot, dst, :], ksem.at[slot]))
            copies.append(pltpu.make_async_copy(cv_ref.at[src, :], vbuf.at[slot, dst, :], vsem.at[slot]))
        return copies

    slot = lax.rem(step, 2)

    @pl.when(step == 0)
    def _():
        for copy in page_copies(0, 0, 0):
            copy.start()

    @pl.when(step + 1 < pl.num_programs(0) * n_j)
    def _():
        nxt = step + 1
        for copy in page_copies(nxt // n_j, lax.rem(nxt, n_j), 1 - slot):
            copy.start()

    @pl.when(j == 0)
    def _():
        m_scr[...] = jnp.full(m_scr.shape, -1e30, f32)
        l_scr[...] = jnp.zeros(l_scr.shape, f32)
        acc_scr[...] = jnp.zeros(acc_scr.shape, f32)

    q = q_ref[...]

    rows = GROUP * t

    def attend(k_heads, v_heads, bias):
        s = jnp.concatenate(
            [_dot_nt(jnp.concatenate(
                [q[:, (kvh * GROUP + g) * HEAD_DIM:(kvh * GROUP + g + 1) * HEAD_DIM]
                 for g in range(GROUP)], axis=0), k_heads[kvh]) for kvh in range(N_KV_HEADS)],
            axis=0)
        s = s + jnp.concatenate([bias] * N_HEADS, axis=0)
        m_old = m_scr[...]
        m_new = jnp.maximum(m_old, jnp.max(s, axis=-1, keepdims=True))
        alpha = jnp.exp2(m_old - m_new)
        p = jnp.exp2(s - m_new)
        l_scr[...] = alpha * l_scr[...] + jnp.sum(p, axis=-1, keepdims=True)
        pv = jnp.concatenate(
            [_dot(p[kvh * rows:(kvh + 1) * rows].astype(bf16), v_heads[kvh])
             for kvh in range(N_KV_HEADS)], axis=0)
        acc_scr[...] = alpha * acc_scr[...] + pv
        m_scr[...] = m_new

    for copy in page_copies(b, j, slot):
        copy.wait()

    def head_rows(buf, kvh):
        return buf[slot, pl.ds(kvh, pps * PAGE_SIZE, stride=N_KV_HEADS), :].astype(bf16)

    attend([head_rows(kbuf, kvh) for kvh in range(N_KV_HEADS)],
           [head_rows(vbuf, kvh) for kvh in range(N_KV_HEADS)], bias_ref[...])

    @pl.when(j == n_j - 1)
    def _():
        pad = jnp.zeros((LANES - t, HEAD_DIM), bf16)

        def new_heads(ref):
            return [jnp.concatenate([ref[:, kvh * HEAD_DIM:(kvh + 1) * HEAD_DIM], pad], axis=0)
                    for kvh in range(N_KV_HEADS)]

        attend(new_heads(kbn_ref), new_heads(vbn_ref), biasn_ref[...])
        o = acc_scr[...] / l_scr[...]
        att = jnp.concatenate([o[h * t:(h + 1) * t] for h in range(N_HEADS)],
                              axis=1).astype(bf16)
        o_ref[...] = x_ref[...] + _dot(att, wo_b[...])


def _sample_attend(page_table, bias, q, kb_new, vb_new, x, wo, cache_k, cache_v, pages_per_step):
    nb, n_pages = page_table.shape
    t = q.shape[0] // nb
    steps = n_pages // pages_per_step
    span = pages_per_step * PAGE_SIZE
    row = lambda width: pl.BlockSpec((t, width), lambda b, j, pt: (b, 0))
    rows = N_HEADS * t
    buf = pltpu.VMEM((2, span * N_KV_HEADS, HEAD_DIM), f32)
    return pl.pallas_call(
        functools.partial(_sample_attend_kernel, pps=pages_per_step),
        grid_spec=pltpu.PrefetchScalarGridSpec(
            num_scalar_prefetch=1,
            grid=(nb, steps),
            in_specs=[pl.BlockSpec((t, span), lambda b, j, pt: (b, j)),
                      pl.BlockSpec((t, LANES), lambda b, j, pt: (b, n_pages)),
                      row(Q_W), row(KV_W), row(KV_W), row(D_MODEL),
                      _weight_spec((Q_W, D_MODEL)),
                      pl.BlockSpec(memory_space=pl.ANY), pl.BlockSpec(memory_space=pl.ANY)],
            out_specs=row(D_MODEL),
            scratch_shapes=[buf, buf, pltpu.SemaphoreType.DMA((2,)), pltpu.SemaphoreType.DMA((2,)),
                            pltpu.VMEM((rows, 1), f32),
                            pltpu.VMEM((rows, 1), f32),
                            pltpu.VMEM((rows, HEAD_DIM), f32),
                            pltpu.VMEM((Q_W, D_MODEL), bf16)],
        ),
        out_shape=jax.ShapeDtypeStruct((nb * t, D_MODEL), f32),
        compiler_params=_params(2),
        name="dsa_sample_attend",
    )(page_table, bias, bias, q, kb_new, vb_new, x, wo, cache_k, cache_v)


def _ffn_kernel(x_ref, g_ref, wgu_ref, wd_ref, o_ref, *, n_ff):
    x = x_ref[...]
    h = _rms(x, g_ref[...]).astype(bf16)
    acc = x
    width = D_FF // n_ff
    for c in range(n_ff):
        gate = _dot(h, wgu_ref[0, :, c * width:(c + 1) * width])
        up = _dot(h, wgu_ref[0, :, D_FF + c * width:D_FF + (c + 1) * width])
        acc = acc + _dot((jax.nn.silu(gate) * up).astype(bf16),
                         wd_ref[0, c * width:(c + 1) * width, :])
    o_ref[...] = acc


def _ffn(x, g, wgu, wd, layer, tm, n_ff):
    t = x.shape[0]
    row = pl.BlockSpec((tm, D_MODEL), lambda i: (i, 0))
    resident = lambda a: pl.BlockSpec((1,) + a.shape[1:], lambda i: (layer, 0, 0),
                                      pipeline_mode=pl.Buffered(1))
    return pl.pallas_call(
        functools.partial(_ffn_kernel, n_ff=n_ff),
        grid=(t // tm,),
        in_specs=[row, _const_spec((1, D_MODEL)), resident(wgu), resident(wd)],
        out_specs=row,
        out_shape=jax.ShapeDtypeStruct((t, D_MODEL), f32),
        compiler_params=_params(1),
        name="ffn",
    )(x, g, wgu, wd)


def _sgu_kernel(x_ref, g_ref, win_ref, gv_ref, ws_ref, bias_ref, wout_ref, *refs,
                chunk_len, emit_v):
    if emit_v:
        o_ref, v_ref, gated_scr, win_b, wout_b = refs
    else:
        o_ref, gated_scr, win_b, wout_b = refs
    _cast_weights_once(pl.program_id(0) == 0, [(win_ref, win_b), (wout_ref, wout_b)])
    tm = x_ref.shape[0]
    n_sub = tm // CHUNK
    x = x_ref[...]
    h = _rms(x, g_ref[...]).astype(bf16)
    z = jax.nn.gelu(_dot(h, win_b[...]))
    u = z[:, :D_MODEL]
    v = _rms(z[:, D_MODEL:], gv_ref[...])
    if emit_v:
        v_ref[...] = v
    vb = v.astype(bf16)
    ti = lax.broadcasted_iota(jnp.int32, (CHUNK, CHUNK), 0)
    si = lax.broadcasted_iota(jnp.int32, (CHUNK, CHUNK), 1)
    shift = chunk_len.bit_length() - 1
    causal = (si <= ti) & (lax.shift_right_logical(si, shift) == lax.shift_right_logical(ti, shift))
    bias = bias_ref[...]
    for grp in range(SG_GROUPS):
        cols = slice(grp * LANES, (grp + 1) * LANES)
        w = jnp.where(causal, ws_ref[grp], 0.0).astype(bf16)
        vg = jnp.concatenate([vb[c * CHUNK:(c + 1) * CHUNK, cols] for c in range(n_sub)], axis=1)
        mixed = _dot(w, vg) + bias[:, grp:grp + 1]
        for c in range(n_sub):
            rows = slice(c * CHUNK, (c + 1) * CHUNK)
            gated_scr[rows, cols] = (u[rows, cols] * mixed[:, c * LANES:(c + 1) * LANES]).astype(bf16)
    o_ref[...] = x + _dot(gated_scr[...], wout_b[...])


def _sgu(x, g, w_in, g_v, w_s, bias_t, w_out, tm, chunk_len, emit_v):
    t = x.shape[0]
    row = pl.BlockSpec((tm, D_MODEL), lambda i: (i, 0))
    out_shape = [jax.ShapeDtypeStruct((t, D_MODEL), f32)] * (2 if emit_v else 1)
    return pl.pallas_call(
        functools.partial(_sgu_kernel, chunk_len=chunk_len, emit_v=emit_v),
        grid=(t // tm,),
        in_specs=[row, _const_spec((1, D_MODEL)), _weight_spec(w_in.shape),
                  _const_spec((1, D_MODEL)), _const_spec(w_s.shape), _const_spec(bias_t.shape),
                  _weight_spec(w_out.shape)],
        out_specs=[row] * len(out_shape),
        out_shape=out_shape,
        scratch_shapes=[pltpu.VMEM((tm, D_MODEL), bf16), pltpu.VMEM(w_in.shape, bf16),
                        pltpu.VMEM(w_out.shape, bf16)],
        compiler_params=_params(1),
        name="sgu",
    )(x, g, w_in, g_v, w_s, bias_t, w_out)


def kernel(x_prompt, x_sample, cache_k, cache_v, cache_kidx, page_table, norm_mix, norm_ffn,
           att_w_in, att_q_norm, att_k_norm, att_w_out, sg_w_in, sg_v_norm, sg_w_s, sg_b_s,
           sg_w_out, ffn_w_gu, ffn_w_down):
    nb_p, t_p, _ = x_prompt.shape
    nb_s, t_s, _ = x_sample.shape
    n_pool = cache_k.shape[1]
    xp = x_prompt.reshape(nb_p * t_p, D_MODEL)
    xs = x_sample.reshape(nb_s * t_s, D_MODEL)
    vec = lambda a: a.reshape(1, -1)

    w_out = att_w_out[0]
    proj = functools.partial(_att_proj, g=vec(norm_mix[0]), w=att_w_in[0],
                             gq=vec(att_q_norm[0]), gk=vec(att_k_norm[0]))
    q_p, k_p, v_p, kb_p, _, qic_p, kic_p, ki_p, kiwi_p, vt_p = proj(xp, tm=512)
    q_s, k_s, v_s, kb_s, vb_s, qic_s, kic_s, ki_s, kiwi_s, _ = proj(xs, tm=nb_s * t_s)

    per_seq = lambda a: a.reshape(nb_p, t_p, a.shape[-1])
    xp = _dsa_prompt(per_seq(q_p), per_seq(qic_p), per_seq(kiwi_p), per_seq(kic_p),
                     per_seq(kb_p), vt_p, per_seq(xp), w_out, tq=128,
                     key_counts=(256, 512, 768, 1024, 1280, 1536, 1792, t_p))
    xp = xp.reshape(nb_p * t_p, D_MODEL)

    scores_s = _sample_scores(page_table, qic_s, kiwi_s, kic_s, jnp.swapaxes(cache_kidx[0], 1, 2))
    bias_s = _sample_select(scores_s, t_s, page_table.shape[1] * PAGE_SIZE, rows_per_step=64)
    kv_rows = n_pool * PAGE_SIZE * N_KV_HEADS
    xs = _sample_attend(page_table, bias_s, q_s, kb_s, vb_s, xs, w_out,
                        cache_k[0].reshape(kv_rows, HEAD_DIM),
                        cache_v[0].reshape(kv_rows, HEAD_DIM), pages_per_step=64)

    ffn_gu, ffn_down = ffn_w_gu.astype(bf16), ffn_w_down.astype(bf16)

    def ffn(x, layer, tm):
        return _ffn(x, vec(norm_ffn[layer]), ffn_gu, ffn_down, layer, tm, n_ff=11)

    xp = ffn(xp, 0, 512)
    xs = ffn(xs, 0, nb_s * t_s)

    sgu = functools.partial(_sgu, g=vec(norm_mix[1]), w_in=sg_w_in[0], g_v=vec(sg_v_norm[0]),
                            w_out=sg_w_out[0])
    (xp,) = sgu(xp, w_s=sg_w_s[0], bias_t=sg_b_s[0].T, tm=512, chunk_len=CHUNK, emit_v=False)
    reps = CHUNK // t_s
    xs, sgv = sgu(xs, w_s=jnp.tile(sg_w_s[0][:, :t_s, :t_s], (1, reps, reps)),
                  bias_t=jnp.tile(sg_b_s[0][:, :t_s], (1, reps)).T, tm=nb_s * t_s,
                  chunk_len=t_s, emit_v=True)

    xp = ffn(xp, 1, 512)
    xs = ffn(xs, 1, nb_s * t_s)

    y_prompt = xp.reshape(nb_p, t_p, D_MODEL)
    y_sample = xs.reshape(nb_s, t_s, D_MODEL)
    kv_p = lambda a: a.reshape(1, nb_p, t_p, N_KV_HEADS, HEAD_DIM)
    kv_s = lambda a: a.reshape(1, nb_s, t_s, N_KV_HEADS, HEAD_DIM)
    return (y_prompt, y_sample, kv_p(k_p), kv_p(v_p), ki_p.reshape(1, nb_p, t_p, IDX_DIM),
            kv_s(k_s), kv_s(v_s), ki_s.reshape(1, nb_s, t_s, IDX_DIM),
            sgv.reshape(1, nb_s, t_s, D_MODEL))
```
